```python
import math
import jax, jax.numpy as jnp
from jax import lax
import numpy as np

D_MODEL = 2048
BATCH = 2
SEQ = 16384
DEPTH = 2
DEC_BATCH = 4
DEC_SEQ = 8192
PAST_LEN = 128

HGRN_DIM = 128
HGRN_HEADS = (D_MODEL // 2) // HGRN_DIM
HGRN_WIDTH = HGRN_HEADS * HGRN_DIM
HGRN_CHUNK = 64
RWKV_DIM = 64
RWKV_HEADS = (D_MODEL // 2) // RWKV_DIM
RWKV_WIDTH = RWKV_HEADS * RWKV_DIM
DECAY_LORA = 64
AAA_LORA = 64
GATE_LORA = 160
RWKV_COLS = 3 * RWKV_WIDTH + 2 * DECAY_LORA + AAA_LORA + GATE_LORA
RWKV_SPLITS = [RWKV_WIDTH, 2 * RWKV_WIDTH, 3 * RWKV_WIDTH, 3 * RWKV_WIDTH + DECAY_LORA,
               3 * RWKV_WIDTH + 2 * DECAY_LORA, 3 * RWKV_WIDTH + 2 * DECAY_LORA + AAA_LORA]
MIX_COLS = 5 * HGRN_WIDTH + RWKV_COLS
ATT_DIM = 128
ATT_HEADS = D_MODEL // ATT_DIM
ATT_KV_HEADS = ATT_HEADS // 4
ATT_GROUP = ATT_HEADS // ATT_KV_HEADS
QKV_COLS = (ATT_HEADS + 2 * ATT_KV_HEADS) * ATT_DIM
WINDOW = 128
BLOCK = 128
ROPE_THETA = 10000.0
D_FF = 256 * ((8 * D_MODEL // 3 + 255) // 256)
CONV_WIDTH = 3
N_AB = (DEPTH + 1) // 2
N_ATT = DEPTH // 2
RMS_EPS = 1e-6
GN_EPS = 64e-5

kernel_name = 'hybrid_hgrn2_rwkv7_swa_convffn_encoder'


def rmsnorm(x, g):
    xf = x.astype(jnp.float32)
    y = xf * lax.rsqrt(jnp.mean(xf * xf, axis=-1, keepdims=True) + RMS_EPS)
    return (y * g.astype(jnp.float32)).astype(x.dtype)


def flip_seq(t):
    return jnp.flip(t, axis=1)


def centred_shift(p):
    pp = jnp.pad(p, ((0, 0), (1, 1), (0, 0)))
    return 0.5 * (pp[:, :-2] + pp[:, 2:])


def hgrn2_scan(q, k, v, logf):
    B, T, H, Dk = q.shape
    Dv = v.shape[-1]
    C = HGRN_CHUNK
    N = T // C

    def blk(t):
        return t.reshape(B, N, C, H, t.shape[-1]).transpose(1, 0, 3, 2, 4)

    q, k, v, logf = blk(q), blk(k), blk(v), blk(logf)
    b = jnp.cumsum(logf, axis=3)
    b_last = b[:, :, :, -1:, :]
    q_d = q * jnp.exp(b)
    k_d = k * jnp.exp(-b)
    causal = jnp.tril(jnp.ones((C, C), dtype=bool))
    att = jnp.where(causal, jnp.einsum('nbhid,nbhjd->nbhij', q_d, k_d), 0.0)
    o_intra = jnp.einsum('nbhij,nbhje->nbhie', att, v)
    upd = jnp.einsum('nbhjd,nbhje->nbhde', k * jnp.exp(b_last - b), v)
    dec = jnp.exp(b_last[:, :, :, 0, :])

    def step(S, inp):
        d, u = inp
        return S * d[..., None] + u, S

    S0 = jnp.zeros((B, H, Dk, Dv), jnp.float32)
    _, S_prev = lax.scan(step, S0, (dec, upd))
    o = o_intra + jnp.einsum('nbhid,nbhde->nbhie', q_d, S_prev)
    return o.transpose(1, 0, 3, 2, 4).reshape(B, T, H, Dv)


def rwkv7_scan(r, w, k, v, a, b):
    B, T, H, N = r.shape

    def step(S, inp):
        r_t, w_t, k_t, v_t, a_t, b_t = inp
        sa = jnp.einsum('bhij,bhj->bhi', S, a_t)
        S = S * w_t[:, :, None, :] + sa[..., None] * b_t[:, :, None, :] + v_t[..., None] * k_t[:, :, None, :]
        return S, jnp.einsum('bhij,bhj->bhi', S, r_t)

    S0 = jnp.zeros((B, H, N, N), jnp.float32)
    xs = (jnp.moveaxis(r, 1, 0), jnp.moveaxis(w, 1, 0), jnp.moveaxis(k, 1, 0),
          jnp.moveaxis(v, 1, 0), jnp.moveaxis(a, 1, 0), jnp.moveaxis(b, 1, 0))
    _, y = lax.scan(step, S0, xs)
    return jnp.moveaxis(y, 0, 1)


def hgrn_rwkv_mixer(h, layer, w_in, lb_table, o_norm, mu, w0, w2, a0, a2, g2, k_k, k_a, r_k, ln_w, ln_b, w_out):
    B, T, _ = h.shape
    f32 = jnp.float32
    proj = (h @ w_in).astype(f32)
    pa = proj[..., :5 * HGRN_WIDTH]
    pb = proj[..., 5 * HGRN_WIDTH:]

    qa, ia, zf_fwd, zf_bwd, ga = jnp.split(pa, 5, axis=-1)
    lb = jnp.cumsum(jax.nn.softmax(lb_table.astype(f32), axis=0), axis=0)[layer]

    def ha(t):
        return t.reshape(B, T, HGRN_HEADS, HGRN_DIM)

    def forget(z):
        f = lb + (1.0 - lb) * jax.nn.sigmoid(z)
        return ha(1.0 - f), ha(jnp.log(f))

    k_fwd, lf_fwd = forget(zf_fwd)
    k_bwd, lf_bwd = forget(zf_bwd)
    qh, ih = ha(qa), ha(ia)
    oa = hgrn2_scan(qh, k_fwd, ih, lf_fwd) + flip_seq(
        hgrn2_scan(flip_seq(qh), flip_seq(k_bwd), flip_seq(ih), flip_seq(lf_bwd)))
    oa = oa * lax.rsqrt(jnp.mean(oa * oa, axis=-1, keepdims=True) + RMS_EPS)
    ya = oa.reshape(B, T, HGRN_WIDTH) * o_norm.astype(f32) * jax.nn.silu(ga)

    pb = pb + mu * (centred_shift(pb) - pb)
    r, k, v, wd_fwd, wd_bwd, ad, gd = jnp.split(pb, RWKV_SPLITS, axis=-1)

    def hb(t):
        return t.reshape(B, T, RWKV_HEADS, RWKV_DIM)

    def decay(wd, w0_, w2_):
        w_log = -jax.nn.softplus(-(w0_ + jnp.tanh(wd) @ w2_)) - 0.5
        return hb(jnp.exp(-jnp.exp(w_log)))

    w_fwd = decay(wd_fwd, w0[0], w2[0])
    w_bwd = decay(wd_bwd, w0[1], w2[1])
    a = jax.nn.sigmoid(a0 + ad @ a2)
    g = jax.nn.sigmoid(gd) @ g2
    kk = hb(k * k_k)
    kk = kk / jnp.maximum(jnp.sqrt(jnp.sum(kk * kk, axis=-1, keepdims=True)), 1e-12)
    k = k * (1.0 + (a - 1.0) * k_a)
    rh, kh, vh, ah = hb(r), hb(k), hb(v), hb(a)
    a_vec = -kk
    b_vec = kk * ah
    yb = rwkv7_scan(rh, w_fwd, kh, vh, a_vec, b_vec) + flip_seq(
        rwkv7_scan(flip_seq(rh), flip_seq(w_bwd), flip_seq(kh), flip_seq(vh), flip_seq(a_vec), flip_seq(b_vec)))
    mean = jnp.mean(yb, axis=-1, keepdims=True)
    var = jnp.mean(jnp.square(yb - mean), axis=-1, keepdims=True)
    yb = ((yb - mean) * lax.rsqrt(var + GN_EPS)).reshape(B, T, RWKV_WIDTH) * ln_w + ln_b
    bonus = (jnp.sum(rh * kh * r_k, axis=-1, keepdims=True) * vh).reshape(B, T, RWKV_WIDTH)
    yb = (yb + bonus) * g

    y = jnp.concatenate([ya, yb], axis=-1).astype(h.dtype)
    return y @ w_out


def rope(x):
    T = x.shape[1]
    half = ATT_DIM // 2
    inv = ROPE_THETA ** (-jnp.arange(half, dtype=jnp.float32) / half)
    ang = jnp.arange(T, dtype=jnp.float32)[:, None] * inv[None, :]
    cos = jnp.cos(ang)[:, None, :]
    sin = jnp.sin(ang)[:, None, :]
    xf = x.astype(jnp.float32)
    x1, x2 = xf[..., :half], xf[..., half:]
    return jnp.concatenate([x1 * cos - x2 * sin, x2 * cos + x1 * sin], axis=-1).astype(x.dtype)


def band_windows(t, nb):
    B, T, KV, D = t.shape
    tp = jnp.pad(t, ((0, 0), (BLOCK, BLOCK), (0, 0), (0, 0))).reshape(B, nb + 2, BLOCK, KV, D)
    return jnp.concatenate([tp[:, :-2], tp[:, 1:-1], tp[:, 2:]], axis=2)


def window_attention_mixer(h, w_qkv, sink, w_o):
    B, T, _ = h.shape
    qkv = h @ w_qkv
    q = qkv[..., :ATT_HEADS * ATT_DIM].reshape(B, T, ATT_HEADS, ATT_DIM)
    k = qkv[..., ATT_HEADS * ATT_DIM:(ATT_HEADS + ATT_KV_HEADS) * ATT_DIM].reshape(B, T, ATT_KV_HEADS, ATT_DIM)
    v = qkv[..., (ATT_HEADS + ATT_KV_HEADS) * ATT_DIM:].reshape(B, T, ATT_KV_HEADS, ATT_DIM)
    q, k = rope(q), rope(k)
    nb = T // BLOCK
    qb = q.reshape(B, nb, BLOCK, ATT_KV_HEADS, ATT_GROUP, ATT_DIM)
    kw = band_windows(k, nb)
    vw = band_windows(v, nb)
    s = jnp.einsum('bnqkgd,bnskd->bnkgqs', qb, kw).astype(jnp.float32) * (ATT_DIM ** -0.5)
    qpos = jnp.arange(nb)[:, None, None] * BLOCK + jnp.arange(BLOCK)[None, :, None]
    kpos = (jnp.arange(nb)[:, None, None] - 1) * BLOCK + jnp.arange(3 * BLOCK)[None, None, :]
    valid = (jnp.abs(kpos - qpos) <= WINDOW) & (kpos >= 0) & (kpos < T)
    s = jnp.where(valid[None, :, None, None], s, -jnp.inf)
    sk = sink.astype(jnp.float32).reshape(ATT_KV_HEADS, ATT_GROUP)[None, None, :, :, None, None]
    m = jnp.maximum(jnp.max(s, axis=-1, keepdims=True), sk)
    p = jnp.exp(s - m)
    p = p / (jnp.sum(p, axis=-1, keepdims=True) + jnp.exp(sk - m))
    o = jnp.einsum('bnkgqs,bnskd->bnqkgd', p.astype(v.dtype), vw)
    return o.reshape(B, T, ATT_HEADS * ATT_DIM) @ w_o


def conv_ffn(h, w_up, conv_w, conv_b, w_down):
    u = h @ w_up
    gate, val = u[..., :D_FF], u[..., D_FF:]
    gp = jnp.pad(gate, ((0, 0), (1, 1), (0, 0)))
    gate = gp[:, :-2] * conv_w[0] + gp[:, 1:-1] * conv_w[1] + gp[:, 2:] * conv_w[2] + conv_b
    return (jax.nn.silu(gate) * val) @ w_down


def trunk(x, p):
    for layer in range(DEPTH):
        j = layer // 2
        h = rmsnorm(x, p['mix_norm'][layer])
        if layer % 2 == 0:
            x = x + hgrn_rwkv_mixer(h, layer, p['ab_w_in'][j], p['hgrn_lb'], p['hgrn_onorm'][j],
                                    p['rwkv_mu'][j], p['rwkv_w0'][j], p['rwkv_w2'][j], p['rwkv_a0'][j],
                                    p['rwkv_a2'][j], p['rwkv_g2'][j], p['rwkv_kk'][j], p['rwkv_ka'][j],
                                    p['rwkv_rk'][j], p['rwkv_ln_w'][j], p['rwkv_ln_b'][j], p['ab_w_out'][j])
        else:
            x = x + window_attention_mixer(h, p['att_w_qkv'][j], p['att_sink'][j], p['att_w_o'][j])
        x = x + conv_ffn(rmsnorm(x, p['ffn_norm'][layer]), p['ffn_w_up'][layer], p['ffn_conv_w'][layer],
                         p['ffn_conv_b'][layer], p['ffn_w_down'][layer])
    return rmsnorm(x, p['final_norm'])


def setup_inputs(seed: int = 0) -> dict:
    key = jax.random.key(seed)
    ks = iter(list(jax.random.split(key, 40)))
    D = D_MODEL

    def nrm(shape, scale):
        return scale * jax.random.normal(next(ks), shape, jnp.float32)

    return {
        'x_prompt': nrm((BATCH, SEQ, D), 1.0),
        'x_sample': nrm((DEC_BATCH, DEC_SEQ, D), 1.0),
        'mix_norm': 1.0 + nrm((DEPTH, D), 0.02),
        'ab_w_in': nrm((N_AB, D, MIX_COLS), D ** -0.5),
        'hgrn_lb': nrm((DEPTH + 1, HGRN_WIDTH), 0.1),
        'hgrn_onorm': 1.0 + nrm((N_AB, HGRN_WIDTH), 0.02),
        'rwkv_mu': jax.random.uniform(next(ks), (N_AB, RWKV_COLS), jnp.float32),
        'rwkv_w0': -1.0 + nrm((N_AB, 2, RWKV_WIDTH), 0.5),
        'rwkv_w2': nrm((N_AB, 2, DECAY_LORA, RWKV_WIDTH), 0.5 * DECAY_LORA ** -0.5),
        'rwkv_a0': nrm((N_AB, RWKV_WIDTH), 0.5),
        'rwkv_a2': nrm((N_AB, AAA_LORA, RWKV_WIDTH), AAA_LORA ** -0.5),
        'rwkv_g2': nrm((N_AB, GATE_LORA, RWKV_WIDTH), GATE_LORA ** -0.5),
        'rwkv_kk': 0.85 + nrm((N_AB, RWKV_WIDTH), 0.05),
        'rwkv_ka': 1.0 + nrm((N_AB, RWKV_WIDTH), 0.05),
        'rwkv_rk': nrm((N_AB, RWKV_HEADS, RWKV_DIM), 0.1),
        'rwkv_ln_w': 1.0 + nrm((N_AB, RWKV_WIDTH), 0.02),
        'rwkv_ln_b': nrm((N_AB, RWKV_WIDTH), 0.02),
        'ab_w_out': nrm((N_AB, HGRN_WIDTH + RWKV_WIDTH, D), (HGRN_WIDTH + RWKV_WIDTH) ** -0.5),
        'att_w_qkv': nrm((N_ATT, D, QKV_COLS), D ** -0.5),
        'att_sink': nrm((N_ATT, ATT_HEADS), 1.0),
        'att_w_o': nrm((N_ATT, ATT_HEADS * ATT_DIM, D), (ATT_HEADS * ATT_DIM) ** -0.5),
        'ffn_norm': 1.0 + nrm((DEPTH, D), 0.02),
        'ffn_w_up': nrm((DEPTH, D, 2 * D_FF), D ** -0.5),
        'ffn_conv_w': nrm((DEPTH, CONV_WIDTH, D_FF), 0.3) + jnp.array([0.0, 1.0, 0.0], jnp.float32)[None, :, None],
        'ffn_conv_b': nrm((DEPTH, D_FF), 0.02),
        'ffn_w_down': nrm((DEPTH, D_FF, D), D_FF ** -0.5),
        'final_norm': 1.0 + nrm((D,), 0.02),
    }


def reference(x_prompt, x_sample, mix_norm, ab_w_in, hgrn_lb, hgrn_onorm, rwkv_mu, rwkv_w0, rwkv_w2,
              rwkv_a0, rwkv_a2, rwkv_g2, rwkv_kk, rwkv_ka, rwkv_rk, rwkv_ln_w, rwkv_ln_b, ab_w_out,
              att_w_qkv, att_sink, att_w_o, ffn_norm, ffn_w_up, ffn_conv_w, ffn_conv_b, ffn_w_down, final_norm):
    p = {
        'mix_norm': mix_norm, 'ab_w_in': ab_w_in, 'hgrn_lb': hgrn_lb, 'hgrn_onorm': hgrn_onorm,
        'rwkv_mu': rwkv_mu, 'rwkv_w0': rwkv_w0, 'rwkv_w2': rwkv_w2, 'rwkv_a0': rwkv_a0, 'rwkv_a2': rwkv_a2,
        'rwkv_g2': rwkv_g2, 'rwkv_kk': rwkv_kk, 'rwkv_ka': rwkv_ka, 'rwkv_rk': rwkv_rk,
        'rwkv_ln_w': rwkv_ln_w, 'rwkv_ln_b': rwkv_ln_b, 'ab_w_out': ab_w_out,
        'att_w_qkv': att_w_qkv, 'att_sink': att_sink, 'att_w_o': att_w_o,
        'ffn_norm': ffn_norm, 'ffn_w_up': ffn_w_up, 'ffn_conv_w': ffn_conv_w, 'ffn_conv_b': ffn_conv_b,
        'ffn_w_down': ffn_w_down, 'final_norm': final_norm,
    }
    y_prompt = trunk(x_prompt, p)
    y_sample = trunk(x_sample, p)
    return (y_prompt, y_sample)
```

```python
import functools

import jax
import jax.numpy as jnp
from jax import lax
from jax.experimental import pallas as pl
from jax.experimental.pallas import tpu as pltpu

F32 = jnp.float32
BF16 = jnp.bfloat16

D_MODEL = 2048
DEPTH = 2
HGRN_DIM = 128
HGRN_HEADS = 8
HGRN_WIDTH = 1024
RWKV_DIM = 64
RWKV_HEADS = 16
RWKV_WIDTH = 1024
DECAY_LORA = 64
AAA_LORA = 64
GATE_LORA = 160
LORA_COLS = 2 * DECAY_LORA + AAA_LORA + GATE_LORA
LORA_PAD = 512
LORA_SLAB = 128
MIX_A_COLS = 5 * HGRN_WIDTH
MIX_COLS = MIX_A_COLS + 3 * RWKV_WIDTH + LORA_COLS
MIX_COLS_PAD = MIX_A_COLS + 3 * RWKV_WIDTH + LORA_PAD
ATT_DIM = 128
ATT_HEADS = 16
ATT_KV_HEADS = 4
ATT_GROUP = 4
QKV_COLS = (ATT_HEADS + 2 * ATT_KV_HEADS) * ATT_DIM
BLOCK = 128
ROPE_THETA = 10000.0
D_FF = 5632
RMS_EPS = 1e-6
GN_EPS = 64e-5
CHUNK = 64
PAIR = 2 * RWKV_DIM
N_PAIRS = RWKV_WIDTH // PAIR

VMEM_LIMIT = 56 * 1024 * 1024
HI = lax.Precision.HIGHEST


def _cparams(sem):
    return pltpu.CompilerParams(dimension_semantics=sem, vmem_limit_bytes=VMEM_LIMIT)


def _mm(a, b):
    return jnp.dot(a.astype(BF16), b.astype(BF16), preferred_element_type=F32)


def _mm_nt(a, b):
    return lax.dot_general(a.astype(BF16), b.astype(BF16), (((1,), (1,)), ((), ())),
                           preferred_element_type=F32)


def _mm_tn(a, b):
    return lax.dot_general(a.astype(BF16), b.astype(BF16), (((0,), (0,)), ((), ())),
                           preferred_element_type=F32)


def _mm_hi(a, b):
    return jnp.dot(a, b, preferred_element_type=F32, precision=HI)


def _sigmoid(x):
    return 1.0 / (1.0 + jnp.exp(-x))


def _norm_matmul_body(x_ref, g_ref, w_ref, o_ref, h_ref):
    @pl.when(pl.program_id(1) == 0)
    def _():
        x = x_ref[...]
        ms = jnp.mean(x * x, axis=-1, keepdims=True)
        h_ref[...] = (x * lax.rsqrt(ms + RMS_EPS) * g_ref[...]).astype(BF16)

    o_ref[...] = jnp.dot(h_ref[...], w_ref[...], preferred_element_type=F32).astype(o_ref.dtype)


def _norm_matmul_rope_body(x_ref, g_ref, w_ref, cos_ref, sin_ref, o_ref, h_ref, *, n_rope, tn):
    j = pl.program_id(1)

    @pl.when(j == 0)
    def _():
        x = x_ref[...]
        ms = jnp.mean(x * x, axis=-1, keepdims=True)
        h_ref[...] = (x * lax.rsqrt(ms + RMS_EPS) * g_ref[...]).astype(BF16)

    acc = jnp.dot(h_ref[...], w_ref[...], preferred_element_type=F32)

    @pl.when(j < n_rope)
    def _():
        cos = cos_ref[...]
        sin = sin_ref[...]
        for hh in range(tn // ATT_DIM):
            a = acc[:, hh * ATT_DIM:(hh + 1) * ATT_DIM]
            rot = pltpu.roll(a, ATT_DIM // 2, axis=1)
            o_ref[:, hh * ATT_DIM:(hh + 1) * ATT_DIM] = (a * cos + rot * sin).astype(o_ref.dtype)

    @pl.when(j >= n_rope)
    def _():
        o_ref[...] = acc.astype(o_ref.dtype)


def _norm_matmul(x, g, w, *, tm, tn, out_dtype, rope=None):
    M, K = x.shape
    N = w.shape[1]
    grid = (M // tm, N // tn)
    in_specs = [pl.BlockSpec((tm, K), lambda i, j: (i, 0)),
                pl.BlockSpec((1, K), lambda i, j: (0, 0)),
                pl.BlockSpec((K, tn), lambda i, j: (0, j))]
    args = [x, g, w]
    if rope is None:
        body = _norm_matmul_body
    else:
        cos, sin, n_rope = rope
        tiles_per_seq = cos.shape[0] // tm
        in_specs += [pl.BlockSpec((tm, ATT_DIM), lambda i, j: (i % tiles_per_seq, 0)),
                     pl.BlockSpec((tm, ATT_DIM), lambda i, j: (i % tiles_per_seq, 0))]
        args += [cos, sin]
        body = functools.partial(_norm_matmul_rope_body, n_rope=n_rope, tn=tn)
    return pl.pallas_call(
        body,
        out_shape=jax.ShapeDtypeStruct((M, N), out_dtype),
        grid=grid,
        in_specs=in_specs,
        out_specs=pl.BlockSpec((tm, tn), lambda i, j: (i, j)),
        scratch_shapes=[pltpu.VMEM((tm, K), BF16)],
        compiler_params=_cparams(("parallel", "arbitrary")),
    )(*args)


def _matmul_res_body(*refs, n_parts):
    a_refs = refs[:n_parts]
    w_ref, r_ref, o_ref = refs[n_parts:]
    acc = r_ref[...]
    k0 = 0
    for a_ref in a_refs:
        kw = a_ref.shape[1]
        acc = acc + jnp.dot(a_ref[...], w_ref[k0:k0 + kw, :], preferred_element_type=F32)
        k0 += kw
    o_ref[...] = acc


def _matmul_res(a_parts, w, res, *, tm, tn):
    M, N = res.shape
    K = w.shape[0]
    in_specs = [pl.BlockSpec((tm, a.shape[1]), lambda i, j: (i, 0)) for a in a_parts]
    in_specs += [pl.BlockSpec((K, tn), lambda i, j: (0, j)),
                 pl.BlockSpec((tm, tn), lambda i, j: (i, j))]
    return pl.pallas_call(
        functools.partial(_matmul_res_body, n_parts=len(a_parts)),
        out_shape=jax.ShapeDtypeStruct((M, N), F32),
        grid=(M // tm, N // tn),
        in_specs=in_specs,
        out_specs=pl.BlockSpec((tm, tn), lambda i, j: (i, j)),
        compiler_params=_cparams(("parallel", "arbitrary")),
    )(*a_parts, w, res)


def _hgrn_body(*refs, rev, n_chunks, final):
    if final:
        q_ref, i_ref, z_ref, lb_ref, ofwd_ref, g_ref, onorm_ref, o_ref, st_ref = refs
    else:
        q_ref, i_ref, z_ref, lb_ref, o_ref, st_ref = refs

    @pl.when(pl.program_id(1) == 0)
    def _():
        st_ref[...] = jnp.zeros_like(st_ref)

    lb = lb_ref[...]
    row = lax.broadcasted_iota(jnp.int32, (CHUNK, CHUNK), 0)
    col = lax.broadcasted_iota(jnp.int32, (CHUNK, CHUNK), 1)
    keep = (row <= col) if rev else (row >= col)
    tri = jnp.where(keep, 1.0, 0.0).astype(F32)
    last = 0 if rev else CHUNK - 1

    order = range(n_chunks - 1, -1, -1) if rev else range(n_chunks)
    for ci in order:
        sl = slice(ci * CHUNK, (ci + 1) * CHUNK)
        f = lb + (1.0 - lb) * _sigmoid(z_ref[sl, :])
        kk = 1.0 - f
        b = _mm_hi(tri, jnp.log(f))
        b_last = b[last:last + 1, :]
        q_d = q_ref[sl, :] * jnp.exp(b)
        k_d = kk * jnp.exp(-b)
        k_u = kk * jnp.exp(b_last - b)
        dec = jnp.exp(b_last)
        v = i_ref[sl, :]
        outs = []
        for h in range(HGRN_HEADS):
            hs = slice(h * HGRN_DIM, (h + 1) * HGRN_DIM)
            att = jnp.where(keep, _mm_nt(q_d[:, hs], k_d[:, hs]), 0.0)
            st = st_ref[h]
            outs.append(_mm(att, v[:, hs]) + _mm_nt(q_d[:, hs], st))
            st_ref[h] = st * dec[:, hs] + _mm_tn(v[:, hs], k_u[:, hs])
        if final:
            g = g_ref[sl, :]
            silu_g = g * _sigmoid(g)
            for h in range(HGRN_HEADS):
                hs = slice(h * HGRN_DIM, (h + 1) * HGRN_DIM)
                oa = outs[h] + ofwd_ref[sl, hs]
                oa = oa * lax.rsqrt(jnp.mean(oa * oa, axis=-1, keepdims=True) + RMS_EPS)
                o_ref[sl, hs] = (oa * onorm_ref[:, hs] * silu_g[:, hs]).astype(o_ref.dtype)
        else:
            for h in range(HGRN_HEADS):
                hs = slice(h * HGRN_DIM, (h + 1) * HGRN_DIM)
                o_ref[sl, hs] = outs[h]


def _hgrn_dir(proj, lb, *, rev, tb, ofwd=None, onorm=None):
    B, T, _ = proj.shape
    nt = T // tb
    final = ofwd is not None

    def tmap(n):
        return (nt - 1 - n) if rev else n

    def col(c):
        return pl.BlockSpec((None, tb, HGRN_WIDTH), lambda b, n: (b, tmap(n), c))

    vec = pl.BlockSpec((1, HGRN_WIDTH), lambda b, n: (0, 0))
    in_specs = [col(0), col(1), col(3 if rev else 2), vec]
    args = [proj, proj, proj, lb]
    if final:
        in_specs += [col(0), col(4), vec]
        args += [ofwd, proj, onorm]
    return pl.pallas_call(
        functools.partial(_hgrn_body, rev=rev, n_chunks=tb // CHUNK, final=final),
        out_shape=jax.ShapeDtypeStruct((B, T, HGRN_WIDTH), BF16 if final else F32),
        grid=(B, nt),
        in_specs=in_specs,
        out_specs=pl.BlockSpec((None, tb, HGRN_WIDTH), lambda b, n: (b, tmap(n), 0)),
        scratch_shapes=[pltpu.VMEM((HGRN_HEADS, HGRN_DIM, HGRN_DIM), F32)],
        compiler_params=_cparams(("parallel", "arbitrary")),
    )(*args)


def _pair_ones():
    r = lax.broadcasted_iota(jnp.int32, (PAIR, PAIR), 0) // RWKV_DIM
    c = lax.broadcasted_iota(jnp.int32, (PAIR, PAIR), 1) // RWKV_DIM
    return jnp.where(r == c, 1.0, 0.0).astype(BF16)


def _head_sum(x, ones_bd):
    outs = []
    for p in range(N_PAIRS):
        xs = x[:, p * PAIR:(p + 1) * PAIR]
        hi = xs.astype(BF16)
        lo = (xs - hi.astype(F32)).astype(BF16)
        lo2 = (xs - hi.astype(F32) - lo.astype(F32)).astype(BF16)
        outs.append(jnp.dot(hi, ones_bd, preferred_element_type=F32)
                    + jnp.dot(lo, ones_bd, preferred_element_type=F32)
                    + jnp.dot(lo2, ones_bd, preferred_element_type=F32))
    return jnp.concatenate(outs, axis=1)


def _rwkv_prep_body(r_ref, k_ref, v_ref, l_ref,
                    rp_ref, kp_ref, vp_ref, lp_ref, rn_ref, kn_ref, vn_ref, ln_ref,
                    mu_r_ref, mu_k_ref, mu_v_ref, mu_l_ref,
                    w0_ref, w2f_ref, w2b_ref, a0_ref, a2_ref, g2_ref, kkw_ref, kaw_ref, rk_ref,
                    ro_ref, ko_ref, vo_ref, kko_ref, bvo_ref, lwf_ref, lwb_ref, go_ref, bo_ref,
                    *, tb):
    n = pl.program_id(1)
    nt = pl.num_programs(1)
    has_prev = jnp.where(n > 0, 1.0, 0.0).astype(F32)
    has_next = jnp.where(n < nt - 1, 1.0, 0.0).astype(F32)
    rows = lax.broadcasted_iota(jnp.int32, (tb, 1), 0)
    first = rows == 0
    lastr = rows == tb - 1

    def shift(x_ref, p_ref, n_ref, mu_ref):
        x = x_ref[...]
        prev = jnp.where(first, p_ref[7:8, :] * has_prev, pltpu.roll(x, 1, axis=0))
        nxt = jnp.where(lastr, n_ref[0:1, :] * has_next, pltpu.roll(x, tb - 1, axis=0))
        return x + mu_ref[...] * (0.5 * (prev + nxt) - x)

    r = shift(r_ref, rp_ref, rn_ref, mu_r_ref)
    k = shift(k_ref, kp_ref, kn_ref, mu_k_ref)
    v = shift(v_ref, vp_ref, vn_ref, mu_v_ref)
    lo = shift(l_ref, lp_ref, ln_ref, mu_l_ref)
    wd = jnp.tanh(lo[:, 0:LORA_SLAB])
    ag = lo[:, LORA_SLAB:3 * LORA_SLAB]

    def log_decay(w0, w2):
        u = w0 + _mm_hi(wd, w2)
        softplus = jnp.maximum(-u, 0.0) + jnp.log(1.0 + jnp.exp(-jnp.abs(u)))
        return -jnp.exp(-softplus - 0.5)

    lwf_ref[...] = log_decay(w0_ref[0:1, :], w2f_ref[...])
    lwb_ref[...] = log_decay(w0_ref[1:2, :], w2b_ref[...])
    a = _sigmoid(a0_ref[...] + _mm_hi(ag[:, 0:LORA_SLAB], a2_ref[...]))
    go_ref[...] = _mm_hi(_sigmoid(ag), g2_ref[...])

    ones_bd = _pair_ones()
    kk = k * kkw_ref[...]
    norm = jnp.maximum(jnp.sqrt(_head_sum(kk * kk, ones_bd)), 1e-12)
    kk = kk / norm
    k2 = k * (1.0 + (a - 1.0) * kaw_ref[...])
    ro_ref[...] = r
    ko_ref[...] = k2
    vo_ref[...] = v
    kko_ref[...] = kk
    bvo_ref[...] = kk * a
    bo_ref[...] = _head_sum(r * k2 * rk_ref[...], ones_bd) * v


def _rwkv_prep(proj, mu, w0, w2, a0, a2, g2, kkw, kaw, rk, *, tb):
    B, T, _ = proj.shape
    nt = T // tb
    hb = tb // 8
    last8 = T // 8 - 1

    def main(width, c):
        return pl.BlockSpec((None, tb, width), lambda b, n: (b, n, c))

    def prev(width, c):
        return pl.BlockSpec((None, 8, width), lambda b, n: (b, jnp.maximum(n * hb - 1, 0), c))

    def nxt(width, c):
        return pl.BlockSpec((None, 8, width), lambda b, n: (b, jnp.minimum((n + 1) * hb, last8), c))

    def full(a):
        return pl.BlockSpec(a.shape, lambda b, n: (0,) * a.ndim)

    W = RWKV_WIDTH
    c_l = (MIX_A_COLS + 3 * W) // LORA_PAD
    zpad = lambda a, before, total: jnp.pad(a, ((before, total - before - a.shape[0]), (0, 0)))
    params = [mu[:, 0:W], mu[:, W:2 * W], mu[:, 2 * W:3 * W], mu[:, 3 * W:],
              w0, zpad(w2[0], 0, LORA_SLAB), zpad(w2[1], DECAY_LORA, LORA_SLAB), a0,
              zpad(a2, 0, LORA_SLAB), zpad(g2, AAA_LORA, 2 * LORA_SLAB), kkw, kaw, rk]
    in_specs = ([main(W, 5), main(W, 6), main(W, 7), main(LORA_PAD, c_l),
                 prev(W, 5), prev(W, 6), prev(W, 7), prev(LORA_PAD, c_l),
                 nxt(W, 5), nxt(W, 6), nxt(W, 7), nxt(LORA_PAD, c_l)]
                + [full(p) for p in params])
    out = jax.ShapeDtypeStruct((B, T, W), F32)
    return pl.pallas_call(
        functools.partial(_rwkv_prep_body, tb=tb),
        out_shape=[out] * 9,
        grid=(B, nt),
        in_specs=in_specs,
        out_specs=[pl.BlockSpec((None, tb, W), lambda b, n: (b, n, 0))] * 9,
        compiler_params=_cparams(("parallel", "parallel")),
    )(*([proj] * 12), *params)


def _neumann_inverse(a):
    n = a.shape[0]
    eye = jnp.where(lax.broadcasted_iota(jnp.int32, (n, n), 0) == lax.broadcasted_iota(jnp.int32, (n, n), 1),
                    1.0, 0.0).astype(F32)
    p = eye + a
    ak = _mm_hi(a, a)
    for _ in range(4):
        s = _mm_hi(jnp.concatenate([p, ak], axis=0), ak)
        p = p + s[:n]
        ak = s[n:]
    return p + _mm_hi(p, ak)


def _rwkv_scan_body(*refs, rev, final):
    if final:
        (r_ref, k_ref, v_ref, kk_ref, bv_ref, lw_ref, yf_ref, g_ref, bonus_ref, lnw_ref, lnb_ref,
         o_ref, h_ref) = refs
    else:
        r_ref, k_ref, v_ref, kk_ref, bv_ref, lw_ref, o_ref, h_ref = refs

    @pl.when(pl.program_id(1) == 0)
    def _():
        h_ref[...] = jnp.zeros_like(h_ref)

    C = CHUNK
    row = lax.broadcasted_iota(jnp.int32, (C, C), 0)
    col = lax.broadcasted_iota(jnp.int32, (C, C), 1)
    tri = jnp.where((row <= col) if rev else (row >= col), 1.0, 0.0).astype(F32)
    last = 0 if rev else C - 1

    lw = lw_ref[...]
    c_inc = _mm_hi(tri, lw)
    c_exc = c_inc - lw
    c_tot = c_inc[last:last + 1, :]
    e_ninc = jnp.exp(-c_inc)
    e_hat = jnp.exp(c_tot - c_inc)
    w_tot = jnp.exp(c_tot)
    kk = kk_ref[...]
    bv = bv_ref[...]
    k2 = k_ref[...]
    v = v_ref[...]
    at = -kk * jnp.exp(c_exc)
    rt = r_ref[...] * jnp.exp(c_inc)
    bt = bv * e_ninc
    kt = k2 * e_ninc
    bh = bv * e_hat
    kh = k2 * e_hat

    lane = lax.broadcasted_iota(jnp.int32, (C, PAIR), 1)
    m0 = lane < RWKV_DIM
    r2 = lax.broadcasted_iota(jnp.int32, (PAIR, 2 * PAIR), 0)
    c2 = lax.broadcasted_iota(jnp.int32, (PAIR, 2 * PAIR), 1) % PAIR
    strict = (r2 < c2) if rev else (r2 > c2)
    incl = (r2 <= c2) if rev else (r2 >= c2)
    eye = (lax.broadcasted_iota(jnp.int32, (PAIR, PAIR), 0)
           == lax.broadcasted_iota(jnp.int32, (PAIR, PAIR), 1))
    zeros = jnp.zeros((PAIR, PAIR), F32)

    ys = []
    for p in range(N_PAIRS):
        ps = slice(p * PAIR, (p + 1) * PAIR)

        def stack2(x):
            xs = x[:, ps]
            return jnp.concatenate([jnp.where(m0, xs, 0.0), jnp.where(m0, 0.0, xs)], axis=0)

        a2, rr2, b2, kt2, v2 = stack2(at), stack2(rt), stack2(bt), stack2(kt), stack2(v)
        bh2, kh2 = stack2(bh), stack2(kh)
        pm = _mm_nt(jnp.concatenate([a2, rr2], axis=0), jnp.concatenate([b2, kt2], axis=0))
        a_abk = jnp.where(strict, pm[:PAIR, :], 0.0)
        a_rbk = jnp.where(incl, pm[PAIR:, :], 0.0)
        tinv = _neumann_inverse(a_abk[:, :PAIR])
        x1 = _mm(a_abk[:, PAIR:], v2)
        z = _mm(tinv, jnp.concatenate([a2, x1], axis=1))
        w2 = jnp.concatenate([z, jnp.concatenate([zeros, v2], axis=1)], axis=0)
        mg = _mm_tn(jnp.concatenate([bh2, kh2], axis=0), w2)
        ry = _mm(a_rbk, w2)
        m = mg[:, :PAIR] + jnp.where(eye, w_tot[:, ps], 0.0)
        rp = rr2 + ry[:, :PAIR]
        h = h_ref[p]
        yh = _mm(jnp.concatenate([rp, m], axis=0), h)
        y2 = yh[:PAIR] + ry[:, PAIR:]
        h_ref[p] = yh[PAIR:] + mg[:, PAIR:]
        ys.append(y2[:C] + y2[C:])
    y = jnp.concatenate(ys, axis=1)

    if final:
        ones_bd = _pair_ones()
        y = y + yf_ref[...]
        mean = _head_sum(y, ones_bd) * (1.0 / RWKV_DIM)
        yc = y - mean
        var = _head_sum(yc * yc, ones_bd) * (1.0 / RWKV_DIM)
        yn = yc * lax.rsqrt(var + GN_EPS) * lnw_ref[...] + lnb_ref[...]
        o_ref[...] = ((yn + bonus_ref[...]) * g_ref[...]).astype(o_ref.dtype)
    else:
        o_ref[...] = y


def _rwkv_dir(r, k2, v, kk, bv, lw, *, rev, fin=None):
    B, T, W = r.shape
    nc = T // CHUNK

    def tmap(n):
        return (nc - 1 - n) if rev else n

    blk = pl.BlockSpec((None, CHUNK, W), lambda b, n: (b, tmap(n), 0))
    vec = pl.BlockSpec((1, W), lambda b, n: (0, 0))
    in_specs = [blk] * 6
    args = [r, k2, v, kk, bv, lw]
    if fin is not None:
        in_specs += [blk, blk, blk, vec, vec]
        args += list(fin)
    return pl.pallas_call(
        functools.partial(_rwkv_scan_body, rev=rev, final=fin is not None),
        out_shape=jax.ShapeDtypeStruct((B, T, W), BF16 if fin is not None else F32),
        grid=(B, nc),
        in_specs=in_specs,
        out_specs=blk,
        scratch_shapes=[pltpu.VMEM((N_PAIRS, PAIR, PAIR), F32)],
        compiler_params=_cparams(("parallel", "arbitrary")),
    )(*args)


def _attn_body(sink_ref, q_ref, kp_ref, kc_ref, kn_ref, vp_ref, vc_ref, vn_ref, o_ref):
    n = pl.program_id(1)
    nb = pl.num_programs(1)
    r = lax.broadcasted_iota(jnp.int32, (BLOCK, 3 * BLOCK), 0)
    c = lax.broadcasted_iota(jnp.int32, (BLOCK, 3 * BLOCK), 1)
    d = c - BLOCK - r
    lo = jnp.where(n > 0, 0, BLOCK)
    hi = jnp.where(n < nb - 1, 3 * BLOCK, 2 * BLOCK)
    valid = (d >= -BLOCK) & (d <= BLOCK) & (c >= lo) & (c < hi)
    scale = ATT_DIM ** -0.5
    for kh in range(ATT_KV_HEADS):
        ks = slice(kh * ATT_DIM, (kh + 1) * ATT_DIM)
        kw = jnp.concatenate([kp_ref[:, ks], kc_ref[:, ks], kn_ref[:, ks]], axis=0)
        vw = jnp.concatenate([vp_ref[:, ks], vc_ref[:, ks], vn_ref[:, ks]], axis=0)
        heads = [kh * ATT_GROUP + g for g in range(ATT_GROUP)]
        qg = jnp.concatenate([q_ref[:, h * ATT_DIM:(h + 1) * ATT_DIM] for h in heads], axis=0)
        s_all = lax.dot_general(qg, kw, (((1,), (1,)), ((), ())), preferred_element_type=F32) * scale
        ps = []
        for g, h in enumerate(heads):
            s = jnp.where(valid, s_all[g * BLOCK:(g + 1) * BLOCK, :], -jnp.inf)
            sk = sink_ref[h]
            m = jnp.maximum(jnp.max(s, axis=-1, keepdims=True), sk)
            e = jnp.exp(s - m)
            den = jnp.sum(e, axis=-1, keepdims=True) + jnp.exp(sk - m)
            ps.append((e / den).astype(BF16))
        o_all = jnp.dot(jnp.concatenate(ps, axis=0), vw, preferred_element_type=F32)
        for g, h in enumerate(heads):
            o_ref[:, h * ATT_DIM:(h + 1) * ATT_DIM] = o_all[g * BLOCK:(g + 1) * BLOCK, :].astype(o_ref.dtype)


def _attention(qkv, sink):
    B, T, _ = qkv.shape
    nb = T // BLOCK
    kvw = ATT_KV_HEADS * ATT_DIM
    kc = (ATT_HEADS * ATT_DIM) // kvw
    vc = kc + 1

    def blk(cidx, off):
        def imap(b, n):
            return (b, jnp.clip(n + off, 0, nb - 1), cidx)
        return pl.BlockSpec((None, BLOCK, kvw), imap)

    return pl.pallas_call(
        _attn_body,
        out_shape=jax.ShapeDtypeStruct((B, T, ATT_HEADS * ATT_DIM), BF16),
        grid=(B, nb),
        in_specs=[pl.BlockSpec(memory_space=pltpu.SMEM),
                  pl.BlockSpec((None, BLOCK, ATT_HEADS * ATT_DIM), lambda b, n: (b, n, 0)),
                  blk(kc, -1), blk(kc, 0), blk(kc, 1), blk(vc, -1), blk(vc, 0), blk(vc, 1)],
        out_specs=pl.BlockSpec((None, BLOCK, ATT_HEADS * ATT_DIM), lambda b, n: (b, n, 0)),
        compiler_params=_cparams(("parallel", "parallel")),
    )(sink, qkv, qkv, qkv, qkv, qkv, qkv, qkv)


def _ffn_body(x_ref, xp_ref, xn_ref, g_ref, wg_ref, wv_ref, cw_ref, cb_ref, wd_ref, fg_ref, o_ref, h_ref,
              *, tm, tiles_per_seq, final_norm):
    i = pl.program_id(0)
    f = pl.program_id(1)
    nf = pl.num_programs(1)

    @pl.when(f == 0)
    def _():
        def norm(x):
            ms = jnp.mean(x * x, axis=-1, keepdims=True)
            return (x * lax.rsqrt(ms + RMS_EPS) * g_ref[...]).astype(BF16)
        x = x_ref[...]
        h_ref[0:tm, :] = norm(x)
        h_ref[tm:tm + 16, :] = norm(jnp.concatenate([xp_ref[...], xn_ref[...]], axis=0))
        o_ref[...] = x

    t = i % tiles_per_seq
    has_prev = jnp.where(t > 0, 1.0, 0.0).astype(F32)
    has_next = jnp.where(t < tiles_per_seq - 1, 1.0, 0.0).astype(F32)
    ge = jnp.dot(h_ref[...], wg_ref[...], preferred_element_type=F32)
    gm = ge[0:tm, :]
    rows = lax.broadcasted_iota(jnp.int32, (tm, 1), 0)
    g_prev = jnp.where(rows == 0, ge[tm + 7:tm + 8, :] * has_prev, pltpu.roll(gm, 1, axis=0))
    g_next = jnp.where(rows == tm - 1, ge[tm + 8:tm + 9, :] * has_next, pltpu.roll(gm, tm - 1, axis=0))
    gate = g_prev * cw_ref[0:1, :] + gm * cw_ref[1:2, :] + g_next * cw_ref[2:3, :] + cb_ref[...]
    val = jnp.dot(h_ref[0:tm, :], wv_ref[...], preferred_element_type=F32)
    act = (gate * _sigmoid(gate) * val).astype(BF16)
    o_ref[...] += jnp.dot(act, wd_ref[...], preferred_element_type=F32)

    if final_norm:
        @pl.when(f == nf - 1)
        def _():
            y = o_ref[...]
            ms = jnp.mean(y * y, axis=-1, keepdims=True)
            o_ref[...] = y * lax.rsqrt(ms + RMS_EPS) * fg_ref[...]


def _ffn(x, g, w_up, conv_w, conv_b, w_down, final_g, *, seq_len, tm, tf, final_norm):
    M, D = x.shape
    nf = D_FF // tf
    hb = tm // 8
    last8 = M // 8 - 1
    in_specs = [
        pl.BlockSpec((tm, D), lambda i, f: (i, 0)),
        pl.BlockSpec((8, D), lambda i, f: (jnp.maximum(i * hb - 1, 0), 0)),
        pl.BlockSpec((8, D), lambda i, f: (jnp.minimum((i + 1) * hb, last8), 0)),
        pl.BlockSpec((1, D), lambda i, f: (0, 0)),
        pl.BlockSpec((D, tf), lambda i, f: (0, f)),
        pl.BlockSpec((D, tf), lambda i, f: (0, nf + f)),
        pl.BlockSpec((3, tf), lambda i, f: (0, f)),
        pl.BlockSpec((1, tf), lambda i, f: (0, f)),
        pl.BlockSpec((tf, D), lambda i, f: (f, 0)),
        pl.BlockSpec((1, D), lambda i, f: (0, 0)),
    ]
    return pl.pallas_call(
        functools.partial(_ffn_body, tm=tm, tiles_per_seq=seq_len // tm, final_norm=final_norm),
        out_shape=jax.ShapeDtypeStruct((M, D), F32),
        grid=(M // tm, nf),
        in_specs=in_specs,
        out_specs=pl.BlockSpec((tm, D), lambda i, f: (i, 0)),
        scratch_shapes=[pltpu.VMEM((tm + 16, D), BF16)],
        compiler_params=_cparams(("parallel", "arbitrary")),
    )(x, x, x, g, w_up, w_up, conv_w, conv_b, w_down, final_g)


def _pick(n, prefs):
    for p in prefs:
        if n % p == 0:
            return p
    raise ValueError(f"no tile for {n}")


def _rope_tables(T):
    half = ATT_DIM // 2
    inv = ROPE_THETA ** (-jnp.arange(half, dtype=F32) / half)
    ang = jnp.arange(T, dtype=F32)[:, None] * inv[None, :]
    cos = jnp.cos(ang)
    sin = jnp.sin(ang)
    return jnp.concatenate([cos, cos], axis=1), jnp.concatenate([-sin, sin], axis=1)


def _prepare_params(p):
    q = dict(p)
    w_in = p['ab_w_in'][0]
    q['w_in'] = jnp.pad(w_in, ((0, 0), (0, MIX_COLS_PAD - MIX_COLS))).astype(BF16)
    mu = p['rwkv_mu'][0]
    q['mu'] = jnp.pad(mu, (0, LORA_PAD - LORA_COLS))[None, :]
    q['lb'] = jnp.cumsum(jax.nn.softmax(p['hgrn_lb'].astype(F32), axis=0), axis=0)
    q['w_out'] = p['ab_w_out'][0].astype(BF16)
    q['w_qkv'] = p['att_w_qkv'][0].astype(BF16)
    q['w_o'] = p['att_w_o'][0].astype(BF16)
    q['w_up'] = p['ffn_w_up'].astype(BF16)
    q['w_down'] = p['ffn_w_down'].astype(BF16)
    return q


def _mixer_layer(x2, q, B, T, layer):
    M = B * T
    tm = _pick(M, (512, 256, 128))
    proj = _norm_matmul(x2, q['mix_norm'][layer][None, :], q['w_in'], tm=tm, tn=512, out_dtype=F32)
    proj = proj.reshape(B, T, MIX_COLS_PAD)
    lb = q['lb'][layer][None, :]
    tb = _pick(T, (256, 128, 64))
    o_fwd = _hgrn_dir(proj, lb, rev=False, tb=tb)
    ya = _hgrn_dir(proj, lb, rev=True, tb=tb, ofwd=o_fwd, onorm=q['hgrn_onorm'][0][None, :])

    r, k2, v, kk, bv, lwf, lwb, g, bonus = _rwkv_prep(
        proj, q['mu'], q['rwkv_w0'][0], q['rwkv_w2'][0], q['rwkv_a0'][0][None, :], q['rwkv_a2'][0],
        q['rwkv_g2'][0], q['rwkv_kk'][0][None, :], q['rwkv_ka'][0][None, :],
        q['rwkv_rk'][0].reshape(1, RWKV_WIDTH), tb=tb)
    y_fwd = _rwkv_dir(r, k2, v, kk, bv, lwf, rev=False)
    yb = _rwkv_dir(r, k2, v, kk, bv, lwb, rev=True,
                   fin=(y_fwd, g, bonus, q['rwkv_ln_w'][0][None, :], q['rwkv_ln_b'][0][None, :]))
    return _matmul_res([ya.reshape(M, HGRN_WIDTH), yb.reshape(M, RWKV_WIDTH)], q['w_out'], x2, tm=tm, tn=512)


def _attention_layer(x2, q, B, T, layer, rope_tabs):
    M = B * T
    tm = _pick(M, (512, 256, 128))
    cos, sin = rope_tabs
    n_rope = ((ATT_HEADS + ATT_KV_HEADS) * ATT_DIM) // 512
    qkv = _norm_matmul(x2, q['mix_norm'][layer][None, :], q['w_qkv'], tm=tm, tn=512, out_dtype=BF16,
                       rope=(cos, sin, n_rope))
    o = _attention(qkv.reshape(B, T, QKV_COLS), q['att_sink'][0])
    return _matmul_res([o.reshape(M, ATT_HEADS * ATT_DIM)], q['w_o'], x2, tm=tm, tn=512)


def _trunk(x, q):
    B, T, D = x.shape
    M = B * T
    x2 = x.reshape(M, D)
    rope_tabs = _rope_tables(T)
    tm = _pick(T, (512, 256, 128))
    for layer in range(DEPTH):
        if layer % 2 == 0:
            x2 = _mixer_layer(x2, q, B, T, layer)
        else:
            x2 = _attention_layer(x2, q, B, T, layer, rope_tabs)
        x2 = _ffn(x2, q['ffn_norm'][layer][None, :], q['w_up'][layer], q['ffn_conv_w'][layer],
                  q['ffn_conv_b'][layer][None, :], q['w_down'][layer], q['final_norm'][None, :],
                  seq_len=T, tm=tm, tf=512, final_norm=(layer == DEPTH - 1))
    return x2.reshape(B, T, D)


def kernel(x_prompt, x_sample, mix_norm, ab_w_in, hgrn_lb, hgrn_onorm, rwkv_mu, rwkv_w0, rwkv_w2, rwkv_a0, rwkv_a2, rwkv_g2, rwkv_kk, rwkv_ka, rwkv_rk, rwkv_ln_w, rwkv_ln_b, ab_w_out, att_w_qkv, att_sink, att_w_o, ffn_norm, ffn_w_up, ffn_conv_w, ffn_conv_b, ffn_w_down, final_norm):
    p = {
        'mix_norm': mix_norm, 'ab_w_in': ab_w_in, 'hgrn_lb': hgrn_lb, 'hgrn_onorm': hgrn_onorm,
        'rwkv_mu': rwkv_mu, 'rwkv_w0': rwkv_w0, 'rwkv_w2': rwkv_w2, 'rwkv_a0': rwkv_a0, 'rwkv_a2': rwkv_a2,
        'rwkv_g2': rwkv_g2, 'rwkv_kk': rwkv_kk, 'rwkv_ka': rwkv_ka, 'rwkv_rk': rwkv_rk,
        'rwkv_ln_w': rwkv_ln_w, 'rwkv_ln_b': rwkv_ln_b, 'ab_w_out': ab_w_out,
        'att_w_qkv': att_w_qkv, 'att_sink': att_sink, 'att_w_o': att_w_o,
        'ffn_norm': ffn_norm, 'ffn_w_up': ffn_w_up, 'ffn_conv_w': ffn_conv_w, 'ffn_conv_b': ffn_conv_b,
        'ffn_w_down': ffn_w_down, 'final_norm': final_norm,
    }
    q = _prepare_params(p)
    return (_trunk(x_prompt, q), _trunk(x_sample, q))
```

```python
import functools

import jax
import jax.numpy as jnp
from jax import lax
from jax.experimental import pallas as pl
from jax.experimental.pallas import tpu as pltpu

F32 = jnp.float32
BF16 = jnp.bfloat16

D_MODEL = 2048
DEPTH = 2
HGRN_DIM = 128
HGRN_HEADS = 8
HGRN_WIDTH = 1024
RWKV_DIM = 64
RWKV_HEADS = 16
RWKV_WIDTH = 1024
DECAY_LORA = 64
AAA_LORA = 64
GATE_LORA = 160
LORA_COLS = 2 * DECAY_LORA + AAA_LORA + GATE_LORA
LORA_PAD = 512
LORA_SLAB = 128
MIX_A_COLS = 5 * HGRN_WIDTH
MIX_COLS = MIX_A_COLS + 3 * RWKV_WIDTH + LORA_COLS
MIX_COLS_PAD = MIX_A_COLS + 3 * RWKV_WIDTH + LORA_PAD
ATT_DIM = 128
ATT_HEADS = 16
ATT_KV_HEADS = 4
ATT_GROUP = 4
QKV_COLS = (ATT_HEADS + 2 * ATT_KV_HEADS) * ATT_DIM
BLOCK = 128
ROPE_THETA = 10000.0
D_FF = 5632
RMS_EPS = 1e-6
GN_EPS = 64e-5
CHUNK = 64
PAIR = 2 * RWKV_DIM
N_PAIRS = RWKV_WIDTH // PAIR

VMEM_LIMIT = 56 * 1024 * 1024
HI = lax.Precision.HIGHEST


def _cparams(sem):
    return pltpu.CompilerParams(dimension_semantics=sem, vmem_limit_bytes=VMEM_LIMIT)


def _mm(a, b):
    return jnp.dot(a.astype(BF16), b.astype(BF16), preferred_element_type=F32)


def _mm_nt(a, b):
    return lax.dot_general(a.astype(BF16), b.astype(BF16), (((1,), (1,)), ((), ())),
                           preferred_element_type=F32)


def _mm_tn(a, b):
    return lax.dot_general(a.astype(BF16), b.astype(BF16), (((0,), (0,)), ((), ())),
                           preferred_element_type=F32)


def _mm_hi(a, b):
    return jnp.dot(a, b, preferred_element_type=F32, precision=HI)


def _split3(x):
    hi = x.astype(BF16)
    r1 = x - hi.astype(F32)
    mid = r1.astype(BF16)
    lo = (r1 - mid.astype(F32)).astype(BF16)
    return hi, mid, lo


def _mm_cumsum(tri, x):
    t = tri.astype(BF16)
    hi, mid, lo = _split3(x)
    return (jnp.dot(t, hi, preferred_element_type=F32) + jnp.dot(t, mid, preferred_element_type=F32)
            + jnp.dot(t, lo, preferred_element_type=F32))


_mm_lora = _mm
_mm_neumann = _mm


def _sigmoid(x):
    return 1.0 / (1.0 + jnp.exp(-x))


def _norm_matmul_body(x_ref, g_ref, w_ref, o_ref, h_ref):
    @pl.when(pl.program_id(1) == 0)
    def _():
        x = x_ref[...]
        ms = jnp.mean(x * x, axis=-1, keepdims=True)
        h_ref[...] = (x * lax.rsqrt(ms + RMS_EPS) * g_ref[...]).astype(BF16)

    o_ref[...] = jnp.dot(h_ref[...], w_ref[...], preferred_element_type=F32).astype(o_ref.dtype)


def _norm_matmul_rope_body(x_ref, g_ref, w_ref, cos_ref, sin_ref, o_ref, h_ref, *, n_rope, tn):
    x = x_ref[...]
    ms = jnp.mean(x * x, axis=-1, keepdims=True)
    h_ref[...] = (x * lax.rsqrt(ms + RMS_EPS) * g_ref[...]).astype(BF16)
    cos = cos_ref[...]
    sin = sin_ref[...]
    group = 4 * ATT_DIM
    for gi in range(tn // group):
        acc = jnp.dot(h_ref[...], w_ref[:, gi * group:(gi + 1) * group], preferred_element_type=F32)
        for hh in range(group // ATT_DIM):
            a = acc[:, hh * ATT_DIM:(hh + 1) * ATT_DIM]
            head = gi * (group // ATT_DIM) + hh
            if head < n_rope:
                a = a * cos + pltpu.roll(a, ATT_DIM // 2, axis=1) * sin
            o_ref[:, head * ATT_DIM:(head + 1) * ATT_DIM] = a.astype(o_ref.dtype)


def _norm_matmul(x, g, w, *, tm, tn, out_dtype, rope=None):
    M, K = x.shape
    N = w.shape[1]
    grid = (M // tm, N // tn)
    in_specs = [pl.BlockSpec((tm, K), lambda i, j: (i, 0)),
                pl.BlockSpec((1, K), lambda i, j: (0, 0)),
                pl.BlockSpec((K, tn), lambda i, j: (0, j))]
    args = [x, g, w]
    if rope is None:
        body = _norm_matmul_body
    else:
        cos, sin, n_rope = rope
        tiles_per_seq = cos.shape[0] // tm
        in_specs += [pl.BlockSpec((tm, ATT_DIM), lambda i, j: (i % tiles_per_seq, 0)),
                     pl.BlockSpec((tm, ATT_DIM), lambda i, j: (i % tiles_per_seq, 0))]
        args += [cos, sin]
        body = functools.partial(_norm_matmul_rope_body, n_rope=n_rope, tn=tn)
    return pl.pallas_call(
        body,
        out_shape=jax.ShapeDtypeStruct((M, N), out_dtype),
        grid=grid,
        in_specs=in_specs,
        out_specs=pl.BlockSpec((tm, tn), lambda i, j: (i, j)),
        scratch_shapes=[pltpu.VMEM((tm, K), BF16)],
        compiler_params=_cparams(("parallel", "arbitrary")),
        name="norm_matmul" if rope is None else "norm_matmul_rope",
    )(*args)


def _matmul_res_body(*refs, n_parts):
    a_refs = refs[:n_parts]
    w_ref, r_ref, o_ref = refs[n_parts:]
    acc = r_ref[...]
    k0 = 0
    for a_ref in a_refs:
        kw = a_ref.shape[1]
        acc = acc + jnp.dot(a_ref[...], w_ref[k0:k0 + kw, :], preferred_element_type=F32)
        k0 += kw
    o_ref[...] = acc


def _matmul_res(a_parts, w, res, *, tm, tn):
    M, N = res.shape
    K = w.shape[0]
    in_specs = [pl.BlockSpec((tm, a.shape[1]), lambda i, j: (i, 0)) for a in a_parts]
    in_specs += [pl.BlockSpec((K, tn), lambda i, j: (0, j)),
                 pl.BlockSpec((tm, tn), lambda i, j: (i, j))]
    return pl.pallas_call(
        functools.partial(_matmul_res_body, n_parts=len(a_parts)),
        out_shape=jax.ShapeDtypeStruct((M, N), F32),
        grid=(M // tm, N // tn),
        in_specs=in_specs,
        out_specs=pl.BlockSpec((tm, tn), lambda i, j: (i, j)),
        compiler_params=_cparams(("parallel", "arbitrary")),
        name="matmul_res",
    )(*a_parts, w, res)


def _hgrn_body(*refs, rev, n_chunks, final):
    if final:
        q_ref, i_ref, z_ref, lb_ref, ofwd_ref, g_ref, onorm_ref, o_ref, st_ref = refs
    else:
        q_ref, i_ref, z_ref, lb_ref, o_ref, st_ref = refs

    @pl.when(pl.program_id(1) == 0)
    def _():
        st_ref[...] = jnp.zeros_like(st_ref)

    lb = lb_ref[...]
    row = lax.broadcasted_iota(jnp.int32, (CHUNK, CHUNK), 0)
    col = lax.broadcasted_iota(jnp.int32, (CHUNK, CHUNK), 1)
    keep = (row <= col) if rev else (row >= col)
    tri = jnp.where(keep, 1.0, 0.0).astype(F32)
    last = 0 if rev else CHUNK - 1

    order = range(n_chunks - 1, -1, -1) if rev else range(n_chunks)
    for ci in order:
        sl = slice(ci * CHUNK, (ci + 1) * CHUNK)
        f = lb + (1.0 - lb) * _sigmoid(z_ref[sl, :])
        kk = 1.0 - f
        b = _mm_cumsum(tri, jnp.log(f))
        b_last = b[last:last + 1, :]
        q_d = q_ref[sl, :] * jnp.exp(b)
        k_d = kk * jnp.exp(-b)
        k_u = kk * jnp.exp(b_last - b)
        dec = jnp.exp(b_last)
        v = i_ref[sl, :]
        outs = []
        for h in range(HGRN_HEADS):
            hs = slice(h * HGRN_DIM, (h + 1) * HGRN_DIM)
            att = jnp.where(keep, _mm_nt(q_d[:, hs], k_d[:, hs]), 0.0)
            st = st_ref[h]
            outs.append(_mm(att, v[:, hs]) + _mm_nt(q_d[:, hs], st))
            st_ref[h] = st * dec[:, hs] + _mm_tn(v[:, hs], k_u[:, hs])
        if final:
            g = g_ref[sl, :]
            silu_g = g * _sigmoid(g)
            for h in range(HGRN_HEADS):
                hs = slice(h * HGRN_DIM, (h + 1) * HGRN_DIM)
                oa = outs[h] + ofwd_ref[sl, hs]
                oa = oa * lax.rsqrt(jnp.mean(oa * oa, axis=-1, keepdims=True) + RMS_EPS)
                o_ref[sl, hs] = (oa * onorm_ref[:, hs] * silu_g[:, hs]).astype(o_ref.dtype)
        else:
            for h in range(HGRN_HEADS):
                hs = slice(h * HGRN_DIM, (h + 1) * HGRN_DIM)
                o_ref[sl, hs] = outs[h]


def _hgrn_dir(proj, lb, *, rev, tb, ofwd=None, onorm=None):
    B, T, _ = proj.shape
    nt = T // tb
    final = ofwd is not None

    def tmap(n):
        return (nt - 1 - n) if rev else n

    def col(c):
        return pl.BlockSpec((None, tb, HGRN_WIDTH), lambda b, n: (b, tmap(n), c))

    vec = pl.BlockSpec((1, HGRN_WIDTH), lambda b, n: (0, 0))
    in_specs = [col(0), col(1), col(3 if rev else 2), vec]
    args = [proj, proj, proj, lb]
    if final:
        in_specs += [col(0), col(4), vec]
        args += [ofwd, proj, onorm]
    return pl.pallas_call(
        functools.partial(_hgrn_body, rev=rev, n_chunks=tb // CHUNK, final=final),
        out_shape=jax.ShapeDtypeStruct((B, T, HGRN_WIDTH), BF16 if final else F32),
        grid=(B, nt),
        in_specs=in_specs,
        out_specs=pl.BlockSpec((None, tb, HGRN_WIDTH), lambda b, n: (b, tmap(n), 0)),
        scratch_shapes=[pltpu.VMEM((HGRN_HEADS, HGRN_DIM, HGRN_DIM), F32)],
        compiler_params=_cparams(("parallel", "arbitrary")),
        name="hgrn_bwd" if rev else "hgrn_fwd",
    )(*args)


def _pair_ones():
    r = lax.broadcasted_iota(jnp.int32, (PAIR, PAIR), 0) // RWKV_DIM
    c = lax.broadcasted_iota(jnp.int32, (PAIR, PAIR), 1) // RWKV_DIM
    return jnp.where(r == c, 1.0, 0.0).astype(BF16)


def _head_sum(x, ones_bd):
    outs = []
    for p in range(N_PAIRS):
        xs = x[:, p * PAIR:(p + 1) * PAIR]
        hi = xs.astype(BF16)
        lo = (xs - hi.astype(F32)).astype(BF16)
        lo2 = (xs - hi.astype(F32) - lo.astype(F32)).astype(BF16)
        outs.append(jnp.dot(hi, ones_bd, preferred_element_type=F32)
                    + jnp.dot(lo, ones_bd, preferred_element_type=F32)
                    + jnp.dot(lo2, ones_bd, preferred_element_type=F32))
    return jnp.concatenate(outs, axis=1)


def _rwkv_prep_body(r_ref, k_ref, v_ref, l_ref,
                    rp_ref, kp_ref, vp_ref, lp_ref, rn_ref, kn_ref, vn_ref, ln_ref,
                    mu_r_ref, mu_k_ref, mu_v_ref, mu_l_ref,
                    w0_ref, w2f_ref, w2b_ref, a0_ref, a2_ref, g2_ref, kkw_ref, kaw_ref, rk_ref,
                    ro_ref, ko_ref, vo_ref, kko_ref, bvo_ref, lwf_ref, lwb_ref, go_ref, bo_ref,
                    *, tb):
    n = pl.program_id(1)
    nt = pl.num_programs(1)
    has_prev = jnp.where(n > 0, 1.0, 0.0).astype(F32)
    has_next = jnp.where(n < nt - 1, 1.0, 0.0).astype(F32)
    rows = lax.broadcasted_iota(jnp.int32, (tb, 1), 0)
    first = rows == 0
    lastr = rows == tb - 1

    def shift(x_ref, p_ref, n_ref, mu_ref):
        x = x_ref[...]
        prev = jnp.where(first, p_ref[7:8, :] * has_prev, pltpu.roll(x, 1, axis=0))
        nxt = jnp.where(lastr, n_ref[0:1, :] * has_next, pltpu.roll(x, tb - 1, axis=0))
        return x + mu_ref[...] * (0.5 * (prev + nxt) - x)

    r = shift(r_ref, rp_ref, rn_ref, mu_r_ref)
    k = shift(k_ref, kp_ref, kn_ref, mu_k_ref)
    v = shift(v_ref, vp_ref, vn_ref, mu_v_ref)
    lo = shift(l_ref, lp_ref, ln_ref, mu_l_ref)
    wd = jnp.tanh(lo[:, 0:LORA_SLAB])
    ag = lo[:, LORA_SLAB:3 * LORA_SLAB]

    def log_decay(w0, w2):
        u = w0 + _mm_lora(wd, w2)
        softplus = jnp.maximum(-u, 0.0) + jnp.log(1.0 + jnp.exp(-jnp.abs(u)))
        return -jnp.exp(-softplus - 0.5)

    lwf_ref[...] = log_decay(w0_ref[0:1, :], w2f_ref[...])
    lwb_ref[...] = log_decay(w0_ref[1:2, :], w2b_ref[...])
    a = _sigmoid(a0_ref[...] + _mm_lora(ag[:, 0:LORA_SLAB], a2_ref[...]))
    go_ref[...] = _mm_lora(_sigmoid(ag), g2_ref[...])

    ones_bd = _pair_ones()
    kk = k * kkw_ref[...]
    norm = jnp.maximum(jnp.sqrt(_head_sum(kk * kk, ones_bd)), 1e-12)
    kk = kk / norm
    k2 = k * (1.0 + (a - 1.0) * kaw_ref[...])
    ro_ref[...] = r
    ko_ref[...] = k2
    vo_ref[...] = v
    kko_ref[...] = kk
    bvo_ref[...] = kk * a
    bo_ref[...] = _head_sum(r * k2 * rk_ref[...], ones_bd) * v


def _rwkv_prep(proj, mu, w0, w2, a0, a2, g2, kkw, kaw, rk, *, tb):
    B, T, _ = proj.shape
    nt = T // tb
    hb = tb // 8
    last8 = T // 8 - 1

    def main(width, c):
        return pl.BlockSpec((None, tb, width), lambda b, n: (b, n, c))

    def prev(width, c):
        return pl.BlockSpec((None, 8, width), lambda b, n: (b, jnp.maximum(n * hb - 1, 0), c))

    def nxt(width, c):
        return pl.BlockSpec((None, 8, width), lambda b, n: (b, jnp.minimum((n + 1) * hb, last8), c))

    def full(a):
        return pl.BlockSpec(a.shape, lambda b, n: (0,) * a.ndim)

    W = RWKV_WIDTH
    c_l = (MIX_A_COLS + 3 * W) // LORA_PAD
    zpad = lambda a, before, total: jnp.pad(a, ((before, total - before - a.shape[0]), (0, 0)))
    params = [mu[:, 0:W], mu[:, W:2 * W], mu[:, 2 * W:3 * W], mu[:, 3 * W:],
              w0, zpad(w2[0], 0, LORA_SLAB), zpad(w2[1], DECAY_LORA, LORA_SLAB), a0,
              zpad(a2, 0, LORA_SLAB), zpad(g2, AAA_LORA, 2 * LORA_SLAB), kkw, kaw, rk]
    in_specs = ([main(W, 5), main(W, 6), main(W, 7), main(LORA_PAD, c_l),
                 prev(W, 5), prev(W, 6), prev(W, 7), prev(LORA_PAD, c_l),
                 nxt(W, 5), nxt(W, 6), nxt(W, 7), nxt(LORA_PAD, c_l)]
                + [full(p) for p in params])
    out = jax.ShapeDtypeStruct((B, T, W), F32)
    return pl.pallas_call(
        functools.partial(_rwkv_prep_body, tb=tb),
        out_shape=[out] * 9,
        grid=(B, nt),
        in_specs=in_specs,
        out_specs=[pl.BlockSpec((None, tb, W), lambda b, n: (b, n, 0))] * 9,
        compiler_params=_cparams(("parallel", "parallel")),
        name="rwkv_prep",
    )(*([proj] * 12), *params)


def _neumann_inverse(mats):
    n = mats[0].shape[0]
    eye = jnp.where(lax.broadcasted_iota(jnp.int32, (n, n), 0) == lax.broadcasted_iota(jnp.int32, (n, n), 1),
                    1.0, 0.0).astype(F32)
    ps = [eye + a for a in mats]
    aks = [_mm_neumann(a, a) for a in mats]
    for _ in range(4):
        ss = [_mm_neumann(jnp.concatenate([p, ak], axis=0), ak) for p, ak in zip(ps, aks)]
        ps = [p + s[:n] for p, s in zip(ps, ss)]
        aks = [s[n:] for s in ss]
    return [p + _mm_neumann(p, ak) for p, ak in zip(ps, aks)]


def _rwkv_scan_body(*refs, rev, final):
    if final:
        (r_ref, k_ref, v_ref, kk_ref, bv_ref, lw_ref, yf_ref, g_ref, bonus_ref, lnw_ref, lnb_ref,
         o_ref, h_ref) = refs
    else:
        r_ref, k_ref, v_ref, kk_ref, bv_ref, lw_ref, o_ref, h_ref = refs

    @pl.when(pl.program_id(1) == 0)
    def _():
        h_ref[...] = jnp.zeros_like(h_ref)

    C = CHUNK
    row = lax.broadcasted_iota(jnp.int32, (C, C), 0)
    col = lax.broadcasted_iota(jnp.int32, (C, C), 1)
    tri = jnp.where((row <= col) if rev else (row >= col), 1.0, 0.0).astype(F32)
    last = 0 if rev else C - 1

    lw = lw_ref[...]
    c_inc = _mm_cumsum(tri, lw)
    c_exc = c_inc - lw
    c_tot = c_inc[last:last + 1, :]
    e_ninc = jnp.exp(-c_inc)
    e_hat = jnp.exp(c_tot - c_inc)
    w_tot = jnp.exp(c_tot)
    kk = kk_ref[...]
    bv = bv_ref[...]
    k2 = k_ref[...]
    v = v_ref[...]
    at = -kk * jnp.exp(c_exc)
    rt = r_ref[...] * jnp.exp(c_inc)
    bt = bv * e_ninc
    kt = k2 * e_ninc
    bh = bv * e_hat
    kh = k2 * e_hat

    lane = lax.broadcasted_iota(jnp.int32, (C, PAIR), 1)
    m0 = lane < RWKV_DIM
    r2 = lax.broadcasted_iota(jnp.int32, (PAIR, 2 * PAIR), 0)
    c2 = lax.broadcasted_iota(jnp.int32, (PAIR, 2 * PAIR), 1) % PAIR
    strict = (r2 < c2) if rev else (r2 > c2)
    incl = (r2 <= c2) if rev else (r2 >= c2)
    eye = (lax.broadcasted_iota(jnp.int32, (PAIR, PAIR), 0)
           == lax.broadcasted_iota(jnp.int32, (PAIR, PAIR), 1))
    zeros = jnp.zeros((PAIR, PAIR), F32)

    pairs = range(N_PAIRS)
    cat = jnp.concatenate

    def stack2(x):
        out = []
        for p in pairs:
            xs = x[:, p * PAIR:(p + 1) * PAIR]
            out.append(cat([jnp.where(m0, xs, 0.0), jnp.where(m0, 0.0, xs)], axis=0))
        return out

    a2, rr2, b2, kt2, v2 = stack2(at), stack2(rt), stack2(bt), stack2(kt), stack2(v)
    bh2, kh2 = stack2(bh), stack2(kh)
    pm = [_mm_nt(cat([a2[p], rr2[p]], axis=0), cat([b2[p], kt2[p]], axis=0)) for p in pairs]
    a_abk = [jnp.where(strict, pm[p][:PAIR, :], 0.0) for p in pairs]
    a_rbk = [jnp.where(incl, pm[p][PAIR:, :], 0.0) for p in pairs]
    tinv = _neumann_inverse([a_abk[p][:, :PAIR] for p in pairs])
    x1 = [_mm(a_abk[p][:, PAIR:], v2[p]) for p in pairs]
    z = [_mm(tinv[p], cat([a2[p], x1[p]], axis=1)) for p in pairs]
    w2 = [cat([z[p], cat([zeros, v2[p]], axis=1)], axis=0) for p in pairs]
    mg = [_mm_tn(cat([bh2[p], kh2[p]], axis=0), w2[p]) for p in pairs]
    ry = [_mm(a_rbk[p], w2[p]) for p in pairs]
    lhs = [cat([rr2[p] + ry[p][:, :PAIR],
                mg[p][:, :PAIR] + jnp.where(eye, w_tot[:, p * PAIR:(p + 1) * PAIR], 0.0)], axis=0)
           for p in pairs]
    yh = [_mm(lhs[p], h_ref[p]) for p in pairs]
    h_ref[...] = jnp.stack([yh[p][PAIR:] + mg[p][:, PAIR:] for p in pairs], axis=0)
    y2 = [yh[p][:PAIR] + ry[p][:, PAIR:] for p in pairs]
    y = cat([y2[p][:C] + y2[p][C:] for p in pairs], axis=1)

    if final:
        ones_bd = _pair_ones()
        y = y + yf_ref[...]
        mean = _head_sum(y, ones_bd) * (1.0 / RWKV_DIM)
        yc = y - mean
        var = _head_sum(yc * yc, ones_bd) * (1.0 / RWKV_DIM)
        yn = yc * lax.rsqrt(var + GN_EPS) * lnw_ref[...] + lnb_ref[...]
        o_ref[...] = ((yn + bonus_ref[...]) * g_ref[...]).astype(o_ref.dtype)
    else:
        o_ref[...] = y


def _rwkv_dir(r, k2, v, kk, bv, lw, *, rev, fin=None):
    B, T, W = r.shape
    nc = T // CHUNK

    def tmap(n):
        return (nc - 1 - n) if rev else n

    blk = pl.BlockSpec((None, CHUNK, W), lambda b, n: (b, tmap(n), 0))
    vec = pl.BlockSpec((1, W), lambda b, n: (0, 0))
    in_specs = [blk] * 6
    args = [r, k2, v, kk, bv, lw]
    if fin is not None:
        in_specs += [blk, blk, blk, vec, vec]
        args += list(fin)
    return pl.pallas_call(
        functools.partial(_rwkv_scan_body, rev=rev, final=fin is not None),
        out_shape=jax.ShapeDtypeStruct((B, T, W), BF16 if fin is not None else F32),
        grid=(B, nc),
        in_specs=in_specs,
        out_specs=blk,
        scratch_shapes=[pltpu.VMEM((N_PAIRS, PAIR, PAIR), F32)],
        compiler_params=_cparams(("parallel", "arbitrary")),
        name="rwkv_bwd" if rev else "rwkv_fwd",
    )(*args)


def _attn_body(sink_ref, q_ref, kp_ref, kc_ref, kn_ref, vp_ref, vc_ref, vn_ref, o_ref):
    n = pl.program_id(1)
    nb = pl.num_programs(1)
    r = lax.broadcasted_iota(jnp.int32, (BLOCK, 3 * BLOCK), 0)
    c = lax.broadcasted_iota(jnp.int32, (BLOCK, 3 * BLOCK), 1)
    d = c - BLOCK - r
    lo = jnp.where(n > 0, 0, BLOCK)
    hi = jnp.where(n < nb - 1, 3 * BLOCK, 2 * BLOCK)
    valid = (d >= -BLOCK) & (d <= BLOCK) & (c >= lo) & (c < hi)
    scale = ATT_DIM ** -0.5
    for kh in range(ATT_KV_HEADS):
        ks = slice(kh * ATT_DIM, (kh + 1) * ATT_DIM)
        kw = jnp.concatenate([kp_ref[:, ks], kc_ref[:, ks], kn_ref[:, ks]], axis=0)
        vw = jnp.concatenate([vp_ref[:, ks], vc_ref[:, ks], vn_ref[:, ks]], axis=0)
        heads = [kh * ATT_GROUP + g for g in range(ATT_GROUP)]
        qg = jnp.concatenate([q_ref[:, h * ATT_DIM:(h + 1) * ATT_DIM] for h in heads], axis=0)
        s_all = lax.dot_general(qg, kw, (((1,), (1,)), ((), ())), preferred_element_type=F32) * scale
        ps = []
        for g, h in enumerate(heads):
            s = jnp.where(valid, s_all[g * BLOCK:(g + 1) * BLOCK, :], -jnp.inf)
            sk = sink_ref[h]
            m = jnp.maximum(jnp.max(s, axis=-1, keepdims=True), sk)
            e = jnp.exp(s - m)
            den = jnp.sum(e, axis=-1, keepdims=True) + jnp.exp(sk - m)
            ps.append((e / den).astype(BF16))
        o_all = jnp.dot(jnp.concatenate(ps, axis=0), vw, preferred_element_type=F32)
        for g, h in enumerate(heads):
            o_ref[:, h * ATT_DIM:(h + 1) * ATT_DIM] = o_all[g * BLOCK:(g + 1) * BLOCK, :].astype(o_ref.dtype)


def _attention(qkv, sink):
    B, T, _ = qkv.shape
    nb = T // BLOCK
    kvw = ATT_KV_HEADS * ATT_DIM
    kc = (ATT_HEADS * ATT_DIM) // kvw
    vc = kc + 1

    def blk(cidx, off):
        def imap(b, n):
            return (b, jnp.clip(n + off, 0, nb - 1), cidx)
        return pl.BlockSpec((None, BLOCK, kvw), imap)

    return pl.pallas_call(
        _attn_body,
        out_shape=jax.ShapeDtypeStruct((B, T, ATT_HEADS * ATT_DIM), BF16),
        grid=(B, nb),
        in_specs=[pl.BlockSpec(memory_space=pltpu.SMEM),
                  pl.BlockSpec((None, BLOCK, ATT_HEADS * ATT_DIM), lambda b, n: (b, n, 0)),
                  blk(kc, -1), blk(kc, 0), blk(kc, 1), blk(vc, -1), blk(vc, 0), blk(vc, 1)],
        out_specs=pl.BlockSpec((None, BLOCK, ATT_HEADS * ATT_DIM), lambda b, n: (b, n, 0)),
        compiler_params=_cparams(("parallel", "parallel")),
        name="attention",
    )(sink, qkv, qkv, qkv, qkv, qkv, qkv, qkv)


def _ffn_body(x_ref, xp_ref, xn_ref, g_ref, wg_ref, wv_ref, cw_ref, cb_ref, wd_ref, fg_ref, o_ref, h_ref,
              *, tm, tiles_per_seq, final_norm):
    i = pl.program_id(0)
    f = pl.program_id(1)
    nf = pl.num_programs(1)

    @pl.when(f == 0)
    def _():
        def norm(x):
            ms = jnp.mean(x * x, axis=-1, keepdims=True)
            return (x * lax.rsqrt(ms + RMS_EPS) * g_ref[...]).astype(BF16)
        x = x_ref[...]
        h_ref[0:tm, :] = norm(x)
        h_ref[tm:tm + 16, :] = norm(jnp.concatenate([xp_ref[...], xn_ref[...]], axis=0))
        o_ref[...] = x

    t = i % tiles_per_seq
    has_prev = jnp.where(t > 0, 1.0, 0.0).astype(F32)
    has_next = jnp.where(t < tiles_per_seq - 1, 1.0, 0.0).astype(F32)
    ge = jnp.dot(h_ref[...], wg_ref[...], preferred_element_type=F32)
    gm = ge[0:tm, :]
    rows = lax.broadcasted_iota(jnp.int32, (tm, 1), 0)
    g_prev = jnp.where(rows == 0, ge[tm + 7:tm + 8, :] * has_prev, pltpu.roll(gm, 1, axis=0))
    g_next = jnp.where(rows == tm - 1, ge[tm + 8:tm + 9, :] * has_next, pltpu.roll(gm, tm - 1, axis=0))
    gate = g_prev * cw_ref[0:1, :] + gm * cw_ref[1:2, :] + g_next * cw_ref[2:3, :] + cb_ref[...]
    val = jnp.dot(h_ref[0:tm, :], wv_ref[...], preferred_element_type=F32)
    act = (gate * _sigmoid(gate) * val).astype(BF16)
    o_ref[...] += jnp.dot(act, wd_ref[...], preferred_element_type=F32)

    if final_norm:
        @pl.when(f == nf - 1)
        def _():
            y = o_ref[...]
            ms = jnp.mean(y * y, axis=-1, keepdims=True)
            o_ref[...] = y * lax.rsqrt(ms + RMS_EPS) * fg_ref[...]


def _ffn(x, g, w_up, conv_w, conv_b, w_down, final_g, *, seq_len, tm, tf, final_norm):
    M, D = x.shape
    nf = D_FF // tf
    hb = tm // 8
    last8 = M // 8 - 1
    in_specs = [
        pl.BlockSpec((tm, D), lambda i, f: (i, 0)),
        pl.BlockSpec((8, D), lambda i, f: (jnp.maximum(i * hb - 1, 0), 0)),
        pl.BlockSpec((8, D), lambda i, f: (jnp.minimum((i + 1) * hb, last8), 0)),
        pl.BlockSpec((1, D), lambda i, f: (0, 0)),
        pl.BlockSpec((D, tf), lambda i, f: (0, f)),
        pl.BlockSpec((D, tf), lambda i, f: (0, nf + f)),
        pl.BlockSpec((3, tf), lambda i, f: (0, f)),
        pl.BlockSpec((1, tf), lambda i, f: (0, f)),
        pl.BlockSpec((tf, D), lambda i, f: (f, 0)),
        pl.BlockSpec((1, D), lambda i, f: (0, 0)),
    ]
    return pl.pallas_call(
        functools.partial(_ffn_body, tm=tm, tiles_per_seq=seq_len // tm, final_norm=final_norm),
        out_shape=jax.ShapeDtypeStruct((M, D), F32),
        grid=(M // tm, nf),
        in_specs=in_specs,
        out_specs=pl.BlockSpec((tm, D), lambda i, f: (i, 0)),
        scratch_shapes=[pltpu.VMEM((tm + 16, D), BF16)],
        compiler_params=_cparams(("parallel", "arbitrary")),
        name="conv_ffn",
    )(x, x, x, g, w_up, w_up, conv_w, conv_b, w_down, final_g)


def _pick(n, prefs):
    for p in prefs:
        if n % p == 0:
            return p
    raise ValueError(f"no tile for {n}")


def _rope_tables(T):
    half = ATT_DIM // 2
    inv = ROPE_THETA ** (-jnp.arange(half, dtype=F32) / half)
    ang = jnp.arange(T, dtype=F32)[:, None] * inv[None, :]
    cos = jnp.cos(ang)
    sin = jnp.sin(ang)
    return jnp.concatenate([cos, cos], axis=1), jnp.concatenate([-sin, sin], axis=1)


def _prepare_params(p):
    q = dict(p)
    w_in = p['ab_w_in'][0]
    q['w_in'] = jnp.pad(w_in, ((0, 0), (0, MIX_COLS_PAD - MIX_COLS))).astype(BF16)
    mu = p['rwkv_mu'][0]
    q['mu'] = jnp.pad(mu, (0, LORA_PAD - LORA_COLS))[None, :]
    q['lb'] = jnp.cumsum(jax.nn.softmax(p['hgrn_lb'].astype(F32), axis=0), axis=0)
    q['w_out'] = p['ab_w_out'][0].astype(BF16)
    q['w_qkv'] = p['att_w_qkv'][0].astype(BF16)
    q['w_o'] = p['att_w_o'][0].astype(BF16)
    q['w_up'] = p['ffn_w_up'].astype(BF16)
    q['w_down'] = p['ffn_w_down'].astype(BF16)
    return q


def _mixer_layer(x2, q, B, T, layer):
    M = B * T
    tm = _pick(M, (512, 256, 128))
    proj = _norm_matmul(x2, q['mix_norm'][layer][None, :], q['w_in'], tm=_pick(M, (1024, 512, 256, 128)),
                        tn=512, out_dtype=F32)
    proj = proj.reshape(B, T, MIX_COLS_PAD)
    lb = q['lb'][layer][None, :]
    tb = _pick(T, (256, 128, 64))
    o_fwd = _hgrn_dir(proj, lb, rev=False, tb=tb)
    ya = _hgrn_dir(proj, lb, rev=True, tb=tb, ofwd=o_fwd, onorm=q['hgrn_onorm'][0][None, :])

    r, k2, v, kk, bv, lwf, lwb, g, bonus = _rwkv_prep(
        proj, q['mu'], q['rwkv_w0'][0], q['rwkv_w2'][0], q['rwkv_a0'][0][None, :], q['rwkv_a2'][0],
        q['rwkv_g2'][0], q['rwkv_kk'][0][None, :], q['rwkv_ka'][0][None, :],
        q['rwkv_rk'][0].reshape(1, RWKV_WIDTH), tb=tb)
    y_fwd = _rwkv_dir(r, k2, v, kk, bv, lwf, rev=False)
    yb = _rwkv_dir(r, k2, v, kk, bv, lwb, rev=True,
                   fin=(y_fwd, g, bonus, q['rwkv_ln_w'][0][None, :], q['rwkv_ln_b'][0][None, :]))
    return _matmul_res([ya.reshape(M, HGRN_WIDTH), yb.reshape(M, RWKV_WIDTH)], q['w_out'], x2,
                       tm=tm, tn=D_MODEL)


def _attention_layer(x2, q, B, T, layer, rope_tabs):
    M = B * T
    tm = _pick(M, (512, 256, 128))
    cos, sin = rope_tabs
    qkv = _norm_matmul(x2, q['mix_norm'][layer][None, :], q['w_qkv'], tm=tm, tn=QKV_COLS, out_dtype=BF16,
                       rope=(cos, sin, ATT_HEADS + ATT_KV_HEADS))
    o = _attention(qkv.reshape(B, T, QKV_COLS), q['att_sink'][0])
    return _matmul_res([o.reshape(M, ATT_HEADS * ATT_DIM)], q['w_o'], x2, tm=tm, tn=D_MODEL)


def _trunk(x, q):
    B, T, D = x.shape
    M = B * T
    x2 = x.reshape(M, D)
    rope_tabs = _rope_tables(T)
    tm = _pick(T, (512, 256, 128))
    for layer in range(DEPTH):
        if layer % 2 == 0:
            x2 = _mixer_layer(x2, q, B, T, layer)
        else:
            x2 = _attention_layer(x2, q, B, T, layer, rope_tabs)
        x2 = _ffn(x2, q['ffn_norm'][layer][None, :], q['w_up'][layer], q['ffn_conv_w'][layer],
                  q['ffn_conv_b'][layer][None, :], q['w_down'][layer], q['final_norm'][None, :],
                  seq_len=T, tm=tm, tf=512, final_norm=(layer == DEPTH - 1))
    return x2.reshape(B, T, D)


def kernel(x_prompt, x_sample, mix_norm, ab_w_in, hgrn_lb, hgrn_onorm, rwkv_mu, rwkv_w0, rwkv_w2, rwkv_a0, rwkv_a2, rwkv_g2, rwkv_kk, rwkv_ka, rwkv_rk, rwkv_ln_w, rwkv_ln_b, ab_w_out, att_w_qkv, att_sink, att_w_o, ffn_norm, ffn_w_up, ffn_conv_w, ffn_conv_b, ffn_w_down, final_norm):
    p = {
        'mix_norm': mix_norm, 'ab_w_in': ab_w_in, 'hgrn_lb': hgrn_lb, 'hgrn_onorm': hgrn_onorm,
        'rwkv_mu': rwkv_mu, 'rwkv_w0': rwkv_w0, 'rwkv_w2': rwkv_w2, 'rwkv_a0': rwkv_a0, 'rwkv_a2': rwkv_a2,
        'rwkv_g2': rwkv_g2, 'rwkv_kk': rwkv_kk, 'rwkv_ka': rwkv_ka, 'rwkv_rk': rwkv_rk,
        'rwkv_ln_w': rwkv_ln_w, 'rwkv_ln_b': rwkv_ln_b, 'ab_w_out': ab_w_out,
        'att_w_qkv': att_w_qkv, 'att_sink': att_sink, 'att_w_o': att_w_o,
        'ffn_norm': ffn_norm, 'ffn_w_up': ffn_w_up, 'ffn_conv_w': ffn_conv_w, 'ffn_conv_b': ffn_conv_b,
        'ffn_w_down': ffn_w_down, 'final_norm': final_norm,
    }
    q = _prepare_params(p)
    return (_trunk(x_prompt, q), _trunk(x_sample, q))
```

```python
import functools

import jax
import jax.numpy as jnp
from jax import lax
from jax.experimental import pallas as pl
from jax.experimental.pallas import tpu as pltpu

F32 = jnp.float32
BF16 = jnp.bfloat16

D_MODEL = 2048
DEPTH = 2
HGRN_DIM = 128
HGRN_HEADS = 8
HGRN_WIDTH = 1024
RWKV_DIM = 64
RWKV_HEADS = 16
RWKV_WIDTH = 1024
DECAY_LORA = 64
AAA_LORA = 64
GATE_LORA = 160
LORA_COLS = 2 * DECAY_LORA + AAA_LORA + GATE_LORA
LORA_PAD = 512
LORA_SLAB = 128
MIX_A_COLS = 5 * HGRN_WIDTH
MIX_COLS = MIX_A_COLS + 3 * RWKV_WIDTH + LORA_COLS
MIX_COLS_PAD = MIX_A_COLS + 3 * RWKV_WIDTH + LORA_PAD
ATT_DIM = 128
ATT_HEADS = 16
ATT_KV_HEADS = 4
ATT_GROUP = 4
QKV_COLS = (ATT_HEADS + 2 * ATT_KV_HEADS) * ATT_DIM
BLOCK = 128
ROPE_THETA = 10000.0
D_FF = 5632
RMS_EPS = 1e-6
GN_EPS = 64e-5
CHUNK = 64
PAIR = 2 * RWKV_DIM
N_PAIRS = RWKV_WIDTH // PAIR

VMEM_LIMIT = 56 * 1024 * 1024
HI = lax.Precision.HIGHEST


def _cparams(sem):
    return pltpu.CompilerParams(dimension_semantics=sem, vmem_limit_bytes=VMEM_LIMIT)


def _mm(a, b):
    return jnp.dot(a.astype(BF16), b.astype(BF16), preferred_element_type=F32)


def _mm_nt(a, b):
    return lax.dot_general(a.astype(BF16), b.astype(BF16), (((1,), (1,)), ((), ())),
                           preferred_element_type=F32)


def _mm_tn(a, b):
    return lax.dot_general(a.astype(BF16), b.astype(BF16), (((0,), (0,)), ((), ())),
                           preferred_element_type=F32)


def _mm_hi(a, b):
    return jnp.dot(a, b, preferred_element_type=F32, precision=HI)


def _split3(x):
    hi = x.astype(BF16)
    r1 = x - hi.astype(F32)
    mid = r1.astype(BF16)
    lo = (r1 - mid.astype(F32)).astype(BF16)
    return hi, mid, lo


def _mm_cumsum(tri, x):
    t = tri.astype(BF16)
    hi, mid, lo = _split3(x)
    return (jnp.dot(t, hi, preferred_element_type=F32) + jnp.dot(t, mid, preferred_element_type=F32)
            + jnp.dot(t, lo, preferred_element_type=F32))


_mm_lora = _mm
_mm_neumann = _mm


def _sigmoid(x):
    return 1.0 / (1.0 + jnp.exp(-x))


def _norm_matmul_body(x_ref, g_ref, w_ref, o_ref, h_ref):
    @pl.when(pl.program_id(1) == 0)
    def _():
        x = x_ref[...]
        ms = jnp.mean(x * x, axis=-1, keepdims=True)
        h_ref[...] = (x * lax.rsqrt(ms + RMS_EPS) * g_ref[...]).astype(BF16)

    o_ref[...] = jnp.dot(h_ref[...], w_ref[...], preferred_element_type=F32).astype(o_ref.dtype)


def _norm_matmul_rope_body(x_ref, g_ref, w_ref, cos_ref, sin_ref, o_ref, h_ref, *, n_rope, tn):
    x = x_ref[...]
    ms = jnp.mean(x * x, axis=-1, keepdims=True)
    h_ref[...] = (x * lax.rsqrt(ms + RMS_EPS) * g_ref[...]).astype(BF16)
    cos = cos_ref[...]
    sin = sin_ref[...]
    group = 4 * ATT_DIM
    for gi in range(tn // group):
        acc = jnp.dot(h_ref[...], w_ref[:, gi * group:(gi + 1) * group], preferred_element_type=F32)
        for hh in range(group // ATT_DIM):
            a = acc[:, hh * ATT_DIM:(hh + 1) * ATT_DIM]
            head = gi * (group // ATT_DIM) + hh
            if head < n_rope:
                a = a * cos + pltpu.roll(a, ATT_DIM // 2, axis=1) * sin
            o_ref[:, head * ATT_DIM:(head + 1) * ATT_DIM] = a.astype(o_ref.dtype)


def _norm_matmul(x, g, w, *, tm, tn, out_dtype, rope=None):
    M, K = x.shape
    N = w.shape[1]
    grid = (M // tm, N // tn)
    in_specs = [pl.BlockSpec((tm, K), lambda i, j: (i, 0)),
                pl.BlockSpec((1, K), lambda i, j: (0, 0)),
                pl.BlockSpec((K, tn), lambda i, j: (0, j))]
    args = [x, g, w]
    if rope is None:
        body = _norm_matmul_body
    else:
        cos, sin, n_rope = rope
        tiles_per_seq = cos.shape[0] // tm
        in_specs += [pl.BlockSpec((tm, ATT_DIM), lambda i, j: (i % tiles_per_seq, 0)),
                     pl.BlockSpec((tm, ATT_DIM), lambda i, j: (i % tiles_per_seq, 0))]
        args += [cos, sin]
        body = functools.partial(_norm_matmul_rope_body, n_rope=n_rope, tn=tn)
    return pl.pallas_call(
        body,
        out_shape=jax.ShapeDtypeStruct((M, N), out_dtype),
        grid=grid,
        in_specs=in_specs,
        out_specs=pl.BlockSpec((tm, tn), lambda i, j: (i, j)),
        scratch_shapes=[pltpu.VMEM((tm, K), BF16)],
        compiler_params=_cparams(("parallel", "arbitrary")),
        name="norm_matmul" if rope is None else "norm_matmul_rope",
    )(*args)


def _matmul_res_body(*refs, n_parts):
    a_refs = refs[:n_parts]
    w_ref, r_ref, o_ref = refs[n_parts:]
    acc = r_ref[...]
    k0 = 0
    for a_ref in a_refs:
        kw = a_ref.shape[1]
        acc = acc + jnp.dot(a_ref[...], w_ref[k0:k0 + kw, :], preferred_element_type=F32)
        k0 += kw
    o_ref[...] = acc


def _matmul_res(a_parts, w, res, *, tm, tn):
    M, N = res.shape
    K = w.shape[0]
    in_specs = [pl.BlockSpec((tm, a.shape[1]), lambda i, j: (i, 0)) for a in a_parts]
    in_specs += [pl.BlockSpec((K, tn), lambda i, j: (0, j)),
                 pl.BlockSpec((tm, tn), lambda i, j: (i, j))]
    return pl.pallas_call(
        functools.partial(_matmul_res_body, n_parts=len(a_parts)),
        out_shape=jax.ShapeDtypeStruct((M, N), F32),
        grid=(M // tm, N // tn),
        in_specs=in_specs,
        out_specs=pl.BlockSpec((tm, tn), lambda i, j: (i, j)),
        compiler_params=_cparams(("parallel", "arbitrary")),
        name="matmul_res",
    )(*a_parts, w, res)


def _hgrn_body(*refs, rev, n_chunks, final):
    if final:
        q_ref, i_ref, z_ref, lb_ref, ofwd_ref, g_ref, onorm_ref, o_ref, st_ref = refs
    else:
        q_ref, i_ref, z_ref, lb_ref, o_ref, st_ref = refs

    @pl.when(pl.program_id(1) == 0)
    def _():
        st_ref[...] = jnp.zeros_like(st_ref)

    lb = lb_ref[...]
    row = lax.broadcasted_iota(jnp.int32, (CHUNK, CHUNK), 0)
    col = lax.broadcasted_iota(jnp.int32, (CHUNK, CHUNK), 1)
    keep = (row <= col) if rev else (row >= col)
    tri = jnp.where(keep, 1.0, 0.0).astype(F32)
    last = 0 if rev else CHUNK - 1

    head_slices = [slice(h * HGRN_DIM, (h + 1) * HGRN_DIM) for h in range(HGRN_HEADS)]
    sts = [st_ref[h] for h in range(HGRN_HEADS)]
    order = range(n_chunks - 1, -1, -1) if rev else range(n_chunks)
    for ci in order:
        sl = slice(ci * CHUNK, (ci + 1) * CHUNK)
        f = lb + (1.0 - lb) * _sigmoid(z_ref[sl, :])
        kk = 1.0 - f
        b = _mm_cumsum(tri, jnp.log(f))
        b_last = b[last:last + 1, :]
        q_d = q_ref[sl, :] * jnp.exp(b)
        k_d = kk * jnp.exp(-b)
        k_u = kk * jnp.exp(b_last - b)
        dec = jnp.exp(b_last)
        v = i_ref[sl, :]
        att = [jnp.where(keep, _mm_nt(q_d[:, hs], k_d[:, hs]), 0.0) for hs in head_slices]
        outs = [_mm(att[h], v[:, hs]) + _mm_nt(q_d[:, hs], sts[h]) for h, hs in enumerate(head_slices)]
        upd = [_mm_tn(v[:, hs], k_u[:, hs]) for hs in head_slices]
        sts = [sts[h] * dec[:, hs] + upd[h] for h, hs in enumerate(head_slices)]
        if final:
            g = g_ref[sl, :]
            silu_g = g * _sigmoid(g)
            normed = []
            for h, hs in enumerate(head_slices):
                oa = outs[h] + ofwd_ref[sl, hs]
                normed.append(oa * lax.rsqrt(jnp.mean(oa * oa, axis=-1, keepdims=True) + RMS_EPS))
            o_ref[sl, :] = (jnp.concatenate(normed, axis=1) * onorm_ref[...] * silu_g).astype(o_ref.dtype)
        else:
            o_ref[sl, :] = jnp.concatenate(outs, axis=1)
    st_ref[...] = jnp.stack(sts, axis=0)


def _hgrn_dir(proj, lb, *, rev, tb, ofwd=None, onorm=None):
    B, T, _ = proj.shape
    nt = T // tb
    final = ofwd is not None

    def tmap(n):
        return (nt - 1 - n) if rev else n

    def col(c):
        return pl.BlockSpec((None, tb, HGRN_WIDTH), lambda b, n: (b, tmap(n), c))

    vec = pl.BlockSpec((1, HGRN_WIDTH), lambda b, n: (0, 0))
    in_specs = [col(0), col(1), col(3 if rev else 2), vec]
    args = [proj, proj, proj, lb]
    if final:
        in_specs += [col(0), col(4), vec]
        args += [ofwd, proj, onorm]
    return pl.pallas_call(
        functools.partial(_hgrn_body, rev=rev, n_chunks=tb // CHUNK, final=final),
        out_shape=jax.ShapeDtypeStruct((B, T, HGRN_WIDTH), BF16 if final else F32),
        grid=(B, nt),
        in_specs=in_specs,
        out_specs=pl.BlockSpec((None, tb, HGRN_WIDTH), lambda b, n: (b, tmap(n), 0)),
        scratch_shapes=[pltpu.VMEM((HGRN_HEADS, HGRN_DIM, HGRN_DIM), F32)],
        compiler_params=_cparams(("parallel", "arbitrary")),
        name="hgrn_bwd" if rev else "hgrn_fwd",
    )(*args)


def _pair_ones():
    r = lax.broadcasted_iota(jnp.int32, (PAIR, PAIR), 0) // RWKV_DIM
    c = lax.broadcasted_iota(jnp.int32, (PAIR, PAIR), 1) // RWKV_DIM
    return jnp.where(r == c, 1.0, 0.0).astype(BF16)


def _head_sum(x, ones_bd):
    outs = []
    for p in range(N_PAIRS):
        xs = x[:, p * PAIR:(p + 1) * PAIR]
        hi = xs.astype(BF16)
        lo = (xs - hi.astype(F32)).astype(BF16)
        lo2 = (xs - hi.astype(F32) - lo.astype(F32)).astype(BF16)
        outs.append(jnp.dot(hi, ones_bd, preferred_element_type=F32)
                    + jnp.dot(lo, ones_bd, preferred_element_type=F32)
                    + jnp.dot(lo2, ones_bd, preferred_element_type=F32))
    return jnp.concatenate(outs, axis=1)


def _rwkv_prep_body(r_ref, k_ref, v_ref, l_ref,
                    rp_ref, kp_ref, vp_ref, lp_ref, rn_ref, kn_ref, vn_ref, ln_ref,
                    mu_r_ref, mu_k_ref, mu_v_ref, mu_l_ref,
                    w0_ref, w2f_ref, w2b_ref, a0_ref, a2_ref, g2_ref, kkw_ref, kaw_ref, rk_ref,
                    ro_ref, ko_ref, vo_ref, kko_ref, bvo_ref, lwf_ref, lwb_ref, go_ref, bo_ref,
                    *, tb):
    n = pl.program_id(1)
    nt = pl.num_programs(1)
    has_prev = jnp.where(n > 0, 1.0, 0.0).astype(F32)
    has_next = jnp.where(n < nt - 1, 1.0, 0.0).astype(F32)
    rows = lax.broadcasted_iota(jnp.int32, (tb, 1), 0)
    first = rows == 0
    lastr = rows == tb - 1

    def shift(x_ref, p_ref, n_ref, mu_ref):
        x = x_ref[...]
        prev = jnp.where(first, p_ref[7:8, :] * has_prev, pltpu.roll(x, 1, axis=0))
        nxt = jnp.where(lastr, n_ref[0:1, :] * has_next, pltpu.roll(x, tb - 1, axis=0))
        return x + mu_ref[...] * (0.5 * (prev + nxt) - x)

    r = shift(r_ref, rp_ref, rn_ref, mu_r_ref)
    k = shift(k_ref, kp_ref, kn_ref, mu_k_ref)
    v = shift(v_ref, vp_ref, vn_ref, mu_v_ref)
    lo = shift(l_ref, lp_ref, ln_ref, mu_l_ref)
    wd = jnp.tanh(lo[:, 0:LORA_SLAB])
    ag = lo[:, LORA_SLAB:3 * LORA_SLAB]

    def log_decay(w0, w2):
        u = w0 + _mm_lora(wd, w2)
        softplus = jnp.maximum(-u, 0.0) + jnp.log(1.0 + jnp.exp(-jnp.abs(u)))
        return -jnp.exp(-softplus - 0.5)

    lwf_ref[...] = log_decay(w0_ref[0:1, :], w2f_ref[...])
    lwb_ref[...] = log_decay(w0_ref[1:2, :], w2b_ref[...])
    a = _sigmoid(a0_ref[...] + _mm_lora(ag[:, 0:LORA_SLAB], a2_ref[...]))
    go_ref[...] = _mm_lora(_sigmoid(ag), g2_ref[...])

    ones_bd = _pair_ones()
    kk = k * kkw_ref[...]
    norm = jnp.maximum(jnp.sqrt(_head_sum(kk * kk, ones_bd)), 1e-12)
    kk = kk / norm
    k2 = k * (1.0 + (a - 1.0) * kaw_ref[...])
    ro_ref[...] = r
    ko_ref[...] = k2
    vo_ref[...] = v
    kko_ref[...] = kk
    bvo_ref[...] = kk * a
    bo_ref[...] = _head_sum(r * k2 * rk_ref[...], ones_bd) * v


def _rwkv_prep(proj, mu, w0, w2, a0, a2, g2, kkw, kaw, rk, *, tb):
    B, T, _ = proj.shape
    nt = T // tb
    hb = tb // 8
    last8 = T // 8 - 1

    def main(width, c):
        return pl.BlockSpec((None, tb, width), lambda b, n: (b, n, c))

    def prev(width, c):
        return pl.BlockSpec((None, 8, width), lambda b, n: (b, jnp.maximum(n * hb - 1, 0), c))

    def nxt(width, c):
        return pl.BlockSpec((None, 8, width), lambda b, n: (b, jnp.minimum((n + 1) * hb, last8), c))

    def full(a):
        return pl.BlockSpec(a.shape, lambda b, n: (0,) * a.ndim)

    W = RWKV_WIDTH
    c_l = (MIX_A_COLS + 3 * W) // LORA_PAD
    zpad = lambda a, before, total: jnp.pad(a, ((before, total - before - a.shape[0]), (0, 0)))
    params = [mu[:, 0:W], mu[:, W:2 * W], mu[:, 2 * W:3 * W], mu[:, 3 * W:],
              w0, zpad(w2[0], 0, LORA_SLAB), zpad(w2[1], DECAY_LORA, LORA_SLAB), a0,
              zpad(a2, 0, LORA_SLAB), zpad(g2, AAA_LORA, 2 * LORA_SLAB), kkw, kaw, rk]
    in_specs = ([main(W, 5), main(W, 6), main(W, 7), main(LORA_PAD, c_l),
                 prev(W, 5), prev(W, 6), prev(W, 7), prev(LORA_PAD, c_l),
                 nxt(W, 5), nxt(W, 6), nxt(W, 7), nxt(LORA_PAD, c_l)]
                + [full(p) for p in params])
    out = jax.ShapeDtypeStruct((B, T, W), F32)
    return pl.pallas_call(
        functools.partial(_rwkv_prep_body, tb=tb),
        out_shape=[out] * 9,
        grid=(B, nt),
        in_specs=in_specs,
        out_specs=[pl.BlockSpec((None, tb, W), lambda b, n: (b, n, 0))] * 9,
        compiler_params=_cparams(("parallel", "parallel")),
        name="rwkv_prep",
    )(*([proj] * 12), *params)


def _neumann_inverse(mats):
    n = mats[0].shape[0]
    eye = jnp.where(lax.broadcasted_iota(jnp.int32, (n, n), 0) == lax.broadcasted_iota(jnp.int32, (n, n), 1),
                    1.0, 0.0).astype(F32)
    ps = [eye + a for a in mats]
    aks = [_mm_neumann(a, a) for a in mats]
    for _ in range(4):
        ss = [_mm_neumann(ak, jnp.concatenate([ak, p], axis=1)) for p, ak in zip(ps, aks)]
        ps = [p + s[:, n:] for p, s in zip(ps, ss)]
        aks = [s[:, :n] for s in ss]
    return [p + _mm_neumann(ak, p) for p, ak in zip(ps, aks)]


def _rwkv_scan_body(*refs, rev, final, n_chunks):
    h_ref = refs[-1]

    @pl.when(pl.program_id(1) == 0)
    def _():
        h_ref[...] = jnp.zeros_like(h_ref)

    def body(i, carry):
        ci = (n_chunks - 1 - i) if rev else i
        _rwkv_chunk(refs, pl.ds(pl.multiple_of(ci * CHUNK, CHUNK), CHUNK), rev=rev, final=final)
        return carry

    lax.fori_loop(0, n_chunks, body, 0)


def _rwkv_chunk(refs, sl, *, rev, final):
    if final:
        (r_ref, k_ref, v_ref, kk_ref, bv_ref, lw_ref, yf_ref, g_ref, bonus_ref, lnw_ref, lnb_ref,
         o_ref, h_ref) = refs
    else:
        r_ref, k_ref, v_ref, kk_ref, bv_ref, lw_ref, o_ref, h_ref = refs

    C = CHUNK
    row = lax.broadcasted_iota(jnp.int32, (C, C), 0)
    col = lax.broadcasted_iota(jnp.int32, (C, C), 1)
    tri = jnp.where((row <= col) if rev else (row >= col), 1.0, 0.0).astype(F32)
    last = 0 if rev else C - 1

    lw = lw_ref[sl, :]
    c_inc = _mm_cumsum(tri, lw)
    c_exc = c_inc - lw
    c_tot = c_inc[last:last + 1, :]
    e_ninc = jnp.exp(-c_inc)
    e_hat = jnp.exp(c_tot - c_inc)
    w_tot = jnp.exp(c_tot)
    kk = kk_ref[sl, :]
    bv = bv_ref[sl, :]
    k2 = k_ref[sl, :]
    v = v_ref[sl, :]
    at = -kk * jnp.exp(c_exc)
    rt = r_ref[sl, :] * jnp.exp(c_inc)
    bt = bv * e_ninc
    kt = k2 * e_ninc
    bh = bv * e_hat
    kh = k2 * e_hat

    lane = lax.broadcasted_iota(jnp.int32, (C, PAIR), 1)
    m0 = lane < RWKV_DIM
    r2 = lax.broadcasted_iota(jnp.int32, (PAIR, 2 * PAIR), 0)
    c2 = lax.broadcasted_iota(jnp.int32, (PAIR, 2 * PAIR), 1) % PAIR
    strict = (r2 < c2) if rev else (r2 > c2)
    incl = (r2 <= c2) if rev else (r2 >= c2)
    eye = (lax.broadcasted_iota(jnp.int32, (PAIR, PAIR), 0)
           == lax.broadcasted_iota(jnp.int32, (PAIR, PAIR), 1))
    zeros = jnp.zeros((PAIR, PAIR), F32)

    pairs = range(N_PAIRS)
    cat = jnp.concatenate

    def stack2(x):
        out = []
        for p in pairs:
            xs = x[:, p * PAIR:(p + 1) * PAIR]
            out.append(cat([jnp.where(m0, xs, 0.0), jnp.where(m0, 0.0, xs)], axis=0))
        return out

    a2, rr2, b2, kt2, v2 = stack2(at), stack2(rt), stack2(bt), stack2(kt), stack2(v)
    bh2, kh2 = stack2(bh), stack2(kh)
    pm = [_mm_nt(cat([a2[p], rr2[p]], axis=0), cat([b2[p], kt2[p]], axis=0)) for p in pairs]
    a_abk = [jnp.where(strict, pm[p][:PAIR, :], 0.0) for p in pairs]
    a_rbk = [jnp.where(incl, pm[p][PAIR:, :], 0.0) for p in pairs]
    tinv = _neumann_inverse([a_abk[p][:, :PAIR] for p in pairs])
    x1 = [_mm(a_abk[p][:, PAIR:], v2[p]) for p in pairs]
    z = [_mm(tinv[p], cat([a2[p], x1[p]], axis=1)) for p in pairs]
    w2 = [cat([z[p], cat([zeros, v2[p]], axis=1)], axis=0) for p in pairs]
    mg = [_mm_tn(cat([bh2[p], kh2[p]], axis=0), w2[p]) for p in pairs]
    ry = [_mm(a_rbk[p], w2[p]) for p in pairs]
    lhs = [cat([rr2[p] + ry[p][:, :PAIR],
                mg[p][:, :PAIR] + jnp.where(eye, w_tot[:, p * PAIR:(p + 1) * PAIR], 0.0)], axis=0)
           for p in pairs]
    yh = [_mm(lhs[p], h_ref[p]) for p in pairs]
    h_ref[...] = jnp.stack([yh[p][PAIR:] + mg[p][:, PAIR:] for p in pairs], axis=0)
    y2 = [yh[p][:PAIR] + ry[p][:, PAIR:] for p in pairs]
    y = cat([y2[p][:C] + y2[p][C:] for p in pairs], axis=1)

    if final:
        ones_bd = _pair_ones()
        y = y + yf_ref[sl, :]
        mean = _head_sum(y, ones_bd) * (1.0 / RWKV_DIM)
        yc = y - mean
        var = _head_sum(yc * yc, ones_bd) * (1.0 / RWKV_DIM)
        yn = yc * lax.rsqrt(var + GN_EPS) * lnw_ref[...] + lnb_ref[...]
        o_ref[sl, :] = ((yn + bonus_ref[sl, :]) * g_ref[sl, :]).astype(o_ref.dtype)
    else:
        o_ref[sl, :] = y


def _rwkv_dir(r, k2, v, kk, bv, lw, *, rev, tb, fin=None):
    B, T, W = r.shape
    nc = T // tb

    def tmap(n):
        return (nc - 1 - n) if rev else n

    blk = pl.BlockSpec((None, tb, W), lambda b, n: (b, tmap(n), 0))
    vec = pl.BlockSpec((1, W), lambda b, n: (0, 0))
    in_specs = [blk] * 6
    args = [r, k2, v, kk, bv, lw]
    if fin is not None:
        in_specs += [blk, blk, blk, vec, vec]
        args += list(fin)
    return pl.pallas_call(
        functools.partial(_rwkv_scan_body, rev=rev, final=fin is not None, n_chunks=tb // CHUNK),
        out_shape=jax.ShapeDtypeStruct((B, T, W), BF16 if fin is not None else F32),
        grid=(B, nc),
        in_specs=in_specs,
        out_specs=blk,
        scratch_shapes=[pltpu.VMEM((N_PAIRS, PAIR, PAIR), F32)],
        compiler_params=_cparams(("parallel", "arbitrary")),
        name="rwkv_bwd" if rev else "rwkv_fwd",
    )(*args)


def _attn_body(sink_ref, q_ref, kp_ref, kc_ref, kn_ref, vp_ref, vc_ref, vn_ref, o_ref):
    n = pl.program_id(1)
    nb = pl.num_programs(1)
    r = lax.broadcasted_iota(jnp.int32, (BLOCK, 3 * BLOCK), 0)
    c = lax.broadcasted_iota(jnp.int32, (BLOCK, 3 * BLOCK), 1)
    d = c - BLOCK - r
    lo = jnp.where(n > 0, 0, BLOCK)
    hi = jnp.where(n < nb - 1, 3 * BLOCK, 2 * BLOCK)
    valid = (d >= -BLOCK) & (d <= BLOCK) & (c >= lo) & (c < hi)
    scale = ATT_DIM ** -0.5
    for kh in range(ATT_KV_HEADS):
        ks = slice(kh * ATT_DIM, (kh + 1) * ATT_DIM)
        kw = jnp.concatenate([kp_ref[:, ks], kc_ref[:, ks], kn_ref[:, ks]], axis=0)
        vw = jnp.concatenate([vp_ref[:, ks], vc_ref[:, ks], vn_ref[:, ks]], axis=0)
        heads = [kh * ATT_GROUP + g for g in range(ATT_GROUP)]
        qg = jnp.concatenate([q_ref[:, h * ATT_DIM:(h + 1) * ATT_DIM] for h in heads], axis=0)
        s_all = lax.dot_general(qg, kw, (((1,), (1,)), ((), ())), preferred_element_type=F32) * scale
        ps = []
        for g, h in enumerate(heads):
            s = jnp.where(valid, s_all[g * BLOCK:(g + 1) * BLOCK, :], -jnp.inf)
            sk = sink_ref[h]
            m = jnp.maximum(jnp.max(s, axis=-1, keepdims=True), sk)
            e = jnp.exp(s - m)
            den = jnp.sum(e, axis=-1, keepdims=True) + jnp.exp(sk - m)
            ps.append((e / den).astype(BF16))
        o_all = jnp.dot(jnp.concatenate(ps, axis=0), vw, preferred_element_type=F32)
        for g, h in enumerate(heads):
            o_ref[:, h * ATT_DIM:(h + 1) * ATT_DIM] = o_all[g * BLOCK:(g + 1) * BLOCK, :].astype(o_ref.dtype)


def _attention(qkv, sink):
    B, T, _ = qkv.shape
    nb = T // BLOCK
    kvw = ATT_KV_HEADS * ATT_DIM
    kc = (ATT_HEADS * ATT_DIM) // kvw
    vc = kc + 1

    def blk(cidx, off):
        def imap(b, n):
            return (b, jnp.clip(n + off, 0, nb - 1), cidx)
        return pl.BlockSpec((None, BLOCK, kvw), imap)

    return pl.pallas_call(
        _attn_body,
        out_shape=jax.ShapeDtypeStruct((B, T, ATT_HEADS * ATT_DIM), BF16),
        grid=(B, nb),
        in_specs=[pl.BlockSpec(memory_space=pltpu.SMEM),
                  pl.BlockSpec((None, BLOCK, ATT_HEADS * ATT_DIM), lambda b, n: (b, n, 0)),
                  blk(kc, -1), blk(kc, 0), blk(kc, 1), blk(vc, -1), blk(vc, 0), blk(vc, 1)],
        out_specs=pl.BlockSpec((None, BLOCK, ATT_HEADS * ATT_DIM), lambda b, n: (b, n, 0)),
        compiler_params=_cparams(("parallel", "parallel")),
        name="attention",
    )(sink, qkv, qkv, qkv, qkv, qkv, qkv, qkv)


def _ffn_body(x_ref, xp_ref, xn_ref, g_ref, wg_ref, wv_ref, cw_ref, cb_ref, wd_ref, fg_ref, o_ref, h_ref,
              *, tm, tiles_per_seq, final_norm):
    i = pl.program_id(0)
    f = pl.program_id(1)
    nf = pl.num_programs(1)

    @pl.when(f == 0)
    def _():
        def norm(x):
            ms = jnp.mean(x * x, axis=-1, keepdims=True)
            return (x * lax.rsqrt(ms + RMS_EPS) * g_ref[...]).astype(BF16)
        x = x_ref[...]
        h_ref[0:tm, :] = norm(x)
        h_ref[tm:tm + 16, :] = norm(jnp.concatenate([xp_ref[...], xn_ref[...]], axis=0))
        o_ref[...] = x

    t = i % tiles_per_seq
    has_prev = jnp.where(t > 0, 1.0, 0.0).astype(F32)
    has_next = jnp.where(t < tiles_per_seq - 1, 1.0, 0.0).astype(F32)
    ge = jnp.dot(h_ref[...], wg_ref[...], preferred_element_type=F32)
    gm = ge[0:tm, :]
    rows = lax.broadcasted_iota(jnp.int32, (tm, 1), 0)
    g_prev = jnp.where(rows == 0, ge[tm + 7:tm + 8, :] * has_prev, pltpu.roll(gm, 1, axis=0))
    g_next = jnp.where(rows == tm - 1, ge[tm + 8:tm + 9, :] * has_next, pltpu.roll(gm, tm - 1, axis=0))
    gate = g_prev * cw_ref[0:1, :] + gm * cw_ref[1:2, :] + g_next * cw_ref[2:3, :] + cb_ref[...]
    val = jnp.dot(h_ref[0:tm, :], wv_ref[...], preferred_element_type=F32)
    act = (gate * _sigmoid(gate) * val).astype(BF16)
    o_ref[...] += jnp.dot(act, wd_ref[...], preferred_element_type=F32)

    if final_norm:
        @pl.when(f == nf - 1)
        def _():
            y = o_ref[...]
            ms = jnp.mean(y * y, axis=-1, keepdims=True)
            o_ref[...] = y * lax.rsqrt(ms + RMS_EPS) * fg_ref[...]


def _ffn(x, g, w_up, conv_w, conv_b, w_down, final_g, *, layer, seq_len, tm, tf, final_norm):
    M, D = x.shape
    nf = D_FF // tf
    hb = tm // 8
    last8 = M // 8 - 1
    in_specs = [
        pl.BlockSpec((tm, D), lambda i, f: (i, 0)),
        pl.BlockSpec((8, D), lambda i, f: (jnp.maximum(i * hb - 1, 0), 0)),
        pl.BlockSpec((8, D), lambda i, f: (jnp.minimum((i + 1) * hb, last8), 0)),
        pl.BlockSpec((1, D), lambda i, f: (0, 0)),
        pl.BlockSpec((None, D, tf), lambda i, f: (layer, 0, f)),
        pl.BlockSpec((None, D, tf), lambda i, f: (layer, 0, nf + f)),
        pl.BlockSpec((3, tf), lambda i, f: (0, f)),
        pl.BlockSpec((1, tf), lambda i, f: (0, f)),
        pl.BlockSpec((None, tf, D), lambda i, f: (layer, f, 0)),
        pl.BlockSpec((1, D), lambda i, f: (0, 0)),
    ]
    return pl.pallas_call(
        functools.partial(_ffn_body, tm=tm, tiles_per_seq=seq_len // tm, final_norm=final_norm),
        out_shape=jax.ShapeDtypeStruct((M, D), F32),
        grid=(M // tm, nf),
        in_specs=in_specs,
        out_specs=pl.BlockSpec((tm, D), lambda i, f: (i, 0)),
        scratch_shapes=[pltpu.VMEM((tm + 16, D), BF16)],
        compiler_params=_cparams(("parallel", "arbitrary")),
        name="conv_ffn",
    )(x, x, x, g, w_up, w_up, conv_w, conv_b, w_down, final_g)


def _pick(n, prefs):
    for p in prefs:
        if n % p == 0:
            return p
    raise ValueError(f"no tile for {n}")


def _rope_tables(T):
    half = ATT_DIM // 2
    inv = ROPE_THETA ** (-jnp.arange(half, dtype=F32) / half)
    ang = jnp.arange(T, dtype=F32)[:, None] * inv[None, :]
    cos = jnp.cos(ang)
    sin = jnp.sin(ang)
    return jnp.concatenate([cos, cos], axis=1), jnp.concatenate([-sin, sin], axis=1)


def _prepare_params(p):
    q = dict(p)
    w_in = p['ab_w_in'][0]
    q['w_in'] = jnp.pad(w_in, ((0, 0), (0, MIX_COLS_PAD - MIX_COLS))).astype(BF16)
    mu = p['rwkv_mu'][0]
    q['mu'] = jnp.pad(mu, (0, LORA_PAD - LORA_COLS))[None, :]
    q['lb'] = jnp.cumsum(jax.nn.softmax(p['hgrn_lb'].astype(F32), axis=0), axis=0)
    q['w_out'] = p['ab_w_out'][0].astype(BF16)
    q['w_qkv'] = p['att_w_qkv'][0].astype(BF16)
    q['w_o'] = p['att_w_o'][0].astype(BF16)
    q['w_up'] = p['ffn_w_up'].astype(BF16)
    q['w_down'] = p['ffn_w_down'].astype(BF16)
    return q


def _mixer_layer(x2, q, B, T, layer):
    M = B * T
    tm = _pick(M, (512, 256, 128))
    proj = _norm_matmul(x2, q['mix_norm'][layer][None, :], q['w_in'], tm=_pick(M, (1024, 512, 256, 128)),
                        tn=512, out_dtype=F32)
    proj = proj.reshape(B, T, MIX_COLS_PAD)
    lb = q['lb'][layer][None, :]
    tb = _pick(T, (256, 128, 64))
    o_fwd = _hgrn_dir(proj, lb, rev=False, tb=tb)
    ya = _hgrn_dir(proj, lb, rev=True, tb=tb, ofwd=o_fwd, onorm=q['hgrn_onorm'][0][None, :])

    r, k2, v, kk, bv, lwf, lwb, g, bonus = _rwkv_prep(
        proj, q['mu'], q['rwkv_w0'][0], q['rwkv_w2'][0], q['rwkv_a0'][0][None, :], q['rwkv_a2'][0],
        q['rwkv_g2'][0], q['rwkv_kk'][0][None, :], q['rwkv_ka'][0][None, :],
        q['rwkv_rk'][0].reshape(1, RWKV_WIDTH), tb=tb)
    y_fwd = _rwkv_dir(r, k2, v, kk, bv, lwf, rev=False, tb=tb)
    yb = _rwkv_dir(r, k2, v, kk, bv, lwb, rev=True, tb=tb,
                   fin=(y_fwd, g, bonus, q['rwkv_ln_w'][0][None, :], q['rwkv_ln_b'][0][None, :]))
    return _matmul_res([ya.reshape(M, HGRN_WIDTH), yb.reshape(M, RWKV_WIDTH)], q['w_out'], x2,
                       tm=tm, tn=D_MODEL)


def _attention_layer(x2, q, B, T, layer, rope_tabs):
    M = B * T
    tm = _pick(M, (512, 256, 128))
    cos, sin = rope_tabs
    qkv = _norm_matmul(x2, q['mix_norm'][layer][None, :], q['w_qkv'], tm=tm, tn=QKV_COLS, out_dtype=BF16,
                       rope=(cos, sin, ATT_HEADS + ATT_KV_HEADS))
    o = _attention(qkv.reshape(B, T, QKV_COLS), q['att_sink'][0])
    return _matmul_res([o.reshape(M, ATT_HEADS * ATT_DIM)], q['w_o'], x2, tm=tm, tn=D_MODEL)


def _trunk(x, q):
    B, T, D = x.shape
    M = B * T
    x2 = x.reshape(M, D)
    rope_tabs = _rope_tables(T)
    tm = _pick(T, (1024, 512, 256, 128))
    for layer in range(DEPTH):
        if layer % 2 == 0:
            x2 = _mixer_layer(x2, q, B, T, layer)
        else:
            x2 = _attention_layer(x2, q, B, T, layer, rope_tabs)
        x2 = _ffn(x2, q['ffn_norm'][layer][None, :], q['w_up'], q['ffn_conv_w'][layer],
                  q['ffn_conv_b'][layer][None, :], q['w_down'], q['final_norm'][None, :],
                  layer=layer, seq_len=T, tm=tm, tf=512, final_norm=(layer == DEPTH - 1))
    return x2.reshape(B, T, D)


def kernel(x_prompt, x_sample, mix_norm, ab_w_in, hgrn_lb, hgrn_onorm, rwkv_mu, rwkv_w0, rwkv_w2, rwkv_a0, rwkv_a2, rwkv_g2, rwkv_kk, rwkv_ka, rwkv_rk, rwkv_ln_w, rwkv_ln_b, ab_w_out, att_w_qkv, att_sink, att_w_o, ffn_norm, ffn_w_up, ffn_conv_w, ffn_conv_b, ffn_w_down, final_norm):
    p = {
        'mix_norm': mix_norm, 'ab_w_in': ab_w_in, 'hgrn_lb': hgrn_lb, 'hgrn_onorm': hgrn_onorm,
        'rwkv_mu': rwkv_mu, 'rwkv_w0': rwkv_w0, 'rwkv_w2': rwkv_w2, 'rwkv_a0': rwkv_a0, 'rwkv_a2': rwkv_a2,
        'rwkv_g2': rwkv_g2, 'rwkv_kk': rwkv_kk, 'rwkv_ka': rwkv_ka, 'rwkv_rk': rwkv_rk,
        'rwkv_ln_w': rwkv_ln_w, 'rwkv_ln_b': rwkv_ln_b, 'ab_w_out': ab_w_out,
        'att_w_qkv': att_w_qkv, 'att_sink': att_sink, 'att_w_o': att_w_o,
        'ffn_norm': ffn_norm, 'ffn_w_up': ffn_w_up, 'ffn_conv_w': ffn_conv_w, 'ffn_conv_b': ffn_conv_b,
        'ffn_w_down': ffn_w_down, 'final_norm': final_norm,
    }
    q = _prepare_params(p)
    return (_trunk(x_prompt, q), _trunk(x_sample, q))
```

```python
import functools

import jax
import jax.numpy as jnp
from jax import lax
from jax.experimental import pallas as pl
from jax.experimental.pallas import tpu as pltpu

F32 = jnp.float32
BF16 = jnp.bfloat16

D_MODEL = 2048
DEPTH = 2
HGRN_DIM = 128
HGRN_HEADS = 8
HGRN_WIDTH = 1024
RWKV_DIM = 64
RWKV_HEADS = 16
RWKV_WIDTH = 1024
DECAY_LORA = 64
AAA_LORA = 64
GATE_LORA = 160
LORA_COLS = 2 * DECAY_LORA + AAA_LORA + GATE_LORA
LORA_PAD = 512
LORA_SLAB = 128
MIX_A_COLS = 5 * HGRN_WIDTH
MIX_COLS = MIX_A_COLS + 3 * RWKV_WIDTH + LORA_COLS
MIX_COLS_PAD = MIX_A_COLS + 3 * RWKV_WIDTH + LORA_PAD
ATT_DIM = 128
ATT_HEADS = 16
ATT_KV_HEADS = 4
ATT_GROUP = 4
QKV_COLS = (ATT_HEADS + 2 * ATT_KV_HEADS) * ATT_DIM
BLOCK = 128
ROPE_THETA = 10000.0
D_FF = 5632
RMS_EPS = 1e-6
GN_EPS = 64e-5
CHUNK = 64
PAIR = 2 * RWKV_DIM
N_PAIRS = RWKV_WIDTH // PAIR
RWKV_GROUP = 4

VMEM_LIMIT = 56 * 1024 * 1024
HI = lax.Precision.HIGHEST


def _cparams(sem):
    return pltpu.CompilerParams(dimension_semantics=sem, vmem_limit_bytes=VMEM_LIMIT)


def _mm(a, b):
    return jnp.dot(a.astype(BF16), b.astype(BF16), preferred_element_type=F32)


def _mm_nt(a, b):
    return lax.dot_general(a.astype(BF16), b.astype(BF16), (((1,), (1,)), ((), ())),
                           preferred_element_type=F32)


def _mm_tn(a, b):
    return lax.dot_general(a.astype(BF16), b.astype(BF16), (((0,), (0,)), ((), ())),
                           preferred_element_type=F32)


def _mm_hi(a, b):
    return jnp.dot(a, b, preferred_element_type=F32, precision=HI)


def _split3(x):
    hi = x.astype(BF16)
    r1 = x - hi.astype(F32)
    mid = r1.astype(BF16)
    lo = (r1 - mid.astype(F32)).astype(BF16)
    return hi, mid, lo


def _mm_cumsum(tri, x):
    t = tri.astype(BF16)
    hi, mid, lo = _split3(x)
    return (jnp.dot(t, hi, preferred_element_type=F32) + jnp.dot(t, mid, preferred_element_type=F32)
            + jnp.dot(t, lo, preferred_element_type=F32))


_mm_lora = _mm
_mm_neumann = _mm


def _sigmoid(x):
    return 1.0 / (1.0 + jnp.exp(-x))


def _norm_matmul_body(x_ref, g_ref, w_ref, o_ref, h_ref):
    @pl.when(pl.program_id(1) == 0)
    def _():
        x = x_ref[...]
        ms = jnp.mean(x * x, axis=-1, keepdims=True)
        h_ref[...] = (x * lax.rsqrt(ms + RMS_EPS) * g_ref[...]).astype(BF16)

    o_ref[...] = jnp.dot(h_ref[...], w_ref[...], preferred_element_type=F32).astype(o_ref.dtype)


def _norm_matmul_rope_body(x_ref, g_ref, w_ref, cos_ref, sin_ref, o_ref, h_ref, *, n_rope, tn):
    x = x_ref[...]
    ms = jnp.mean(x * x, axis=-1, keepdims=True)
    h_ref[...] = (x * lax.rsqrt(ms + RMS_EPS) * g_ref[...]).astype(BF16)
    cos = cos_ref[...]
    sin = sin_ref[...]
    group = 4 * ATT_DIM
    for gi in range(tn // group):
        acc = jnp.dot(h_ref[...], w_ref[:, gi * group:(gi + 1) * group], preferred_element_type=F32)
        for hh in range(group // ATT_DIM):
            a = acc[:, hh * ATT_DIM:(hh + 1) * ATT_DIM]
            head = gi * (group // ATT_DIM) + hh
            if head < n_rope:
                a = a * cos + pltpu.roll(a, ATT_DIM // 2, axis=1) * sin
            o_ref[:, head * ATT_DIM:(head + 1) * ATT_DIM] = a.astype(o_ref.dtype)


def _norm_matmul(x, g, w, *, tm, tn, out_dtype, rope=None):
    M, K = x.shape
    N = w.shape[1]
    grid = (M // tm, N // tn)
    in_specs = [pl.BlockSpec((tm, K), lambda i, j: (i, 0)),
                pl.BlockSpec((1, K), lambda i, j: (0, 0)),
                pl.BlockSpec((K, tn), lambda i, j: (0, j))]
    args = [x, g, w]
    if rope is None:
        body = _norm_matmul_body
    else:
        cos, sin, n_rope = rope
        tiles_per_seq = cos.shape[0] // tm
        in_specs += [pl.BlockSpec((tm, ATT_DIM), lambda i, j: (i % tiles_per_seq, 0)),
                     pl.BlockSpec((tm, ATT_DIM), lambda i, j: (i % tiles_per_seq, 0))]
        args += [cos, sin]
        body = functools.partial(_norm_matmul_rope_body, n_rope=n_rope, tn=tn)
    return pl.pallas_call(
        body,
        out_shape=jax.ShapeDtypeStruct((M, N), out_dtype),
        grid=grid,
        in_specs=in_specs,
        out_specs=pl.BlockSpec((tm, tn), lambda i, j: (i, j)),
        scratch_shapes=[pltpu.VMEM((tm, K), BF16)],
        compiler_params=_cparams(("parallel", "arbitrary")),
        name="norm_matmul" if rope is None else "norm_matmul_rope",
    )(*args)


def _matmul_res_body(*refs, n_parts):
    a_refs = refs[:n_parts]
    w_ref, r_ref, o_ref = refs[n_parts:]
    acc = r_ref[...]
    k0 = 0
    for a_ref in a_refs:
        kw = a_ref.shape[1]
        acc = acc + jnp.dot(a_ref[...], w_ref[k0:k0 + kw, :], preferred_element_type=F32)
        k0 += kw
    o_ref[...] = acc


def _matmul_res(a_parts, w, res, *, tm, tn):
    M, N = res.shape
    K = w.shape[0]
    in_specs = [pl.BlockSpec((tm, a.shape[1]), lambda i, j: (i, 0)) for a in a_parts]
    in_specs += [pl.BlockSpec((K, tn), lambda i, j: (0, j)),
                 pl.BlockSpec((tm, tn), lambda i, j: (i, j))]
    return pl.pallas_call(
        functools.partial(_matmul_res_body, n_parts=len(a_parts)),
        out_shape=jax.ShapeDtypeStruct((M, N), F32),
        grid=(M // tm, N // tn),
        in_specs=in_specs,
        out_specs=pl.BlockSpec((tm, tn), lambda i, j: (i, j)),
        compiler_params=_cparams(("parallel", "arbitrary")),
        name="matmul_res",
    )(*a_parts, w, res)


def _hgrn_body(*refs, rev, n_chunks, final):
    if final:
        q_ref, i_ref, z_ref, lb_ref, ofwd_ref, g_ref, onorm_ref, o_ref, st_ref = refs
    else:
        q_ref, i_ref, z_ref, lb_ref, o_ref, st_ref = refs

    @pl.when(pl.program_id(1) == 0)
    def _():
        st_ref[...] = jnp.zeros_like(st_ref)

    lb = lb_ref[...]
    row = lax.broadcasted_iota(jnp.int32, (CHUNK, CHUNK), 0)
    col = lax.broadcasted_iota(jnp.int32, (CHUNK, CHUNK), 1)
    keep = (row <= col) if rev else (row >= col)
    tri = jnp.where(keep, 1.0, 0.0).astype(F32)
    last = 0 if rev else CHUNK - 1

    head_slices = [slice(h * HGRN_DIM, (h + 1) * HGRN_DIM) for h in range(HGRN_HEADS)]
    sts = [st_ref[h] for h in range(HGRN_HEADS)]
    order = range(n_chunks - 1, -1, -1) if rev else range(n_chunks)
    for ci in order:
        sl = slice(ci * CHUNK, (ci + 1) * CHUNK)
        f = lb + (1.0 - lb) * _sigmoid(z_ref[sl, :])
        kk = 1.0 - f
        b = _mm_cumsum(tri, jnp.log(f))
        b_last = b[last:last + 1, :]
        q_d = q_ref[sl, :] * jnp.exp(b)
        k_d = kk * jnp.exp(-b)
        k_u = kk * jnp.exp(b_last - b)
        dec = jnp.exp(b_last)
        v = i_ref[sl, :]
        att = [jnp.where(keep, _mm_nt(q_d[:, hs], k_d[:, hs]), 0.0) for hs in head_slices]
        outs = [_mm(att[h], v[:, hs]) + _mm_nt(q_d[:, hs], sts[h]) for h, hs in enumerate(head_slices)]
        upd = [_mm_tn(v[:, hs], k_u[:, hs]) for hs in head_slices]
        sts = [sts[h] * dec[:, hs] + upd[h] for h, hs in enumerate(head_slices)]
        if final:
            g = g_ref[sl, :]
            silu_g = g * _sigmoid(g)
            normed = []
            for h, hs in enumerate(head_slices):
                oa = outs[h] + ofwd_ref[sl, hs]
                normed.append(oa * lax.rsqrt(jnp.mean(oa * oa, axis=-1, keepdims=True) + RMS_EPS))
            o_ref[sl, :] = (jnp.concatenate(normed, axis=1) * onorm_ref[...] * silu_g).astype(o_ref.dtype)
        else:
            o_ref[sl, :] = jnp.concatenate(outs, axis=1)
    st_ref[...] = jnp.stack(sts, axis=0)


def _hgrn_dir(proj, lb, *, rev, tb, ofwd=None, onorm=None):
    B, T, _ = proj.shape
    nt = T // tb
    final = ofwd is not None

    def tmap(n):
        return (nt - 1 - n) if rev else n

    def col(c):
        return pl.BlockSpec((None, tb, HGRN_WIDTH), lambda b, n: (b, tmap(n), c))

    vec = pl.BlockSpec((1, HGRN_WIDTH), lambda b, n: (0, 0))
    in_specs = [col(0), col(1), col(3 if rev else 2), vec]
    args = [proj, proj, proj, lb]
    if final:
        in_specs += [col(0), col(4), vec]
        args += [ofwd, proj, onorm]
    return pl.pallas_call(
        functools.partial(_hgrn_body, rev=rev, n_chunks=tb // CHUNK, final=final),
        out_shape=jax.ShapeDtypeStruct((B, T, HGRN_WIDTH), BF16 if final else F32),
        grid=(B, nt),
        in_specs=in_specs,
        out_specs=pl.BlockSpec((None, tb, HGRN_WIDTH), lambda b, n: (b, tmap(n), 0)),
        scratch_shapes=[pltpu.VMEM((HGRN_HEADS, HGRN_DIM, HGRN_DIM), F32)],
        compiler_params=_cparams(("parallel", "arbitrary")),
        name="hgrn_bwd" if rev else "hgrn_fwd",
    )(*args)


def _pair_ones():
    r = lax.broadcasted_iota(jnp.int32, (PAIR, PAIR), 0) // RWKV_DIM
    c = lax.broadcasted_iota(jnp.int32, (PAIR, PAIR), 1) // RWKV_DIM
    return jnp.where(r == c, 1.0, 0.0).astype(BF16)


def _head_sum(x, ones_bd):
    outs = []
    for p in range(N_PAIRS):
        hi, mid, lo = _split3(x[:, p * PAIR:(p + 1) * PAIR])
        outs.append(jnp.dot(hi, ones_bd, preferred_element_type=F32)
                    + jnp.dot(mid, ones_bd, preferred_element_type=F32)
                    + jnp.dot(lo, ones_bd, preferred_element_type=F32))
    return jnp.concatenate(outs, axis=1)


def _rwkv_prep_body(r_ref, k_ref, v_ref, l_ref,
                    rp_ref, kp_ref, vp_ref, lp_ref, rn_ref, kn_ref, vn_ref, ln_ref,
                    mu_r_ref, mu_k_ref, mu_v_ref, mu_l_ref,
                    w0_ref, w2f_ref, w2b_ref, a0_ref, a2_ref, g2_ref, kkw_ref, kaw_ref, rk_ref,
                    ro_ref, ko_ref, vo_ref, kko_ref, bvo_ref, lwf_ref, lwb_ref, go_ref, bo_ref,
                    *, tb):
    n = pl.program_id(1)
    nt = pl.num_programs(1)
    has_prev = jnp.where(n > 0, 1.0, 0.0).astype(F32)
    has_next = jnp.where(n < nt - 1, 1.0, 0.0).astype(F32)
    rows = lax.broadcasted_iota(jnp.int32, (tb, 1), 0)
    first = rows == 0
    lastr = rows == tb - 1

    def shift(x_ref, p_ref, n_ref, mu_ref):
        x = x_ref[...]
        prev = jnp.where(first, p_ref[7:8, :] * has_prev, pltpu.roll(x, 1, axis=0))
        nxt = jnp.where(lastr, n_ref[0:1, :] * has_next, pltpu.roll(x, tb - 1, axis=0))
        return x + mu_ref[...] * (0.5 * (prev + nxt) - x)

    r = shift(r_ref, rp_ref, rn_ref, mu_r_ref)
    k = shift(k_ref, kp_ref, kn_ref, mu_k_ref)
    v = shift(v_ref, vp_ref, vn_ref, mu_v_ref)
    lo = shift(l_ref, lp_ref, ln_ref, mu_l_ref)
    wd = jnp.tanh(lo[:, 0:LORA_SLAB])
    ag = lo[:, LORA_SLAB:3 * LORA_SLAB]

    def log_decay(w0, w2):
        u = w0 + _mm_lora(wd, w2)
        softplus = jnp.maximum(-u, 0.0) + jnp.log(1.0 + jnp.exp(-jnp.abs(u)))
        return -jnp.exp(-softplus - 0.5)

    lwf_ref[...] = log_decay(w0_ref[0:1, :], w2f_ref[...])
    lwb_ref[...] = log_decay(w0_ref[1:2, :], w2b_ref[...])
    a = _sigmoid(a0_ref[...] + _mm_lora(ag[:, 0:LORA_SLAB], a2_ref[...]))
    go_ref[...] = _mm_lora(_sigmoid(ag), g2_ref[...])

    ones_bd = _pair_ones()
    kk = k * kkw_ref[...]
    norm = jnp.maximum(jnp.sqrt(_head_sum(kk * kk, ones_bd)), 1e-12)
    kk = kk / norm
    k2 = k * (1.0 + (a - 1.0) * kaw_ref[...])
    ro_ref[...] = r
    ko_ref[...] = k2
    vo_ref[...] = v
    kko_ref[...] = kk
    bvo_ref[...] = kk * a
    bo_ref[...] = _head_sum(r * k2 * rk_ref[...], ones_bd) * v


def _rwkv_prep(proj, mu, w0, w2, a0, a2, g2, kkw, kaw, rk, *, tb):
    B, T, _ = proj.shape
    nt = T // tb
    hb = tb // 8
    last8 = T // 8 - 1

    def main(width, c):
        return pl.BlockSpec((None, tb, width), lambda b, n: (b, n, c))

    def prev(width, c):
        return pl.BlockSpec((None, 8, width), lambda b, n: (b, jnp.maximum(n * hb - 1, 0), c))

    def nxt(width, c):
        return pl.BlockSpec((None, 8, width), lambda b, n: (b, jnp.minimum((n + 1) * hb, last8), c))

    def full(a):
        return pl.BlockSpec(a.shape, lambda b, n: (0,) * a.ndim)

    W = RWKV_WIDTH
    c_l = (MIX_A_COLS + 3 * W) // LORA_PAD
    zpad = lambda a, before, total: jnp.pad(a, ((before, total - before - a.shape[0]), (0, 0)))
    params = [mu[:, 0:W], mu[:, W:2 * W], mu[:, 2 * W:3 * W], mu[:, 3 * W:],
              w0, zpad(w2[0], 0, LORA_SLAB), zpad(w2[1], DECAY_LORA, LORA_SLAB), a0,
              zpad(a2, 0, LORA_SLAB), zpad(g2, AAA_LORA, 2 * LORA_SLAB), kkw, kaw, rk]
    in_specs = ([main(W, 5), main(W, 6), main(W, 7), main(LORA_PAD, c_l),
                 prev(W, 5), prev(W, 6), prev(W, 7), prev(LORA_PAD, c_l),
                 nxt(W, 5), nxt(W, 6), nxt(W, 7), nxt(LORA_PAD, c_l)]
                + [full(p) for p in params])
    out = jax.ShapeDtypeStruct((B, T, W), F32)
    return pl.pallas_call(
        functools.partial(_rwkv_prep_body, tb=tb),
        out_shape=[out] * 9,
        grid=(B, nt),
        in_specs=in_specs,
        out_specs=[pl.BlockSpec((None, tb, W), lambda b, n: (b, n, 0))] * 9,
        compiler_params=_cparams(("parallel", "parallel")),
        name="rwkv_prep",
    )(*([proj] * 12), *params)


def _neumann_inverse(mats):
    n = mats[0].shape[0]
    eye = jnp.where(lax.broadcasted_iota(jnp.int32, (n, n), 0) == lax.broadcasted_iota(jnp.int32, (n, n), 1),
                    1.0, 0.0).astype(F32)
    ps = [eye + a for a in mats]
    aks = [_mm_neumann(a, a) for a in mats]
    for _ in range(4):
        ss = [_mm_neumann(ak, jnp.concatenate([ak, p], axis=1)) for p, ak in zip(ps, aks)]
        ps = [p + s[:, n:] for p, s in zip(ps, ss)]
        aks = [s[:, :n] for s in ss]
    return [p + _mm_neumann(ak, p) for p, ak in zip(ps, aks)]


def _rwkv_scan_body(*refs, rev, final, n_chunks, group):
    h_ref = refs[-1]

    @pl.when(pl.program_id(1) == 0)
    def _():
        h_ref[...] = jnp.zeros_like(h_ref)

    n_groups = n_chunks // group
    if n_groups == 1:
        _rwkv_group(refs, 0, rev=rev, final=final, group=group)
    else:
        def body(i, carry):
            gi = (n_groups - 1 - i) if rev else i
            _rwkv_group(refs, gi * (group * CHUNK), rev=rev, final=final, group=group)
            return carry

        lax.fori_loop(0, n_groups, body, 0)


def _rwkv_group(refs, base, *, rev, final, group):
    if final:
        (r_ref, k_ref, v_ref, kk_ref, bv_ref, lw_ref, yf_ref, g_ref, bonus_ref, lnw_ref, lnb_ref,
         o_ref, h_ref) = refs
    else:
        r_ref, k_ref, v_ref, kk_ref, bv_ref, lw_ref, o_ref, h_ref = refs

    C = CHUNK
    row = lax.broadcasted_iota(jnp.int32, (C, C), 0)
    col = lax.broadcasted_iota(jnp.int32, (C, C), 1)
    tri = jnp.where((row <= col) if rev else (row >= col), 1.0, 0.0).astype(F32)
    last = 0 if rev else C - 1
    lane = lax.broadcasted_iota(jnp.int32, (C, PAIR), 1)
    m0 = lane < RWKV_DIM
    r2 = lax.broadcasted_iota(jnp.int32, (PAIR, 2 * PAIR), 0)
    c2 = lax.broadcasted_iota(jnp.int32, (PAIR, 2 * PAIR), 1) % PAIR
    strict = (r2 < c2) if rev else (r2 > c2)
    incl = (r2 <= c2) if rev else (r2 >= c2)
    eye = (lax.broadcasted_iota(jnp.int32, (PAIR, PAIR), 0)
           == lax.broadcasted_iota(jnp.int32, (PAIR, PAIR), 1))
    zeros = jnp.zeros((PAIR, PAIR), F32)
    pairs = range(N_PAIRS)
    cat = jnp.concatenate

    def rows(c):
        if isinstance(base, int):
            return slice(base + c * C, base + (c + 1) * C)
        return pl.ds(pl.multiple_of(base + c * C, C), C)

    def stack2(x):
        out = []
        for p in pairs:
            xs = x[:, p * PAIR:(p + 1) * PAIR]
            out.append(cat([jnp.where(m0, xs, 0.0), jnp.where(m0, 0.0, xs)], axis=0))
        return out

    a2, rr2, b2, kt2, v2, bh2, kh2, w_tot = [], [], [], [], [], [], [], []
    for c in range(group):
        sl = rows(c)
        lw = lw_ref[sl, :]
        c_inc = _mm_cumsum(tri, lw)
        c_exc = c_inc - lw
        c_tot = c_inc[last:last + 1, :]
        e_ninc = jnp.exp(-c_inc)
        e_hat = jnp.exp(c_tot - c_inc)
        kk = kk_ref[sl, :]
        bv = bv_ref[sl, :]
        k2 = k_ref[sl, :]
        w_tot += [jnp.exp(c_tot)] * N_PAIRS
        a2 += stack2(-kk * jnp.exp(c_exc))
        rr2 += stack2(r_ref[sl, :] * jnp.exp(c_inc))
        b2 += stack2(bv * e_ninc)
        kt2 += stack2(k2 * e_ninc)
        v2 += stack2(v_ref[sl, :])
        bh2 += stack2(bv * e_hat)
        kh2 += stack2(k2 * e_hat)

    items = range(group * N_PAIRS)
    pm = [_mm_nt(cat([a2[i], rr2[i]], axis=0), cat([b2[i], kt2[i]], axis=0)) for i in items]
    a_abk = [jnp.where(strict, pm[i][:PAIR, :], 0.0) for i in items]
    a_rbk = [jnp.where(incl, pm[i][PAIR:, :], 0.0) for i in items]
    tinv = _neumann_inverse([a_abk[i][:, :PAIR] for i in items])
    x1 = [_mm(a_abk[i][:, PAIR:], v2[i]) for i in items]
    z = [_mm(tinv[i], cat([a2[i], x1[i]], axis=1)) for i in items]
    w2 = [cat([z[i], cat([zeros, v2[i]], axis=1)], axis=0) for i in items]
    mg = [_mm_tn(cat([bh2[i], kh2[i]], axis=0), w2[i]) for i in items]
    ry = [_mm(a_rbk[i], w2[i]) for i in items]
    lhs = [cat([rr2[i] + ry[i][:, :PAIR],
                mg[i][:, :PAIR] + jnp.where(eye, w_tot[i][:, (i % N_PAIRS) * PAIR:(i % N_PAIRS + 1) * PAIR], 0.0)],
               axis=0) for i in items]

    hs = [h_ref[p] for p in pairs]
    ones_bd = _pair_ones() if final else None
    for c in (range(group - 1, -1, -1) if rev else range(group)):
        sl = rows(c)
        yh = [_mm(lhs[c * N_PAIRS + p], hs[p]) for p in pairs]
        hs = [yh[p][PAIR:] + mg[c * N_PAIRS + p][:, PAIR:] for p in pairs]
        y2 = [yh[p][:PAIR] + ry[c * N_PAIRS + p][:, PAIR:] for p in pairs]
        y = cat([y2[p][:C] + y2[p][C:] for p in pairs], axis=1)
        if final:
            y = y + yf_ref[sl, :]
            mean = _head_sum(y, ones_bd) * (1.0 / RWKV_DIM)
            yc = y - mean
            var = _head_sum(yc * yc, ones_bd) * (1.0 / RWKV_DIM)
            yn = yc * lax.rsqrt(var + GN_EPS) * lnw_ref[...] + lnb_ref[...]
            o_ref[sl, :] = ((yn + bonus_ref[sl, :]) * g_ref[sl, :]).astype(o_ref.dtype)
        else:
            o_ref[sl, :] = y
    h_ref[...] = jnp.stack(hs, axis=0)


def _rwkv_dir(r, k2, v, kk, bv, lw, *, rev, tb, fin=None):
    B, T, W = r.shape
    nc = T // tb

    def tmap(n):
        return (nc - 1 - n) if rev else n

    blk = pl.BlockSpec((None, tb, W), lambda b, n: (b, tmap(n), 0))
    vec = pl.BlockSpec((1, W), lambda b, n: (0, 0))
    in_specs = [blk] * 6
    args = [r, k2, v, kk, bv, lw]
    if fin is not None:
        in_specs += [blk, blk, blk, vec, vec]
        args += list(fin)
    return pl.pallas_call(
        functools.partial(_rwkv_scan_body, rev=rev, final=fin is not None, n_chunks=tb // CHUNK,
                          group=RWKV_GROUP),
        out_shape=jax.ShapeDtypeStruct((B, T, W), BF16 if fin is not None else F32),
        grid=(B, nc),
        in_specs=in_specs,
        out_specs=blk,
        scratch_shapes=[pltpu.VMEM((N_PAIRS, PAIR, PAIR), F32)],
        compiler_params=_cparams(("parallel", "arbitrary")),
        name="rwkv_bwd" if rev else "rwkv_fwd",
    )(*args)


def _attn_body(sink_ref, q_ref, kp_ref, kc_ref, kn_ref, vp_ref, vc_ref, vn_ref, o_ref):
    n = pl.program_id(1)
    nb = pl.num_programs(1)
    r = lax.broadcasted_iota(jnp.int32, (BLOCK, 3 * BLOCK), 0)
    c = lax.broadcasted_iota(jnp.int32, (BLOCK, 3 * BLOCK), 1)
    d = c - BLOCK - r
    lo = jnp.where(n > 0, 0, BLOCK)
    hi = jnp.where(n < nb - 1, 3 * BLOCK, 2 * BLOCK)
    valid = (d >= -BLOCK) & (d <= BLOCK) & (c >= lo) & (c < hi)
    scale = ATT_DIM ** -0.5
    for kh in range(ATT_KV_HEADS):
        ks = slice(kh * ATT_DIM, (kh + 1) * ATT_DIM)
        kw = jnp.concatenate([kp_ref[:, ks], kc_ref[:, ks], kn_ref[:, ks]], axis=0)
        vw = jnp.concatenate([vp_ref[:, ks], vc_ref[:, ks], vn_ref[:, ks]], axis=0)
        heads = [kh * ATT_GROUP + g for g in range(ATT_GROUP)]
        qg = jnp.concatenate([q_ref[:, h * ATT_DIM:(h + 1) * ATT_DIM] for h in heads], axis=0)
        s_all = lax.dot_general(qg, kw, (((1,), (1,)), ((), ())), preferred_element_type=F32) * scale
        ps, inv_dens = [], []
        for g, h in enumerate(heads):
            s = jnp.where(valid, s_all[g * BLOCK:(g + 1) * BLOCK, :], -jnp.inf)
            sk = sink_ref[h]
            m = jnp.maximum(jnp.max(s, axis=-1, keepdims=True), sk)
            e = jnp.exp(s - m)
            inv_dens.append(1.0 / (jnp.sum(e, axis=-1, keepdims=True) + jnp.exp(sk - m)))
            ps.append(e.astype(BF16))
        o_all = jnp.dot(jnp.concatenate(ps, axis=0), vw, preferred_element_type=F32)
        for g, h in enumerate(heads):
            o_ref[:, h * ATT_DIM:(h + 1) * ATT_DIM] = (
                o_all[g * BLOCK:(g + 1) * BLOCK, :] * inv_dens[g]).astype(o_ref.dtype)


def _attention(qkv, sink):
    B, T, _ = qkv.shape
    nb = T // BLOCK
    kvw = ATT_KV_HEADS * ATT_DIM
    kc = (ATT_HEADS * ATT_DIM) // kvw
    vc = kc + 1

    def blk(cidx, off):
        def imap(b, n):
            return (b, jnp.clip(n + off, 0, nb - 1), cidx)
        return pl.BlockSpec((None, BLOCK, kvw), imap)

    return pl.pallas_call(
        _attn_body,
        out_shape=jax.ShapeDtypeStruct((B, T, ATT_HEADS * ATT_DIM), BF16),
        grid=(B, nb),
        in_specs=[pl.BlockSpec(memory_space=pltpu.SMEM),
                  pl.BlockSpec((None, BLOCK, ATT_HEADS * ATT_DIM), lambda b, n: (b, n, 0)),
                  blk(kc, -1), blk(kc, 0), blk(kc, 1), blk(vc, -1), blk(vc, 0), blk(vc, 1)],
        out_specs=pl.BlockSpec((None, BLOCK, ATT_HEADS * ATT_DIM), lambda b, n: (b, n, 0)),
        compiler_params=_cparams(("parallel", "parallel")),
        name="attention",
    )(sink, qkv, qkv, qkv, qkv, qkv, qkv, qkv)


def _ffn_body(x_ref, xp_ref, xn_ref, g_ref, wg_ref, wv_ref, cw_ref, cb_ref, wd_ref, fg_ref, o_ref, h_ref,
              *, tm, tiles_per_seq, final_norm):
    i = pl.program_id(0)
    f = pl.program_id(1)
    nf = pl.num_programs(1)

    @pl.when(f == 0)
    def _():
        def norm(x):
            ms = jnp.mean(x * x, axis=-1, keepdims=True)
            return (x * lax.rsqrt(ms + RMS_EPS) * g_ref[...]).astype(BF16)
        x = x_ref[...]
        h_ref[0:tm, :] = norm(x)
        h_ref[tm:tm + 16, :] = norm(jnp.concatenate([xp_ref[...], xn_ref[...]], axis=0))
        o_ref[...] = x

    t = i % tiles_per_seq
    has_prev = jnp.where(t > 0, 1.0, 0.0).astype(F32)
    has_next = jnp.where(t < tiles_per_seq - 1, 1.0, 0.0).astype(F32)
    ge = jnp.dot(h_ref[...], wg_ref[...], preferred_element_type=F32)
    gm = ge[0:tm, :]
    rows = lax.broadcasted_iota(jnp.int32, (tm, 1), 0)
    g_prev = jnp.where(rows == 0, ge[tm + 7:tm + 8, :] * has_prev, pltpu.roll(gm, 1, axis=0))
    g_next = jnp.where(rows == tm - 1, ge[tm + 8:tm + 9, :] * has_next, pltpu.roll(gm, tm - 1, axis=0))
    gate = g_prev * cw_ref[0:1, :] + gm * cw_ref[1:2, :] + g_next * cw_ref[2:3, :] + cb_ref[...]
    val = jnp.dot(h_ref[0:tm, :], wv_ref[...], preferred_element_type=F32)
    act = (gate * _sigmoid(gate) * val).astype(BF16)
    o_ref[...] += jnp.dot(act, wd_ref[...], preferred_element_type=F32)

    if final_norm:
        @pl.when(f == nf - 1)
        def _():
            y = o_ref[...]
            ms = jnp.mean(y * y, axis=-1, keepdims=True)
            o_ref[...] = y * lax.rsqrt(ms + RMS_EPS) * fg_ref[...]


def _ffn(x, g, w_up, conv_w, conv_b, w_down, final_g, *, layer, seq_len, tm, tf, final_norm):
    M, D = x.shape
    nf = D_FF // tf
    hb = tm // 8
    last8 = M // 8 - 1
    in_specs = [
        pl.BlockSpec((tm, D), lambda i, f: (i, 0)),
        pl.BlockSpec((8, D), lambda i, f: (jnp.maximum(i * hb - 1, 0), 0)),
        pl.BlockSpec((8, D), lambda i, f: (jnp.minimum((i + 1) * hb, last8), 0)),
        pl.BlockSpec((1, D), lambda i, f: (0, 0)),
        pl.BlockSpec((None, D, tf), lambda i, f: (layer, 0, f)),
        pl.BlockSpec((None, D, tf), lambda i, f: (layer, 0, nf + f)),
        pl.BlockSpec((3, tf), lambda i, f: (0, f)),
        pl.BlockSpec((1, tf), lambda i, f: (0, f)),
        pl.BlockSpec((None, tf, D), lambda i, f: (layer, f, 0)),
        pl.BlockSpec((1, D), lambda i, f: (0, 0)),
    ]
    return pl.pallas_call(
        functools.partial(_ffn_body, tm=tm, tiles_per_seq=seq_len // tm, final_norm=final_norm),
        out_shape=jax.ShapeDtypeStruct((M, D), F32),
        grid=(M // tm, nf),
        in_specs=in_specs,
        out_specs=pl.BlockSpec((tm, D), lambda i, f: (i, 0)),
        scratch_shapes=[pltpu.VMEM((tm + 16, D), BF16)],
        compiler_params=_cparams(("parallel", "arbitrary")),
        name="conv_ffn",
    )(x, x, x, g, w_up, w_up, conv_w, conv_b, w_down, final_g)


def _pick(n, prefs):
    for p in prefs:
        if n % p == 0:
            return p
    raise ValueError(f"no tile for {n}")


def _rope_tables(T):
    half = ATT_DIM // 2
    inv = ROPE_THETA ** (-jnp.arange(half, dtype=F32) / half)
    ang = jnp.arange(T, dtype=F32)[:, None] * inv[None, :]
    cos = jnp.cos(ang)
    sin = jnp.sin(ang)
    return jnp.concatenate([cos, cos], axis=1), jnp.concatenate([-sin, sin], axis=1)


def _prepare_params(p):
    q = dict(p)
    w_in = p['ab_w_in'][0]
    q['w_in'] = jnp.pad(w_in, ((0, 0), (0, MIX_COLS_PAD - MIX_COLS))).astype(BF16)
    mu = p['rwkv_mu'][0]
    q['mu'] = jnp.pad(mu, (0, LORA_PAD - LORA_COLS))[None, :]
    q['lb'] = jnp.cumsum(jax.nn.softmax(p['hgrn_lb'].astype(F32), axis=0), axis=0)
    q['w_out'] = p['ab_w_out'][0].astype(BF16)
    q['w_qkv'] = p['att_w_qkv'][0].astype(BF16)
    q['w_o'] = p['att_w_o'][0].astype(BF16)
    q['w_up'] = p['ffn_w_up'].astype(BF16)
    q['w_down'] = p['ffn_w_down'].astype(BF16)
    return q


def _mixer_layer(x2, q, B, T, layer):
    M = B * T
    tm = _pick(M, (512, 256, 128))
    proj = _norm_matmul(x2, q['mix_norm'][layer][None, :], q['w_in'], tm=tm, tn=MIX_COLS_PAD // 4,
                        out_dtype=F32)
    proj = proj.reshape(B, T, MIX_COLS_PAD)
    lb = q['lb'][layer][None, :]
    tb = _pick(T, (256, 128, 64))
    o_fwd = _hgrn_dir(proj, lb, rev=False, tb=tb)
    ya = _hgrn_dir(proj, lb, rev=True, tb=tb, ofwd=o_fwd, onorm=q['hgrn_onorm'][0][None, :])

    r, k2, v, kk, bv, lwf, lwb, g, bonus = _rwkv_prep(
        proj, q['mu'], q['rwkv_w0'][0], q['rwkv_w2'][0], q['rwkv_a0'][0][None, :], q['rwkv_a2'][0],
        q['rwkv_g2'][0], q['rwkv_kk'][0][None, :], q['rwkv_ka'][0][None, :],
        q['rwkv_rk'][0].reshape(1, RWKV_WIDTH), tb=tb)
    y_fwd = _rwkv_dir(r, k2, v, kk, bv, lwf, rev=False, tb=tb)
    yb = _rwkv_dir(r, k2, v, kk, bv, lwb, rev=True, tb=tb,
                   fin=(y_fwd, g, bonus, q['rwkv_ln_w'][0][None, :], q['rwkv_ln_b'][0][None, :]))
    return _matmul_res([ya.reshape(M, HGRN_WIDTH), yb.reshape(M, RWKV_WIDTH)], q['w_out'], x2,
                       tm=tm, tn=D_MODEL)


def _attention_layer(x2, q, B, T, layer, rope_tabs):
    M = B * T
    tm = _pick(M, (512, 256, 128))
    cos, sin = rope_tabs
    qkv = _norm_matmul(x2, q['mix_norm'][layer][None, :], q['w_qkv'], tm=tm, tn=QKV_COLS, out_dtype=BF16,
                       rope=(cos, sin, ATT_HEADS + ATT_KV_HEADS))
    o = _attention(qkv.reshape(B, T, QKV_COLS), q['att_sink'][0])
    return _matmul_res([o.reshape(M, ATT_HEADS * ATT_DIM)], q['w_o'], x2, tm=tm, tn=D_MODEL)


def _trunk(x, q):
    B, T, D = x.shape
    M = B * T
    x2 = x.reshape(M, D)
    rope_tabs = _rope_tables(T)
    tm = _pick(T, (1024, 512, 256, 128))
    for layer in range(DEPTH):
        if layer % 2 == 0:
            x2 = _mixer_layer(x2, q, B, T, layer)
        else:
            x2 = _attention_layer(x2, q, B, T, layer, rope_tabs)
        x2 = _ffn(x2, q['ffn_norm'][layer][None, :], q['w_up'], q['ffn_conv_w'][layer],
                  q['ffn_conv_b'][layer][None, :], q['w_down'], q['final_norm'][None, :],
                  layer=layer, seq_len=T, tm=tm, tf=512, final_norm=(layer == DEPTH - 1))
    return x2.reshape(B, T, D)


def kernel(x_prompt, x_sample, mix_norm, ab_w_in, hgrn_lb, hgrn_onorm, rwkv_mu, rwkv_w0, rwkv_w2, rwkv_a0, rwkv_a2, rwkv_g2, rwkv_kk, rwkv_ka, rwkv_rk, rwkv_ln_w, rwkv_ln_b, ab_w_out, att_w_qkv, att_sink, att_w_o, ffn_norm, ffn_w_up, ffn_conv_w, ffn_conv_b, ffn_w_down, final_norm):
    p = {
        'mix_norm': mix_norm, 'ab_w_in': ab_w_in, 'hgrn_lb': hgrn_lb, 'hgrn_onorm': hgrn_onorm,
        'rwkv_mu': rwkv_mu, 'rwkv_w0': rwkv_w0, 'rwkv_w2': rwkv_w2, 'rwkv_a0': rwkv_a0, 'rwkv_a2': rwkv_a2,
        'rwkv_g2': rwkv_g2, 'rwkv_kk': rwkv_kk, 'rwkv_ka': rwkv_ka, 'rwkv_rk': rwkv_rk,
        'rwkv_ln_w': rwkv_ln_w, 'rwkv_ln_b': rwkv_ln_b, 'ab_w_out': ab_w_out,
        'att_w_qkv': att_w_qkv, 'att_sink': att_sink, 'att_w_o': att_w_o,
        'ffn_norm': ffn_norm, 'ffn_w_up': ffn_w_up, 'ffn_conv_w': ffn_conv_w, 'ffn_conv_b': ffn_conv_b,
        'ffn_w_down': ffn_w_down, 'final_norm': final_norm,
    }
    q = _prepare_params(p)
    return (_trunk(x_prompt, q), _trunk(x_sample, q))
```

```python
import functools

import jax
import jax.numpy as jnp
from jax import lax
from jax.experimental import pallas as pl
from jax.experimental.pallas import tpu as pltpu

F32 = jnp.float32
BF16 = jnp.bfloat16

D_MODEL = 2048
DEPTH = 2
HGRN_DIM = 128
HGRN_HEADS = 8
HGRN_WIDTH = 1024
RWKV_DIM = 64
RWKV_HEADS = 16
RWKV_WIDTH = 1024
DECAY_LORA = 64
AAA_LORA = 64
GATE_LORA = 160
LORA_COLS = 2 * DECAY_LORA + AAA_LORA + GATE_LORA
LORA_PAD = 512
LORA_SLAB = 128
MIX_A_COLS = 5 * HGRN_WIDTH
MIX_COLS = MIX_A_COLS + 3 * RWKV_WIDTH + LORA_COLS
MIX_COLS_PAD = MIX_A_COLS + 3 * RWKV_WIDTH + LORA_PAD
ATT_DIM = 128
ATT_HEADS = 16
ATT_KV_HEADS = 4
ATT_GROUP = 4
QKV_COLS = (ATT_HEADS + 2 * ATT_KV_HEADS) * ATT_DIM
BLOCK = 128
ROPE_THETA = 10000.0
D_FF = 5632
RMS_EPS = 1e-6
GN_EPS = 64e-5
CHUNK = 64
PAIR = 2 * RWKV_DIM
N_PAIRS = RWKV_WIDTH // PAIR
RWKV_GROUP = 4

VMEM_LIMIT = 56 * 1024 * 1024
HI = lax.Precision.HIGHEST


def _cparams(sem):
    return pltpu.CompilerParams(dimension_semantics=sem, vmem_limit_bytes=VMEM_LIMIT)


def _mm(a, b):
    return jnp.dot(a.astype(BF16), b.astype(BF16), preferred_element_type=F32)


def _mm_nt(a, b):
    return lax.dot_general(a.astype(BF16), b.astype(BF16), (((1,), (1,)), ((), ())),
                           preferred_element_type=F32)


def _mm_tn(a, b):
    return lax.dot_general(a.astype(BF16), b.astype(BF16), (((0,), (0,)), ((), ())),
                           preferred_element_type=F32)


def _mm_hi(a, b):
    return jnp.dot(a, b, preferred_element_type=F32, precision=HI)


def _split3(x):
    hi = x.astype(BF16)
    r1 = x - hi.astype(F32)
    mid = r1.astype(BF16)
    lo = (r1 - mid.astype(F32)).astype(BF16)
    return hi, mid, lo


def _mm_cumsum(tri, x):
    t = tri.astype(BF16)
    hi, mid, lo = _split3(x)
    return (jnp.dot(t, hi, preferred_element_type=F32) + jnp.dot(t, mid, preferred_element_type=F32)
            + jnp.dot(t, lo, preferred_element_type=F32))


_mm_lora = _mm
_mm_neumann = _mm


def _sigmoid(x):
    return 1.0 / (1.0 + jnp.exp(-x))


def _norm_matmul_body(x_ref, g_ref, w_ref, o_ref, h_ref):
    @pl.when(pl.program_id(1) == 0)
    def _():
        x = x_ref[...]
        ms = jnp.mean(x * x, axis=-1, keepdims=True)
        h_ref[...] = (x * lax.rsqrt(ms + RMS_EPS) * g_ref[...]).astype(BF16)

    o_ref[...] = jnp.dot(h_ref[...], w_ref[...], preferred_element_type=F32).astype(o_ref.dtype)


def _norm_matmul_rope_body(x_ref, g_ref, w_ref, cos_ref, sin_ref, o_ref, h_ref, *, n_rope, tn):
    x = x_ref[...]
    ms = jnp.mean(x * x, axis=-1, keepdims=True)
    h_ref[...] = (x * lax.rsqrt(ms + RMS_EPS) * g_ref[...]).astype(BF16)
    cos = cos_ref[...]
    sin = sin_ref[...]
    group = 4 * ATT_DIM
    for gi in range(tn // group):
        acc = jnp.dot(h_ref[...], w_ref[:, gi * group:(gi + 1) * group], preferred_element_type=F32)
        for hh in range(group // ATT_DIM):
            a = acc[:, hh * ATT_DIM:(hh + 1) * ATT_DIM]
            head = gi * (group // ATT_DIM) + hh
            if head < n_rope:
                a = a * cos + pltpu.roll(a, ATT_DIM // 2, axis=1) * sin
            o_ref[:, head * ATT_DIM:(head + 1) * ATT_DIM] = a.astype(o_ref.dtype)


def _norm_matmul(x, g, w, *, tm, tn, out_dtype, rope=None):
    M, K = x.shape
    N = w.shape[1]
    grid = (M // tm, N // tn)
    in_specs = [pl.BlockSpec((tm, K), lambda i, j: (i, 0)),
                pl.BlockSpec((1, K), lambda i, j: (0, 0)),
                pl.BlockSpec((K, tn), lambda i, j: (0, j))]
    args = [x, g, w]
    if rope is None:
        body = _norm_matmul_body
    else:
        cos, sin, n_rope = rope
        tiles_per_seq = cos.shape[0] // tm
        in_specs += [pl.BlockSpec((tm, ATT_DIM), lambda i, j: (i % tiles_per_seq, 0)),
                     pl.BlockSpec((tm, ATT_DIM), lambda i, j: (i % tiles_per_seq, 0))]
        args += [cos, sin]
        body = functools.partial(_norm_matmul_rope_body, n_rope=n_rope, tn=tn)
    return pl.pallas_call(
        body,
        out_shape=jax.ShapeDtypeStruct((M, N), out_dtype),
        grid=grid,
        in_specs=in_specs,
        out_specs=pl.BlockSpec((tm, tn), lambda i, j: (i, j)),
        scratch_shapes=[pltpu.VMEM((tm, K), BF16)],
        compiler_params=_cparams(("parallel", "arbitrary")),
        name="norm_matmul" if rope is None else "norm_matmul_rope",
    )(*args)


def _matmul_res_body(*refs, n_parts):
    a_refs = refs[:n_parts]
    w_ref, r_ref, o_ref = refs[n_parts:]
    acc = r_ref[...]
    k0 = 0
    for a_ref in a_refs:
        kw = a_ref.shape[1]
        acc = acc + jnp.dot(a_ref[...], w_ref[k0:k0 + kw, :], preferred_element_type=F32)
        k0 += kw
    o_ref[...] = acc


def _matmul_res(a_parts, w, res, *, tm, tn):
    M, N = res.shape
    K = w.shape[0]
    in_specs = [pl.BlockSpec((tm, a.shape[1]), lambda i, j: (i, 0)) for a in a_parts]
    in_specs += [pl.BlockSpec((K, tn), lambda i, j: (0, j)),
                 pl.BlockSpec((tm, tn), lambda i, j: (i, j))]
    return pl.pallas_call(
        functools.partial(_matmul_res_body, n_parts=len(a_parts)),
        out_shape=jax.ShapeDtypeStruct((M, N), F32),
        grid=(M // tm, N // tn),
        in_specs=in_specs,
        out_specs=pl.BlockSpec((tm, tn), lambda i, j: (i, j)),
        compiler_params=_cparams(("parallel", "arbitrary")),
        name="matmul_res",
    )(*a_parts, w, res)


def _hgrn_body(*refs, rev, n_chunks, final):
    if final:
        q_ref, i_ref, z_ref, lb_ref, ofwd_ref, g_ref, onorm_ref, o_ref, st_ref = refs
    else:
        q_ref, i_ref, z_ref, lb_ref, o_ref, st_ref = refs

    @pl.when(pl.program_id(1) == 0)
    def _():
        st_ref[...] = jnp.zeros_like(st_ref)

    lb = lb_ref[...]
    row = lax.broadcasted_iota(jnp.int32, (CHUNK, CHUNK), 0)
    col = lax.broadcasted_iota(jnp.int32, (CHUNK, CHUNK), 1)
    keep = (row <= col) if rev else (row >= col)
    tri = jnp.where(keep, 1.0, 0.0).astype(F32)
    last = 0 if rev else CHUNK - 1

    head_slices = [slice(h * HGRN_DIM, (h + 1) * HGRN_DIM) for h in range(HGRN_HEADS)]
    sts = [st_ref[h] for h in range(HGRN_HEADS)]
    order = range(n_chunks - 1, -1, -1) if rev else range(n_chunks)
    for ci in order:
        sl = slice(ci * CHUNK, (ci + 1) * CHUNK)
        f = lb + (1.0 - lb) * _sigmoid(z_ref[sl, :])
        kk = 1.0 - f
        b = _mm_cumsum(tri, jnp.log(f))
        b_last = b[last:last + 1, :]
        q_d = q_ref[sl, :] * jnp.exp(b)
        k_d = kk * jnp.exp(-b)
        k_u = kk * jnp.exp(b_last - b)
        dec = jnp.exp(b_last)
        v = i_ref[sl, :]
        att = [jnp.where(keep, _mm_nt(q_d[:, hs], k_d[:, hs]), 0.0) for hs in head_slices]
        outs = [_mm(att[h], v[:, hs]) + _mm_nt(q_d[:, hs], sts[h]) for h, hs in enumerate(head_slices)]
        upd = [_mm_tn(v[:, hs], k_u[:, hs]) for hs in head_slices]
        sts = [sts[h] * dec[:, hs] + upd[h] for h, hs in enumerate(head_slices)]
        if final:
            g = g_ref[sl, :]
            silu_g = g * _sigmoid(g)
            normed = []
            for h, hs in enumerate(head_slices):
                oa = outs[h] + ofwd_ref[sl, hs]
                normed.append(oa * lax.rsqrt(jnp.mean(oa * oa, axis=-1, keepdims=True) + RMS_EPS))
            o_ref[sl, :] = (jnp.concatenate(normed, axis=1) * onorm_ref[...] * silu_g).astype(o_ref.dtype)
        else:
            o_ref[sl, :] = jnp.concatenate(outs, axis=1)
    st_ref[...] = jnp.stack(sts, axis=0)


def _hgrn_dir(proj, lb, *, rev, tb, ofwd=None, onorm=None):
    B, T, _ = proj.shape
    nt = T // tb
    final = ofwd is not None

    def tmap(n):
        return (nt - 1 - n) if rev else n

    def col(c):
        return pl.BlockSpec((None, tb, HGRN_WIDTH), lambda b, n: (b, tmap(n), c))

    vec = pl.BlockSpec((1, HGRN_WIDTH), lambda b, n: (0, 0))
    in_specs = [col(0), col(1), col(3 if rev else 2), vec]
    args = [proj, proj, proj, lb]
    if final:
        in_specs += [col(0), col(4), vec]
        args += [ofwd, proj, onorm]
    return pl.pallas_call(
        functools.partial(_hgrn_body, rev=rev, n_chunks=tb // CHUNK, final=final),
        out_shape=jax.ShapeDtypeStruct((B, T, HGRN_WIDTH), BF16 if final else F32),
        grid=(B, nt),
        in_specs=in_specs,
        out_specs=pl.BlockSpec((None, tb, HGRN_WIDTH), lambda b, n: (b, tmap(n), 0)),
        scratch_shapes=[pltpu.VMEM((HGRN_HEADS, HGRN_DIM, HGRN_DIM), F32)],
        compiler_params=_cparams(("parallel", "arbitrary")),
        name="hgrn_bwd" if rev else "hgrn_fwd",
    )(*args)


def _head_sum(x):
    m0 = lax.broadcasted_iota(jnp.int32, (x.shape[0], PAIR), 1) < RWKV_DIM
    outs = []
    for p in range(N_PAIRS):
        xs = x[:, p * PAIR:(p + 1) * PAIR]
        s0 = jnp.sum(jnp.where(m0, xs, 0.0), axis=-1, keepdims=True)
        s1 = jnp.sum(jnp.where(m0, 0.0, xs), axis=-1, keepdims=True)
        outs.append(jnp.where(m0, s0, s1))
    return jnp.concatenate(outs, axis=1)


def _rwkv_prep_body(r_ref, k_ref, v_ref, l_ref,
                    rp_ref, kp_ref, vp_ref, lp_ref, rn_ref, kn_ref, vn_ref, ln_ref,
                    mu_r_ref, mu_k_ref, mu_v_ref, mu_l_ref,
                    w0_ref, w2f_ref, w2b_ref, a0_ref, a2_ref, g2_ref, kkw_ref, kaw_ref, rk_ref,
                    ro_ref, ko_ref, vo_ref, kko_ref, bvo_ref, lwf_ref, lwb_ref, go_ref, bo_ref,
                    *, tb):
    n = pl.program_id(1)
    nt = pl.num_programs(1)
    has_prev = jnp.where(n > 0, 1.0, 0.0).astype(F32)
    has_next = jnp.where(n < nt - 1, 1.0, 0.0).astype(F32)
    rows = lax.broadcasted_iota(jnp.int32, (tb, 1), 0)
    first = rows == 0
    lastr = rows == tb - 1

    def shift(x_ref, p_ref, n_ref, mu_ref):
        x = x_ref[...]
        prev = jnp.where(first, p_ref[7:8, :] * has_prev, pltpu.roll(x, 1, axis=0))
        nxt = jnp.where(lastr, n_ref[0:1, :] * has_next, pltpu.roll(x, tb - 1, axis=0))
        return x + mu_ref[...] * (0.5 * (prev + nxt) - x)

    r = shift(r_ref, rp_ref, rn_ref, mu_r_ref)
    k = shift(k_ref, kp_ref, kn_ref, mu_k_ref)
    v = shift(v_ref, vp_ref, vn_ref, mu_v_ref)
    lo = shift(l_ref, lp_ref, ln_ref, mu_l_ref)
    wd = jnp.tanh(lo[:, 0:LORA_SLAB])
    ag = lo[:, LORA_SLAB:3 * LORA_SLAB]

    def log_decay(w0, w2):
        u = w0 + _mm_lora(wd, w2)
        softplus = jnp.maximum(-u, 0.0) + jnp.log(1.0 + jnp.exp(-jnp.abs(u)))
        return -jnp.exp(-softplus - 0.5)

    lwf_ref[...] = log_decay(w0_ref[0:1, :], w2f_ref[...])
    lwb_ref[...] = log_decay(w0_ref[1:2, :], w2b_ref[...])
    a = _sigmoid(a0_ref[...] + _mm_lora(ag[:, 0:LORA_SLAB], a2_ref[...]))
    go_ref[...] = _mm_lora(_sigmoid(ag), g2_ref[...])

    kk = k * kkw_ref[...]
    norm = jnp.maximum(jnp.sqrt(_head_sum(kk * kk)), 1e-12)
    kk = kk / norm
    k2 = k * (1.0 + (a - 1.0) * kaw_ref[...])
    ro_ref[...] = r
    ko_ref[...] = k2
    vo_ref[...] = v
    kko_ref[...] = kk
    bvo_ref[...] = kk * a
    bo_ref[...] = _head_sum(r * k2 * rk_ref[...]) * v


def _rwkv_prep(proj, mu, w0, w2, a0, a2, g2, kkw, kaw, rk, *, tb):
    B, T, _ = proj.shape
    nt = T // tb
    hb = tb // 8
    last8 = T // 8 - 1

    def main(width, c):
        return pl.BlockSpec((None, tb, width), lambda b, n: (b, n, c))

    def prev(width, c):
        return pl.BlockSpec((None, 8, width), lambda b, n: (b, jnp.maximum(n * hb - 1, 0), c))

    def nxt(width, c):
        return pl.BlockSpec((None, 8, width), lambda b, n: (b, jnp.minimum((n + 1) * hb, last8), c))

    def full(a):
        return pl.BlockSpec(a.shape, lambda b, n: (0,) * a.ndim)

    W = RWKV_WIDTH
    c_l = (MIX_A_COLS + 3 * W) // LORA_PAD
    zpad = lambda a, before, total: jnp.pad(a, ((before, total - before - a.shape[0]), (0, 0)))
    params = [mu[:, 0:W], mu[:, W:2 * W], mu[:, 2 * W:3 * W], mu[:, 3 * W:],
              w0, zpad(w2[0], 0, LORA_SLAB), zpad(w2[1], DECAY_LORA, LORA_SLAB), a0,
              zpad(a2, 0, LORA_SLAB), zpad(g2, AAA_LORA, 2 * LORA_SLAB), kkw, kaw, rk]
    in_specs = ([main(W, 5), main(W, 6), main(W, 7), main(LORA_PAD, c_l),
                 prev(W, 5), prev(W, 6), prev(W, 7), prev(LORA_PAD, c_l),
                 nxt(W, 5), nxt(W, 6), nxt(W, 7), nxt(LORA_PAD, c_l)]
                + [full(p) for p in params])
    out = jax.ShapeDtypeStruct((B, T, W), F32)
    return pl.pallas_call(
        functools.partial(_rwkv_prep_body, tb=tb),
        out_shape=[out] * 9,
        grid=(B, nt),
        in_specs=in_specs,
        out_specs=[pl.BlockSpec((None, tb, W), lambda b, n: (b, n, 0))] * 9,
        compiler_params=_cparams(("parallel", "parallel")),
        name="rwkv_prep",
    )(*([proj] * 12), *params)


def _bd(x):
    lane = lax.broadcasted_iota(jnp.int32, x.shape, 1) % PAIR
    m0 = lane < RWKV_DIM
    return jnp.concatenate([jnp.where(m0, x, 0.0), jnp.where(m0, 0.0, x)], axis=0)

def _neumann_inverse(mats):
    n, w = mats[0].shape
    eye = jnp.where(lax.broadcasted_iota(jnp.int32, (n, w), 0) == lax.broadcasted_iota(jnp.int32, (n, w), 1) % n,
                    1.0, 0.0).astype(F32)
    ps = [eye + a for a in mats]
    aks = [_mm_neumann(a, _bd(a)) for a in mats]
    for _ in range(4):
        ss = [_mm_neumann(ak, _bd(jnp.concatenate([ak, p], axis=1))) for p, ak in zip(ps, aks)]
        ps = [p + s[:, w:] for p, s in zip(ps, ss)]
        aks = [s[:, :w] for s in ss]
    return [p + _mm_neumann(ak, _bd(p)) for p, ak in zip(ps, aks)]


def _rwkv_scan_body(*refs, rev, final, n_chunks, group):
    h_ref = refs[-1]

    @pl.when(pl.program_id(1) == 0)
    def _():
        h_ref[...] = jnp.zeros_like(h_ref)

    n_groups = n_chunks // group
    if n_groups == 1:
        _rwkv_group(refs, 0, rev=rev, final=final, group=group)
    else:
        def body(i, carry):
            gi = (n_groups - 1 - i) if rev else i
            _rwkv_group(refs, gi * (group * CHUNK), rev=rev, final=final, group=group)
            return carry

        lax.fori_loop(0, n_groups, body, 0)


def _rwkv_group(refs, base, *, rev, final, group):
    if final:
        (r_ref, k_ref, v_ref, kk_ref, bv_ref, lw_ref, yf_ref, g_ref, bonus_ref, lnw_ref, lnb_ref,
         o_ref, h_ref) = refs
    else:
        r_ref, k_ref, v_ref, kk_ref, bv_ref, lw_ref, o_ref, h_ref = refs

    C = CHUNK
    row = lax.broadcasted_iota(jnp.int32, (C, C), 0)
    col = lax.broadcasted_iota(jnp.int32, (C, C), 1)
    tri = jnp.where((row <= col) if rev else (row >= col), 1.0, 0.0).astype(F32)
    last = 0 if rev else C - 1
    t_row = lax.broadcasted_iota(jnp.int32, (C, 2 * PAIR), 0)
    t_col = lax.broadcasted_iota(jnp.int32, (C, 2 * PAIR), 1) % C
    strict = (t_row < t_col) if rev else (t_row > t_col)
    incl = (t_row <= t_col) if rev else (t_row >= t_col)
    ch_row = lax.broadcasted_iota(jnp.int32, (PAIR, 2 * PAIR), 0)
    ch_col = lax.broadcasted_iota(jnp.int32, (PAIR, 2 * PAIR), 1) % PAIR
    same_head = (ch_row // RWKV_DIM) == (ch_col // RWKV_DIM)
    eye = (lax.broadcasted_iota(jnp.int32, (PAIR, PAIR), 0)
           == lax.broadcasted_iota(jnp.int32, (PAIR, PAIR), 1))
    zeros = jnp.zeros((C, PAIR), F32)
    pairs = range(N_PAIRS)
    cat = jnp.concatenate

    def rows(c):
        if isinstance(base, int):
            return slice(base + c * C, base + (c + 1) * C)
        return pl.ds(pl.multiple_of(base + c * C, C), C)

    def slabs(x):
        return [x[:, p * PAIR:(p + 1) * PAIR] for p in pairs]

    at, rt, bt, kt, vv, bh, kh, w_tot = [], [], [], [], [], [], [], []
    for c in range(group):
        sl = rows(c)
        lw = lw_ref[sl, :]
        c_inc = _mm_cumsum(tri, lw)
        c_exc = c_inc - lw
        c_tot = c_inc[last:last + 1, :]
        e_ninc = jnp.exp(-c_inc)
        e_hat = jnp.exp(c_tot - c_inc)
        kk = kk_ref[sl, :]
        bv = bv_ref[sl, :]
        k2 = k_ref[sl, :]
        w_tot += slabs(jnp.exp(c_tot))
        at += slabs(-kk * jnp.exp(c_exc))
        rt += slabs(r_ref[sl, :] * jnp.exp(c_inc))
        bt += slabs(bv * e_ninc)
        kt += slabs(k2 * e_ninc)
        vv += slabs(v_ref[sl, :])
        bh += slabs(bv * e_hat)
        kh += slabs(k2 * e_hat)

    items = range(group * N_PAIRS)
    pm = [_mm_nt(cat([at[i], rt[i]], axis=0), cat([_bd(bt[i]), _bd(kt[i])], axis=0)) for i in items]
    a_abk = [jnp.where(strict, pm[i][:C, :], 0.0) for i in items]
    a_rbk = [jnp.where(incl, pm[i][C:, :], 0.0) for i in items]
    tinv = _neumann_inverse([a_abk[i][:, :PAIR] for i in items])
    x1 = [_mm(a_abk[i][:, PAIR:], _bd(vv[i])) for i in items]
    z = [_mm(tinv[i], _bd(cat([at[i], x1[i]], axis=1))) for i in items]
    w2 = [cat([z[i], cat([zeros, vv[i]], axis=1)], axis=0) for i in items]
    mg = [jnp.where(same_head, _mm_tn(cat([bh[i], kh[i]], axis=0), w2[i]), 0.0) for i in items]
    ry = [_mm(a_rbk[i], cat([_bd(z[i]), _bd(w2[i][C:])], axis=0)) for i in items]
    lhs = [cat([rt[i] + ry[i][:, :PAIR], mg[i][:, :PAIR] + jnp.where(eye, w_tot[i], 0.0)], axis=0)
           for i in items]

    hs = [h_ref[p] for p in pairs]
    for c in (range(group - 1, -1, -1) if rev else range(group)):
        sl = rows(c)
        yh = [_mm(lhs[c * N_PAIRS + p], hs[p]) for p in pairs]
        hs = [yh[p][C:] + mg[c * N_PAIRS + p][:, PAIR:] for p in pairs]
        y = cat([yh[p][:C] + ry[c * N_PAIRS + p][:, PAIR:] for p in pairs], axis=1)
        if final:
            y = y + yf_ref[sl, :]
            mean = _head_sum(y) * (1.0 / RWKV_DIM)
            yc = y - mean
            var = _head_sum(yc * yc) * (1.0 / RWKV_DIM)
            yn = yc * lax.rsqrt(var + GN_EPS) * lnw_ref[...] + lnb_ref[...]
            o_ref[sl, :] = ((yn + bonus_ref[sl, :]) * g_ref[sl, :]).astype(o_ref.dtype)
        else:
            o_ref[sl, :] = y
    h_ref[...] = jnp.stack(hs, axis=0)


def _rwkv_dir(r, k2, v, kk, bv, lw, *, rev, tb, fin=None):
    B, T, W = r.shape
    nc = T // tb

    def tmap(n):
        return (nc - 1 - n) if rev else n

    blk = pl.BlockSpec((None, tb, W), lambda b, n: (b, tmap(n), 0))
    vec = pl.BlockSpec((1, W), lambda b, n: (0, 0))
    in_specs = [blk] * 6
    args = [r, k2, v, kk, bv, lw]
    if fin is not None:
        in_specs += [blk, blk, blk, vec, vec]
        args += list(fin)
    return pl.pallas_call(
        functools.partial(_rwkv_scan_body, rev=rev, final=fin is not None, n_chunks=tb // CHUNK,
                          group=RWKV_GROUP),
        out_shape=jax.ShapeDtypeStruct((B, T, W), BF16 if fin is not None else F32),
        grid=(B, nc),
        in_specs=in_specs,
        out_specs=blk,
        scratch_shapes=[pltpu.VMEM((N_PAIRS, PAIR, PAIR), F32)],
        compiler_params=_cparams(("parallel", "arbitrary")),
        name="rwkv_bwd" if rev else "rwkv_fwd",
    )(*args)


def _attn_body(sink_ref, q_ref, kp_ref, kc_ref, kn_ref, vp_ref, vc_ref, vn_ref, o_ref):
    n = pl.program_id(1)
    nb = pl.num_programs(1)
    r = lax.broadcasted_iota(jnp.int32, (BLOCK, 3 * BLOCK), 0)
    c = lax.broadcasted_iota(jnp.int32, (BLOCK, 3 * BLOCK), 1)
    d = c - BLOCK - r
    lo = jnp.where(n > 0, 0, BLOCK)
    hi = jnp.where(n < nb - 1, 3 * BLOCK, 2 * BLOCK)
    valid = (d >= -BLOCK) & (d <= BLOCK) & (c >= lo) & (c < hi)
    scale = ATT_DIM ** -0.5
    for kh in range(ATT_KV_HEADS):
        ks = slice(kh * ATT_DIM, (kh + 1) * ATT_DIM)
        kw = jnp.concatenate([kp_ref[:, ks], kc_ref[:, ks], kn_ref[:, ks]], axis=0)
        vw = jnp.concatenate([vp_ref[:, ks], vc_ref[:, ks], vn_ref[:, ks]], axis=0)
        heads = [kh * ATT_GROUP + g for g in range(ATT_GROUP)]
        qg = jnp.concatenate([q_ref[:, h * ATT_DIM:(h + 1) * ATT_DIM] for h in heads], axis=0)
        s_all = lax.dot_general(qg, kw, (((1,), (1,)), ((), ())), preferred_element_type=F32) * scale
        ps, inv_dens = [], []
        for g, h in enumerate(heads):
            s = jnp.where(valid, s_all[g * BLOCK:(g + 1) * BLOCK, :], -jnp.inf)
            sk = sink_ref[h]
            m = jnp.maximum(jnp.max(s, axis=-1, keepdims=True), sk)
            e = jnp.exp(s - m)
            inv_dens.append(1.0 / (jnp.sum(e, axis=-1, keepdims=True) + jnp.exp(sk - m)))
            ps.append(e.astype(BF16))
        o_all = jnp.dot(jnp.concatenate(ps, axis=0), vw, preferred_element_type=F32)
        for g, h in enumerate(heads):
            o_ref[:, h * ATT_DIM:(h + 1) * ATT_DIM] = (
                o_all[g * BLOCK:(g + 1) * BLOCK, :] * inv_dens[g]).astype(o_ref.dtype)


def _attention(qkv, sink):
    B, T, _ = qkv.shape
    nb = T // BLOCK
    kvw = ATT_KV_HEADS * ATT_DIM
    kc = (ATT_HEADS * ATT_DIM) // kvw
    vc = kc + 1

    def blk(cidx, off):
        def imap(b, n):
            return (b, jnp.clip(n + off, 0, nb - 1), cidx)
        return pl.BlockSpec((None, BLOCK, kvw), imap)

    return pl.pallas_call(
        _attn_body,
        out_shape=jax.ShapeDtypeStruct((B, T, ATT_HEADS * ATT_DIM), BF16),
        grid=(B, nb),
        in_specs=[pl.BlockSpec(memory_space=pltpu.SMEM),
                  pl.BlockSpec((None, BLOCK, ATT_HEADS * ATT_DIM), lambda b, n: (b, n, 0)),
                  blk(kc, -1), blk(kc, 0), blk(kc, 1), blk(vc, -1), blk(vc, 0), blk(vc, 1)],
        out_specs=pl.BlockSpec((None, BLOCK, ATT_HEADS * ATT_DIM), lambda b, n: (b, n, 0)),
        compiler_params=_cparams(("parallel", "parallel")),
        name="attention",
    )(sink, qkv, qkv, qkv, qkv, qkv, qkv, qkv)


def _ffn_body(x_ref, xp_ref, xn_ref, g_ref, wg_ref, wv_ref, cw_ref, cb_ref, wd_ref, fg_ref, o_ref, h_ref,
              *, tm, tiles_per_seq, final_norm):
    i = pl.program_id(0)
    f = pl.program_id(1)
    nf = pl.num_programs(1)

    @pl.when(f == 0)
    def _():
        def norm(x):
            ms = jnp.mean(x * x, axis=-1, keepdims=True)
            return (x * lax.rsqrt(ms + RMS_EPS) * g_ref[...]).astype(BF16)
        x = x_ref[...]
        h_ref[0:tm, :] = norm(x)
        h_ref[tm:tm + 16, :] = norm(jnp.concatenate([xp_ref[...], xn_ref[...]], axis=0))
        o_ref[...] = x

    t = i % tiles_per_seq
    has_prev = jnp.where(t > 0, 1.0, 0.0).astype(F32)
    has_next = jnp.where(t < tiles_per_seq - 1, 1.0, 0.0).astype(F32)
    ge = jnp.dot(h_ref[...], wg_ref[...], preferred_element_type=F32)
    gm = ge[0:tm, :]
    rows = lax.broadcasted_iota(jnp.int32, (tm, 1), 0)
    g_prev = jnp.where(rows == 0, ge[tm + 7:tm + 8, :] * has_prev, pltpu.roll(gm, 1, axis=0))
    g_next = jnp.where(rows == tm - 1, ge[tm + 8:tm + 9, :] * has_next, pltpu.roll(gm, tm - 1, axis=0))
    gate = g_prev * cw_ref[0:1, :] + gm * cw_ref[1:2, :] + g_next * cw_ref[2:3, :] + cb_ref[...]
    val = jnp.dot(h_ref[0:tm, :], wv_ref[...], preferred_element_type=F32)
    act = (gate * _sigmoid(gate) * val).astype(BF16)
    o_ref[...] += jnp.dot(act, wd_ref[...], preferred_element_type=F32)

    if final_norm:
        @pl.when(f == nf - 1)
        def _():
            y = o_ref[...]
            ms = jnp.mean(y * y, axis=-1, keepdims=True)
            o_ref[...] = y * lax.rsqrt(ms + RMS_EPS) * fg_ref[...]


def _ffn(x, g, w_up, conv_w, conv_b, w_down, final_g, *, layer, seq_len, tm, tf, final_norm):
    M, D = x.shape
    nf = D_FF // tf
    hb = tm // 8
    last8 = M // 8 - 1
    in_specs = [
        pl.BlockSpec((tm, D), lambda i, f: (i, 0)),
        pl.BlockSpec((8, D), lambda i, f: (jnp.maximum(i * hb - 1, 0), 0)),
        pl.BlockSpec((8, D), lambda i, f: (jnp.minimum((i + 1) * hb, last8), 0)),
        pl.BlockSpec((1, D), lambda i, f: (0, 0)),
        pl.BlockSpec((None, D, tf), lambda i, f: (layer, 0, f)),
        pl.BlockSpec((None, D, tf), lambda i, f: (layer, 0, nf + f)),
        pl.BlockSpec((3, tf), lambda i, f: (0, f)),
        pl.BlockSpec((1, tf), lambda i, f: (0, f)),
        pl.BlockSpec((None, tf, D), lambda i, f: (layer, f, 0)),
        pl.BlockSpec((1, D), lambda i, f: (0, 0)),
    ]
    return pl.pallas_call(
        functools.partial(_ffn_body, tm=tm, tiles_per_seq=seq_len // tm, final_norm=final_norm),
        out_shape=jax.ShapeDtypeStruct((M, D), F32),
        grid=(M // tm, nf),
        in_specs=in_specs,
        out_specs=pl.BlockSpec((tm, D), lambda i, f: (i, 0)),
        scratch_shapes=[pltpu.VMEM((tm + 16, D), BF16)],
        compiler_params=_cparams(("parallel", "arbitrary")),
        name="conv_ffn",
    )(x, x, x, g, w_up, w_up, conv_w, conv_b, w_down, final_g)


def _pick(n, prefs):
    for p in prefs:
        if n % p == 0:
            return p
    raise ValueError(f"no tile for {n}")


def _rope_tables(T):
    half = ATT_DIM // 2
    inv = ROPE_THETA ** (-jnp.arange(half, dtype=F32) / half)
    ang = jnp.arange(T, dtype=F32)[:, None] * inv[None, :]
    cos = jnp.cos(ang)
    sin = jnp.sin(ang)
    return jnp.concatenate([cos, cos], axis=1), jnp.concatenate([-sin, sin], axis=1)


def _prepare_params(p):
    q = dict(p)
    w_in = p['ab_w_in'][0]
    q['w_in'] = jnp.pad(w_in, ((0, 0), (0, MIX_COLS_PAD - MIX_COLS))).astype(BF16)
    mu = p['rwkv_mu'][0]
    q['mu'] = jnp.pad(mu, (0, LORA_PAD - LORA_COLS))[None, :]
    q['lb'] = jnp.cumsum(jax.nn.softmax(p['hgrn_lb'].astype(F32), axis=0), axis=0)
    q['w_out'] = p['ab_w_out'][0].astype(BF16)
    q['w_qkv'] = p['att_w_qkv'][0].astype(BF16)
    q['w_o'] = p['att_w_o'][0].astype(BF16)
    q['w_up'] = p['ffn_w_up'].astype(BF16)
    q['w_down'] = p['ffn_w_down'].astype(BF16)
    return q


def _mixer_layer(x2, q, B, T, layer):
    M = B * T
    tm = _pick(M, (512, 256, 128))
    proj = _norm_matmul(x2, q['mix_norm'][layer][None, :], q['w_in'], tm=tm, tn=MIX_COLS_PAD // 4,
                        out_dtype=F32)
    proj = proj.reshape(B, T, MIX_COLS_PAD)
    lb = q['lb'][layer][None, :]
    tb = _pick(T, (256, 128, 64))
    o_fwd = _hgrn_dir(proj, lb, rev=False, tb=tb)
    ya = _hgrn_dir(proj, lb, rev=True, tb=tb, ofwd=o_fwd, onorm=q['hgrn_onorm'][0][None, :])

    r, k2, v, kk, bv, lwf, lwb, g, bonus = _rwkv_prep(
        proj, q['mu'], q['rwkv_w0'][0], q['rwkv_w2'][0], q['rwkv_a0'][0][None, :], q['rwkv_a2'][0],
        q['rwkv_g2'][0], q['rwkv_kk'][0][None, :], q['rwkv_ka'][0][None, :],
        q['rwkv_rk'][0].reshape(1, RWKV_WIDTH), tb=tb)
    y_fwd = _rwkv_dir(r, k2, v, kk, bv, lwf, rev=False, tb=tb)
    yb = _rwkv_dir(r, k2, v, kk, bv, lwb, rev=True, tb=tb,
                   fin=(y_fwd, g, bonus, q['rwkv_ln_w'][0][None, :], q['rwkv_ln_b'][0][None, :]))
    return _matmul_res([ya.reshape(M, HGRN_WIDTH), yb.reshape(M, RWKV_WIDTH)], q['w_out'], x2,
                       tm=tm, tn=D_MODEL)


def _attention_layer(x2, q, B, T, layer, rope_tabs):
    M = B * T
    tm = _pick(M, (512, 256, 128))
    cos, sin = rope_tabs
    qkv = _norm_matmul(x2, q['mix_norm'][layer][None, :], q['w_qkv'], tm=tm, tn=QKV_COLS, out_dtype=BF16,
                       rope=(cos, sin, ATT_HEADS + ATT_KV_HEADS))
    o = _attention(qkv.reshape(B, T, QKV_COLS), q['att_sink'][0])
    return _matmul_res([o.reshape(M, ATT_HEADS * ATT_DIM)], q['w_o'], x2, tm=tm, tn=D_MODEL)


def _trunk(x, q):
    B, T, D = x.shape
    M = B * T
    x2 = x.reshape(M, D)
    rope_tabs = _rope_tables(T)
    tm = _pick(T, (1024, 512, 256, 128))
    for layer in range(DEPTH):
        if layer % 2 == 0:
            x2 = _mixer_layer(x2, q, B, T, layer)
        else:
            x2 = _attention_layer(x2, q, B, T, layer, rope_tabs)
        x2 = _ffn(x2, q['ffn_norm'][layer][None, :], q['w_up'], q['ffn_conv_w'][layer],
                  q['ffn_conv_b'][layer][None, :], q['w_down'], q['final_norm'][None, :],
                  layer=layer, seq_len=T, tm=tm, tf=512, final_norm=(layer == DEPTH - 1))
    return x2.reshape(B, T, D)


def kernel(x_prompt, x_sample, mix_norm, ab_w_in, hgrn_lb, hgrn_onorm, rwkv_mu, rwkv_w0, rwkv_w2, rwkv_a0, rwkv_a2, rwkv_g2, rwkv_kk, rwkv_ka, rwkv_rk, rwkv_ln_w, rwkv_ln_b, ab_w_out, att_w_qkv, att_sink, att_w_o, ffn_norm, ffn_w_up, ffn_conv_w, ffn_conv_b, ffn_w_down, final_norm):
    p = {
        'mix_norm': mix_norm, 'ab_w_in': ab_w_in, 'hgrn_lb': hgrn_lb, 'hgrn_onorm': hgrn_onorm,
        'rwkv_mu': rwkv_mu, 'rwkv_w0': rwkv_w0, 'rwkv_w2': rwkv_w2, 'rwkv_a0': rwkv_a0, 'rwkv_a2': rwkv_a2,
        'rwkv_g2': rwkv_g2, 'rwkv_kk': rwkv_kk, 'rwkv_ka': rwkv_ka, 'rwkv_rk': rwkv_rk,
        'rwkv_ln_w': rwkv_ln_w, 'rwkv_ln_b': rwkv_ln_b, 'ab_w_out': ab_w_out,
        'att_w_qkv': att_w_qkv, 'att_sink': att_sink, 'att_w_o': att_w_o,
        'ffn_norm': ffn_norm, 'ffn_w_up': ffn_w_up, 'ffn_conv_w': ffn_conv_w, 'ffn_conv_b': ffn_conv_b,
        'ffn_w_down': ffn_w_down, 'final_norm': final_norm,
    }
    q = _prepare_params(p)
    return (_trunk(x_prompt, q), _trunk(x_sample, q))
```

```python
import functools

import jax
import jax.numpy as jnp
from jax import lax
from jax.experimental import pallas as pl
from jax.experimental.pallas import tpu as pltpu

F32 = jnp.float32
BF16 = jnp.bfloat16

D_MODEL = 2048
DEPTH = 2
HGRN_DIM = 128
HGRN_HEADS = 8
HGRN_WIDTH = 1024
RWKV_DIM = 64
RWKV_HEADS = 16
RWKV_WIDTH = 1024
DECAY_LORA = 64
AAA_LORA = 64
GATE_LORA = 160
LORA_COLS = 2 * DECAY_LORA + AAA_LORA + GATE_LORA
LORA_PAD = 512
LORA_SLAB = 128
MIX_A_COLS = 5 * HGRN_WIDTH
MIX_COLS = MIX_A_COLS + 3 * RWKV_WIDTH + LORA_COLS
MIX_COLS_PAD = MIX_A_COLS + 3 * RWKV_WIDTH + LORA_PAD
ATT_DIM = 128
ATT_HEADS = 16
ATT_KV_HEADS = 4
ATT_GROUP = 4
QKV_COLS = (ATT_HEADS + 2 * ATT_KV_HEADS) * ATT_DIM
BLOCK = 128
ROPE_THETA = 10000.0
D_FF = 5632
RMS_EPS = 1e-6
GN_EPS = 64e-5
CHUNK = 64
PAIR = 2 * RWKV_DIM
N_PAIRS = RWKV_WIDTH // PAIR
RWKV_GROUP = 4

VMEM_LIMIT = 56 * 1024 * 1024


def _cparams(sem):
    return pltpu.CompilerParams(dimension_semantics=sem, vmem_limit_bytes=VMEM_LIMIT)


def _mm(a, b):
    return jnp.dot(a.astype(BF16), b.astype(BF16), preferred_element_type=F32)


def _mm_nt(a, b):
    return lax.dot_general(a.astype(BF16), b.astype(BF16), (((1,), (1,)), ((), ())),
                           preferred_element_type=F32)


def _mm_tn(a, b):
    return lax.dot_general(a.astype(BF16), b.astype(BF16), (((0,), (0,)), ((), ())),
                           preferred_element_type=F32)


def _cumsum_rows(x, rev):
    n = x.shape[0]
    row = lax.broadcasted_iota(jnp.int32, (n, 1), 0)
    s = 1
    while s < n:
        if rev:
            x = x + jnp.where(row < n - s, pltpu.roll(x, n - s, axis=0), 0.0)
        else:
            x = x + jnp.where(row >= s, pltpu.roll(x, s, axis=0), 0.0)
        s *= 2
    return x


_mm_lora = _mm
_mm_neumann = _mm


def _sigmoid(x):
    return 1.0 / (1.0 + jnp.exp(-x))


def _norm_matmul_body(x_ref, g_ref, w_ref, o_ref, h_ref):
    @pl.when(pl.program_id(1) == 0)
    def _():
        x = x_ref[...]
        ms = jnp.mean(x * x, axis=-1, keepdims=True)
        h_ref[...] = (x * lax.rsqrt(ms + RMS_EPS) * g_ref[...]).astype(BF16)

    o_ref[...] = jnp.dot(h_ref[...], w_ref[...], preferred_element_type=F32).astype(o_ref.dtype)


def _norm_matmul_rope_body(x_ref, g_ref, w_ref, cos_ref, sin_ref, o_ref, h_ref, *, n_rope, tn):
    x = x_ref[...]
    ms = jnp.mean(x * x, axis=-1, keepdims=True)
    h_ref[...] = (x * lax.rsqrt(ms + RMS_EPS) * g_ref[...]).astype(BF16)
    cos = cos_ref[...]
    sin = sin_ref[...]
    group = 4 * ATT_DIM
    for gi in range(tn // group):
        acc = jnp.dot(h_ref[...], w_ref[:, gi * group:(gi + 1) * group], preferred_element_type=F32)
        for hh in range(group // ATT_DIM):
            a = acc[:, hh * ATT_DIM:(hh + 1) * ATT_DIM]
            head = gi * (group // ATT_DIM) + hh
            if head < n_rope:
                a = a * cos + pltpu.roll(a, ATT_DIM // 2, axis=1) * sin
            o_ref[:, head * ATT_DIM:(head + 1) * ATT_DIM] = a.astype(o_ref.dtype)


def _norm_matmul(x, g, w, *, tm, tn, out_dtype, rope=None):
    M, K = x.shape
    N = w.shape[1]
    grid = (M // tm, N // tn)
    in_specs = [pl.BlockSpec((tm, K), lambda i, j: (i, 0)),
                pl.BlockSpec((1, K), lambda i, j: (0, 0)),
                pl.BlockSpec((K, tn), lambda i, j: (0, j))]
    args = [x, g, w]
    if rope is None:
        body = _norm_matmul_body
    else:
        cos, sin, n_rope = rope
        tiles_per_seq = cos.shape[0] // tm
        in_specs += [pl.BlockSpec((tm, ATT_DIM), lambda i, j: (i % tiles_per_seq, 0)),
                     pl.BlockSpec((tm, ATT_DIM), lambda i, j: (i % tiles_per_seq, 0))]
        args += [cos, sin]
        body = functools.partial(_norm_matmul_rope_body, n_rope=n_rope, tn=tn)
    return pl.pallas_call(
        body,
        out_shape=jax.ShapeDtypeStruct((M, N), out_dtype),
        grid=grid,
        in_specs=in_specs,
        out_specs=pl.BlockSpec((tm, tn), lambda i, j: (i, j)),
        scratch_shapes=[pltpu.VMEM((tm, K), BF16)],
        compiler_params=_cparams(("parallel", "arbitrary")),
        name="norm_matmul" if rope is None else "norm_matmul_rope",
    )(*args)


def _matmul_res_body(*refs, n_parts):
    a_refs = refs[:n_parts]
    w_ref, r_ref, o_ref = refs[n_parts:]
    acc = r_ref[...]
    k0 = 0
    for a_ref in a_refs:
        kw = a_ref.shape[1]
        acc = acc + jnp.dot(a_ref[...], w_ref[k0:k0 + kw, :], preferred_element_type=F32)
        k0 += kw
    o_ref[...] = acc


def _matmul_res(a_parts, w, res, *, tm, tn):
    M, N = res.shape
    K = w.shape[0]
    in_specs = [pl.BlockSpec((tm, a.shape[1]), lambda i, j: (i, 0)) for a in a_parts]
    in_specs += [pl.BlockSpec((K, tn), lambda i, j: (0, j)),
                 pl.BlockSpec((tm, tn), lambda i, j: (i, j))]
    return pl.pallas_call(
        functools.partial(_matmul_res_body, n_parts=len(a_parts)),
        out_shape=jax.ShapeDtypeStruct((M, N), F32),
        grid=(M // tm, N // tn),
        in_specs=in_specs,
        out_specs=pl.BlockSpec((tm, tn), lambda i, j: (i, j)),
        compiler_params=_cparams(("parallel", "arbitrary")),
        name="matmul_res",
    )(*a_parts, w, res)


def _hgrn_body(*refs, rev, n_chunks, final):
    if final:
        q_ref, i_ref, z_ref, lb_ref, ofwd_ref, g_ref, onorm_ref, o_ref, st_ref = refs
    else:
        q_ref, i_ref, z_ref, lb_ref, o_ref, st_ref = refs

    @pl.when(pl.program_id(1) == 0)
    def _():
        st_ref[...] = jnp.zeros_like(st_ref)

    lb = lb_ref[...]
    row = lax.broadcasted_iota(jnp.int32, (CHUNK, CHUNK), 0)
    col = lax.broadcasted_iota(jnp.int32, (CHUNK, CHUNK), 1)
    keep = (row <= col) if rev else (row >= col)
    last = 0 if rev else CHUNK - 1

    head_slices = [slice(h * HGRN_DIM, (h + 1) * HGRN_DIM) for h in range(HGRN_HEADS)]
    sts = [st_ref[h] for h in range(HGRN_HEADS)]
    order = range(n_chunks - 1, -1, -1) if rev else range(n_chunks)
    for ci in order:
        sl = slice(ci * CHUNK, (ci + 1) * CHUNK)
        f = lb + (1.0 - lb) * _sigmoid(z_ref[sl, :])
        kk = 1.0 - f
        b = _cumsum_rows(jnp.log(f), rev)
        b_last = b[last:last + 1, :]
        q_d = q_ref[sl, :] * jnp.exp(b)
        k_d = kk * jnp.exp(-b)
        k_u = kk * jnp.exp(b_last - b)
        dec = jnp.exp(b_last)
        v = i_ref[sl, :]
        att = [jnp.where(keep, _mm_nt(q_d[:, hs], k_d[:, hs]), 0.0) for hs in head_slices]
        outs = [_mm(att[h], v[:, hs]) + _mm_nt(q_d[:, hs], sts[h]) for h, hs in enumerate(head_slices)]
        upd = [_mm_tn(v[:, hs], k_u[:, hs]) for hs in head_slices]
        sts = [sts[h] * dec[:, hs] + upd[h] for h, hs in enumerate(head_slices)]
        if final:
            g = g_ref[sl, :]
            silu_g = g * _sigmoid(g)
            normed = []
            for h, hs in enumerate(head_slices):
                oa = outs[h] + ofwd_ref[sl, hs]
                normed.append(oa * lax.rsqrt(jnp.mean(oa * oa, axis=-1, keepdims=True) + RMS_EPS))
            o_ref[sl, :] = (jnp.concatenate(normed, axis=1) * onorm_ref[...] * silu_g).astype(o_ref.dtype)
        else:
            o_ref[sl, :] = jnp.concatenate(outs, axis=1)
    st_ref[...] = jnp.stack(sts, axis=0)


def _hgrn_dir(proj, lb, *, rev, tb, ofwd=None, onorm=None):
    B, T, _ = proj.shape
    nt = T // tb
    final = ofwd is not None

    def tmap(n):
        return (nt - 1 - n) if rev else n

    def col(c):
        return pl.BlockSpec((None, tb, HGRN_WIDTH), lambda b, n: (b, tmap(n), c))

    vec = pl.BlockSpec((1, HGRN_WIDTH), lambda b, n: (0, 0))
    in_specs = [col(0), col(1), col(3 if rev else 2), vec]
    args = [proj, proj, proj, lb]
    if final:
        in_specs += [col(0), col(4), vec]
        args += [ofwd, proj, onorm]
    return pl.pallas_call(
        functools.partial(_hgrn_body, rev=rev, n_chunks=tb // CHUNK, final=final),
        out_shape=jax.ShapeDtypeStruct((B, T, HGRN_WIDTH), BF16 if final else F32),
        grid=(B, nt),
        in_specs=in_specs,
        out_specs=pl.BlockSpec((None, tb, HGRN_WIDTH), lambda b, n: (b, tmap(n), 0)),
        scratch_shapes=[pltpu.VMEM((HGRN_HEADS, HGRN_DIM, HGRN_DIM), F32)],
        compiler_params=_cparams(("parallel", "arbitrary")),
        name="hgrn_bwd" if rev else "hgrn_fwd",
    )(*args)


def _head_sum(x):
    m0 = lax.broadcasted_iota(jnp.int32, (x.shape[0], PAIR), 1) < RWKV_DIM
    outs = []
    for p in range(N_PAIRS):
        xs = x[:, p * PAIR:(p + 1) * PAIR]
        s0 = jnp.sum(jnp.where(m0, xs, 0.0), axis=-1, keepdims=True)
        s1 = jnp.sum(jnp.where(m0, 0.0, xs), axis=-1, keepdims=True)
        outs.append(jnp.where(m0, s0, s1))
    return jnp.concatenate(outs, axis=1)


def _rwkv_prep_body(r_ref, k_ref, v_ref, l_ref,
                    rp_ref, kp_ref, vp_ref, lp_ref, rn_ref, kn_ref, vn_ref, ln_ref,
                    mu_r_ref, mu_k_ref, mu_v_ref, mu_l_ref,
                    w0_ref, w2f_ref, w2b_ref, a0_ref, a2_ref, g2_ref, kkw_ref, kaw_ref, rk_ref,
                    ro_ref, ko_ref, vo_ref, kko_ref, bvo_ref, lwf_ref, lwb_ref, go_ref, bo_ref,
                    *, tb):
    n = pl.program_id(1)
    nt = pl.num_programs(1)
    has_prev = jnp.where(n > 0, 1.0, 0.0).astype(F32)
    has_next = jnp.where(n < nt - 1, 1.0, 0.0).astype(F32)
    rows = lax.broadcasted_iota(jnp.int32, (tb, 1), 0)
    first = rows == 0
    lastr = rows == tb - 1

    def shift(x_ref, p_ref, n_ref, mu_ref):
        x = x_ref[...]
        prev = jnp.where(first, p_ref[7:8, :] * has_prev, pltpu.roll(x, 1, axis=0))
        nxt = jnp.where(lastr, n_ref[0:1, :] * has_next, pltpu.roll(x, tb - 1, axis=0))
        return x + mu_ref[...] * (0.5 * (prev + nxt) - x)

    r = shift(r_ref, rp_ref, rn_ref, mu_r_ref)
    k = shift(k_ref, kp_ref, kn_ref, mu_k_ref)
    v = shift(v_ref, vp_ref, vn_ref, mu_v_ref)
    lo = shift(l_ref, lp_ref, ln_ref, mu_l_ref)
    wd = jnp.tanh(lo[:, 0:LORA_SLAB])
    ag = lo[:, LORA_SLAB:3 * LORA_SLAB]

    def log_decay(w0, w2):
        u = w0 + _mm_lora(wd, w2)
        softplus = jnp.maximum(-u, 0.0) + jnp.log(1.0 + jnp.exp(-jnp.abs(u)))
        return -jnp.exp(-softplus - 0.5)

    lwf_ref[...] = log_decay(w0_ref[0:1, :], w2f_ref[...])
    lwb_ref[...] = log_decay(w0_ref[1:2, :], w2b_ref[...])
    a = _sigmoid(a0_ref[...] + _mm_lora(ag[:, 0:LORA_SLAB], a2_ref[...]))
    go_ref[...] = _mm_lora(_sigmoid(ag), g2_ref[...])

    kk = k * kkw_ref[...]
    norm = jnp.maximum(jnp.sqrt(_head_sum(kk * kk)), 1e-12)
    kk = kk / norm
    k2 = k * (1.0 + (a - 1.0) * kaw_ref[...])
    ro_ref[...] = r
    ko_ref[...] = k2
    vo_ref[...] = v
    kko_ref[...] = kk
    bvo_ref[...] = kk * a
    bo_ref[...] = _head_sum(r * k2 * rk_ref[...]) * v


def _rwkv_prep(proj, mu, w0, w2, a0, a2, g2, kkw, kaw, rk, *, tb):
    B, T, _ = proj.shape
    nt = T // tb
    hb = tb // 8
    last8 = T // 8 - 1

    def main(width, c):
        return pl.BlockSpec((None, tb, width), lambda b, n: (b, n, c))

    def prev(width, c):
        return pl.BlockSpec((None, 8, width), lambda b, n: (b, jnp.maximum(n * hb - 1, 0), c))

    def nxt(width, c):
        return pl.BlockSpec((None, 8, width), lambda b, n: (b, jnp.minimum((n + 1) * hb, last8), c))

    def full(a):
        return pl.BlockSpec(a.shape, lambda b, n: (0,) * a.ndim)

    W = RWKV_WIDTH
    c_l = (MIX_A_COLS + 3 * W) // LORA_PAD
    zpad = lambda a, before, total: jnp.pad(a, ((before, total - before - a.shape[0]), (0, 0)))
    params = [mu[:, 0:W], mu[:, W:2 * W], mu[:, 2 * W:3 * W], mu[:, 3 * W:],
              w0, zpad(w2[0], 0, LORA_SLAB), zpad(w2[1], DECAY_LORA, LORA_SLAB), a0,
              zpad(a2, 0, LORA_SLAB), zpad(g2, AAA_LORA, 2 * LORA_SLAB), kkw, kaw, rk]
    in_specs = ([main(W, 5), main(W, 6), main(W, 7), main(LORA_PAD, c_l),
                 prev(W, 5), prev(W, 6), prev(W, 7), prev(LORA_PAD, c_l),
                 nxt(W, 5), nxt(W, 6), nxt(W, 7), nxt(LORA_PAD, c_l)]
                + [full(p) for p in params])
    out = jax.ShapeDtypeStruct((B, T, W), F32)
    return pl.pallas_call(
        functools.partial(_rwkv_prep_body, tb=tb),
        out_shape=[out] * 9,
        grid=(B, nt),
        in_specs=in_specs,
        out_specs=[pl.BlockSpec((None, tb, W), lambda b, n: (b, n, 0))] * 9,
        compiler_params=_cparams(("parallel", "parallel")),
        name="rwkv_prep",
    )(*([proj] * 12), *params)


def _bd(x):
    lane = lax.broadcasted_iota(jnp.int32, x.shape, 1) % PAIR
    m0 = lane < RWKV_DIM
    return jnp.concatenate([jnp.where(m0, x, 0.0), jnp.where(m0, 0.0, x)], axis=0)

def _neumann_inverse(mats):
    n, w = mats[0].shape
    eye = jnp.where(lax.broadcasted_iota(jnp.int32, (n, w), 0) == lax.broadcasted_iota(jnp.int32, (n, w), 1) % n,
                    1.0, 0.0).astype(F32)
    ps = [eye + a for a in mats]
    aks = [_mm_neumann(a, _bd(a)) for a in mats]
    for _ in range(4):
        ss = [_mm_neumann(ak, _bd(jnp.concatenate([ak, p], axis=1))) for p, ak in zip(ps, aks)]
        ps = [p + s[:, w:] for p, s in zip(ps, ss)]
        aks = [s[:, :w] for s in ss]
    return [p + _mm_neumann(ak, _bd(p)) for p, ak in zip(ps, aks)]


def _rwkv_scan_body(*refs, rev, final, n_chunks, group):
    h_ref = refs[-1]

    @pl.when(pl.program_id(1) == 0)
    def _():
        h_ref[...] = jnp.zeros_like(h_ref)

    n_groups = n_chunks // group
    if n_groups == 1:
        _rwkv_group(refs, 0, rev=rev, final=final, group=group)
    else:
        def body(i, carry):
            gi = (n_groups - 1 - i) if rev else i
            _rwkv_group(refs, gi * (group * CHUNK), rev=rev, final=final, group=group)
            return carry

        lax.fori_loop(0, n_groups, body, 0)


def _rwkv_group(refs, base, *, rev, final, group):
    if final:
        (r_ref, k_ref, v_ref, kk_ref, bv_ref, lw_ref, yf_ref, g_ref, bonus_ref, lnw_ref, lnb_ref,
         o_ref, h_ref) = refs
    else:
        r_ref, k_ref, v_ref, kk_ref, bv_ref, lw_ref, o_ref, h_ref = refs

    C = CHUNK
    last = 0 if rev else C - 1
    t_row = lax.broadcasted_iota(jnp.int32, (C, 2 * PAIR), 0)
    t_col = lax.broadcasted_iota(jnp.int32, (C, 2 * PAIR), 1) % C
    strict = (t_row < t_col) if rev else (t_row > t_col)
    incl = (t_row <= t_col) if rev else (t_row >= t_col)
    ch_row = lax.broadcasted_iota(jnp.int32, (PAIR, 2 * PAIR), 0)
    ch_col = lax.broadcasted_iota(jnp.int32, (PAIR, 2 * PAIR), 1) % PAIR
    same_head = (ch_row // RWKV_DIM) == (ch_col // RWKV_DIM)
    eye = (lax.broadcasted_iota(jnp.int32, (PAIR, PAIR), 0)
           == lax.broadcasted_iota(jnp.int32, (PAIR, PAIR), 1))
    zeros = jnp.zeros((C, PAIR), F32)
    pairs = range(N_PAIRS)
    cat = jnp.concatenate

    def rows(c):
        if isinstance(base, int):
            return slice(base + c * C, base + (c + 1) * C)
        return pl.ds(pl.multiple_of(base + c * C, C), C)

    def slabs(x):
        return [x[:, p * PAIR:(p + 1) * PAIR] for p in pairs]

    at, rt, bt, kt, vv, bh, kh, w_tot = [], [], [], [], [], [], [], []
    for c in range(group):
        sl = rows(c)
        lw = lw_ref[sl, :]
        c_inc = _cumsum_rows(lw, rev)
        c_exc = c_inc - lw
        c_tot = c_inc[last:last + 1, :]
        e_ninc = jnp.exp(-c_inc)
        e_hat = jnp.exp(c_tot - c_inc)
        kk = kk_ref[sl, :]
        bv = bv_ref[sl, :]
        k2 = k_ref[sl, :]
        w_tot += slabs(jnp.exp(c_tot))
        at += slabs(-kk * jnp.exp(c_exc))
        rt += slabs(r_ref[sl, :] * jnp.exp(c_inc))
        bt += slabs(bv * e_ninc)
        kt += slabs(k2 * e_ninc)
        vv += slabs(v_ref[sl, :])
        bh += slabs(bv * e_hat)
        kh += slabs(k2 * e_hat)

    items = range(group * N_PAIRS)
    pm = [_mm_nt(cat([at[i], rt[i]], axis=0), cat([_bd(bt[i]), _bd(kt[i])], axis=0)) for i in items]
    a_abk = [jnp.where(strict, pm[i][:C, :], 0.0) for i in items]
    a_rbk = [jnp.where(incl, pm[i][C:, :], 0.0) for i in items]
    tinv = _neumann_inverse([a_abk[i][:, :PAIR] for i in items])
    x1 = [_mm(a_abk[i][:, PAIR:], _bd(vv[i])) for i in items]
    z = [_mm(tinv[i], _bd(cat([at[i], x1[i]], axis=1))) for i in items]
    w2 = [cat([z[i], cat([zeros, vv[i]], axis=1)], axis=0) for i in items]
    mg = [jnp.where(same_head, _mm_tn(cat([bh[i], kh[i]], axis=0), w2[i]), 0.0) for i in items]
    ry = [_mm(a_rbk[i], cat([_bd(z[i]), _bd(w2[i][C:])], axis=0)) for i in items]
    lhs = [cat([rt[i] + ry[i][:, :PAIR], mg[i][:, :PAIR] + jnp.where(eye, w_tot[i], 0.0)], axis=0)
           for i in items]

    hs = [h_ref[p] for p in pairs]
    for c in (range(group - 1, -1, -1) if rev else range(group)):
        sl = rows(c)
        yh = [_mm(lhs[c * N_PAIRS + p], hs[p]) for p in pairs]
        hs = [yh[p][C:] + mg[c * N_PAIRS + p][:, PAIR:] for p in pairs]
        y = cat([yh[p][:C] + ry[c * N_PAIRS + p][:, PAIR:] for p in pairs], axis=1)
        if final:
            y = y + yf_ref[sl, :]
            mean = _head_sum(y) * (1.0 / RWKV_DIM)
            yc = y - mean
            var = _head_sum(yc * yc) * (1.0 / RWKV_DIM)
            yn = yc * lax.rsqrt(var + GN_EPS) * lnw_ref[...] + lnb_ref[...]
            o_ref[sl, :] = ((yn + bonus_ref[sl, :]) * g_ref[sl, :]).astype(o_ref.dtype)
        else:
            o_ref[sl, :] = y
    h_ref[...] = jnp.stack(hs, axis=0)


def _rwkv_dir(r, k2, v, kk, bv, lw, *, rev, tb, fin=None):
    B, T, W = r.shape
    nc = T // tb

    def tmap(n):
        return (nc - 1 - n) if rev else n

    blk = pl.BlockSpec((None, tb, W), lambda b, n: (b, tmap(n), 0))
    vec = pl.BlockSpec((1, W), lambda b, n: (0, 0))
    in_specs = [blk] * 6
    args = [r, k2, v, kk, bv, lw]
    if fin is not None:
        in_specs += [blk, blk, blk, vec, vec]
        args += list(fin)
    return pl.pallas_call(
        functools.partial(_rwkv_scan_body, rev=rev, final=fin is not None, n_chunks=tb // CHUNK,
                          group=RWKV_GROUP),
        out_shape=jax.ShapeDtypeStruct((B, T, W), BF16 if fin is not None else F32),
        grid=(B, nc),
        in_specs=in_specs,
        out_specs=blk,
        scratch_shapes=[pltpu.VMEM((N_PAIRS, PAIR, PAIR), F32)],
        compiler_params=_cparams(("parallel", "arbitrary")),
        name="rwkv_bwd" if rev else "rwkv_fwd",
    )(*args)


def _attn_body(sink_ref, q_ref, kp_ref, kc_ref, kn_ref, vp_ref, vc_ref, vn_ref, o_ref):
    n = pl.program_id(1)
    nb = pl.num_programs(1)
    r = lax.broadcasted_iota(jnp.int32, (BLOCK, 3 * BLOCK), 0)
    c = lax.broadcasted_iota(jnp.int32, (BLOCK, 3 * BLOCK), 1)
    d = c - BLOCK - r
    lo = jnp.where(n > 0, 0, BLOCK)
    hi = jnp.where(n < nb - 1, 3 * BLOCK, 2 * BLOCK)
    valid = (d >= -BLOCK) & (d <= BLOCK) & (c >= lo) & (c < hi)
    scale = ATT_DIM ** -0.5
    for kh in range(ATT_KV_HEADS):
        ks = slice(kh * ATT_DIM, (kh + 1) * ATT_DIM)
        kw = jnp.concatenate([kp_ref[:, ks], kc_ref[:, ks], kn_ref[:, ks]], axis=0)
        vw = jnp.concatenate([vp_ref[:, ks], vc_ref[:, ks], vn_ref[:, ks]], axis=0)
        heads = [kh * ATT_GROUP + g for g in range(ATT_GROUP)]
        qg = jnp.concatenate([q_ref[:, h * ATT_DIM:(h + 1) * ATT_DIM] for h in heads], axis=0)
        s_all = lax.dot_general(qg, kw, (((1,), (1,)), ((), ())), preferred_element_type=F32) * scale
        ps, inv_dens = [], []
        for g, h in enumerate(heads):
            s = jnp.where(valid, s_all[g * BLOCK:(g + 1) * BLOCK, :], -jnp.inf)
            sk = sink_ref[h]
            m = jnp.maximum(jnp.max(s, axis=-1, keepdims=True), sk)
            e = jnp.exp(s - m)
            inv_dens.append(1.0 / (jnp.sum(e, axis=-1, keepdims=True) + jnp.exp(sk - m)))
            ps.append(e.astype(BF16))
        o_all = jnp.dot(jnp.concatenate(ps, axis=0), vw, preferred_element_type=F32)
        for g, h in enumerate(heads):
            o_ref[:, h * ATT_DIM:(h + 1) * ATT_DIM] = (
                o_all[g * BLOCK:(g + 1) * BLOCK, :] * inv_dens[g]).astype(o_ref.dtype)


def _attention(qkv, sink):
    B, T, _ = qkv.shape
    nb = T // BLOCK
    kvw = ATT_KV_HEADS * ATT_DIM
    kc = (ATT_HEADS * ATT_DIM) // kvw
    vc = kc + 1

    def blk(cidx, off):
        def imap(b, n):
            return (b, jnp.clip(n + off, 0, nb - 1), cidx)
        return pl.BlockSpec((None, BLOCK, kvw), imap)

    return pl.pallas_call(
        _attn_body,
        out_shape=jax.ShapeDtypeStruct((B, T, ATT_HEADS * ATT_DIM), BF16),
        grid=(B, nb),
        in_specs=[pl.BlockSpec(memory_space=pltpu.SMEM),
                  pl.BlockSpec((None, BLOCK, ATT_HEADS * ATT_DIM), lambda b, n: (b, n, 0)),
                  blk(kc, -1), blk(kc, 0), blk(kc, 1), blk(vc, -1), blk(vc, 0), blk(vc, 1)],
        out_specs=pl.BlockSpec((None, BLOCK, ATT_HEADS * ATT_DIM), lambda b, n: (b, n, 0)),
        compiler_params=_cparams(("parallel", "parallel")),
        name="attention",
    )(sink, qkv, qkv, qkv, qkv, qkv, qkv, qkv)


def _ffn_body(x_ref, xp_ref, xn_ref, g_ref, wg_ref, wv_ref, cw_ref, cb_ref, wd_ref, fg_ref, o_ref, h_ref,
              *, tm, tiles_per_seq, final_norm):
    i = pl.program_id(0)
    f = pl.program_id(1)
    nf = pl.num_programs(1)

    @pl.when(f == 0)
    def _():
        def norm(x):
            ms = jnp.mean(x * x, axis=-1, keepdims=True)
            return (x * lax.rsqrt(ms + RMS_EPS) * g_ref[...]).astype(BF16)
        x = x_ref[...]
        h_ref[0:tm, :] = norm(x)
        h_ref[tm:tm + 16, :] = norm(jnp.concatenate([xp_ref[...], xn_ref[...]], axis=0))
        o_ref[...] = x

    t = i % tiles_per_seq
    has_prev = jnp.where(t > 0, 1.0, 0.0).astype(F32)
    has_next = jnp.where(t < tiles_per_seq - 1, 1.0, 0.0).astype(F32)
    ge = jnp.dot(h_ref[...], wg_ref[...], preferred_element_type=F32)
    gm = ge[0:tm, :]
    rows = lax.broadcasted_iota(jnp.int32, (tm, 1), 0)
    g_prev = jnp.where(rows == 0, ge[tm + 7:tm + 8, :] * has_prev, pltpu.roll(gm, 1, axis=0))
    g_next = jnp.where(rows == tm - 1, ge[tm + 8:tm + 9, :] * has_next, pltpu.roll(gm, tm - 1, axis=0))
    gate = g_prev * cw_ref[0:1, :] + gm * cw_ref[1:2, :] + g_next * cw_ref[2:3, :] + cb_ref[...]
    val = jnp.dot(h_ref[0:tm, :], wv_ref[...], preferred_element_type=F32)
    act = (gate * _sigmoid(gate) * val).astype(BF16)
    o_ref[...] += jnp.dot(act, wd_ref[...], preferred_element_type=F32)

    if final_norm:
        @pl.when(f == nf - 1)
        def _():
            y = o_ref[...]
            ms = jnp.mean(y * y, axis=-1, keepdims=True)
            o_ref[...] = y * lax.rsqrt(ms + RMS_EPS) * fg_ref[...]


def _ffn(x, g, w_up, conv_w, conv_b, w_down, final_g, *, layer, seq_len, tm, tf, final_norm):
    M, D = x.shape
    nf = D_FF // tf
    hb = tm // 8
    last8 = M // 8 - 1
    in_specs = [
        pl.BlockSpec((tm, D), lambda i, f: (i, 0)),
        pl.BlockSpec((8, D), lambda i, f: (jnp.maximum(i * hb - 1, 0), 0)),
        pl.BlockSpec((8, D), lambda i, f: (jnp.minimum((i + 1) * hb, last8), 0)),
        pl.BlockSpec((1, D), lambda i, f: (0, 0)),
        pl.BlockSpec((None, D, tf), lambda i, f: (layer, 0, f)),
        pl.BlockSpec((None, D, tf), lambda i, f: (layer, 0, nf + f)),
        pl.BlockSpec((3, tf), lambda i, f: (0, f)),
        pl.BlockSpec((1, tf), lambda i, f: (0, f)),
        pl.BlockSpec((None, tf, D), lambda i, f: (layer, f, 0)),
        pl.BlockSpec((1, D), lambda i, f: (0, 0)),
    ]
    return pl.pallas_call(
        functools.partial(_ffn_body, tm=tm, tiles_per_seq=seq_len // tm, final_norm=final_norm),
        out_shape=jax.ShapeDtypeStruct((M, D), F32),
        grid=(M // tm, nf),
        in_specs=in_specs,
        out_specs=pl.BlockSpec((tm, D), lambda i, f: (i, 0)),
        scratch_shapes=[pltpu.VMEM((tm + 16, D), BF16)],
        compiler_params=_cparams(("parallel", "arbitrary")),
        name="conv_ffn",
    )(x, x, x, g, w_up, w_up, conv_w, conv_b, w_down, final_g)


def _pick(n, prefs):
    for p in prefs:
        if n % p == 0:
            return p
    raise ValueError(f"no tile for {n}")


def _rope_tables(T):
    half = ATT_DIM // 2
    inv = ROPE_THETA ** (-jnp.arange(half, dtype=F32) / half)
    ang = jnp.arange(T, dtype=F32)[:, None] * inv[None, :]
    cos = jnp.cos(ang)
    sin = jnp.sin(ang)
    return jnp.concatenate([cos, cos], axis=1), jnp.concatenate([-sin, sin], axis=1)


def _prepare_params(p):
    q = dict(p)
    w_in = p['ab_w_in'][0]
    q['w_in'] = jnp.pad(w_in, ((0, 0), (0, MIX_COLS_PAD - MIX_COLS))).astype(BF16)
    mu = p['rwkv_mu'][0]
    q['mu'] = jnp.pad(mu, (0, LORA_PAD - LORA_COLS))[None, :]
    q['lb'] = jnp.cumsum(jax.nn.softmax(p['hgrn_lb'].astype(F32), axis=0), axis=0)
    q['w_out'] = p['ab_w_out'][0].astype(BF16)
    q['w_qkv'] = p['att_w_qkv'][0].astype(BF16)
    q['w_o'] = p['att_w_o'][0].astype(BF16)
    q['w_up'] = p['ffn_w_up'].astype(BF16)
    q['w_down'] = p['ffn_w_down'].astype(BF16)
    return q


def _mixer_layer(x2, q, B, T, layer):
    M = B * T
    tm = _pick(M, (512, 256, 128))
    proj = _norm_matmul(x2, q['mix_norm'][layer][None, :], q['w_in'], tm=tm, tn=MIX_COLS_PAD // 4,
                        out_dtype=F32)
    proj = proj.reshape(B, T, MIX_COLS_PAD)
    lb = q['lb'][layer][None, :]
    tb = _pick(T, (256, 128, 64))
    tb_hgrn = _pick(T, (512, 256, 128, 64))
    o_fwd = _hgrn_dir(proj, lb, rev=False, tb=tb_hgrn)
    ya = _hgrn_dir(proj, lb, rev=True, tb=tb_hgrn, ofwd=o_fwd, onorm=q['hgrn_onorm'][0][None, :])

    r, k2, v, kk, bv, lwf, lwb, g, bonus = _rwkv_prep(
        proj, q['mu'], q['rwkv_w0'][0], q['rwkv_w2'][0], q['rwkv_a0'][0][None, :], q['rwkv_a2'][0],
        q['rwkv_g2'][0], q['rwkv_kk'][0][None, :], q['rwkv_ka'][0][None, :],
        q['rwkv_rk'][0].reshape(1, RWKV_WIDTH), tb=tb)
    y_fwd = _rwkv_dir(r, k2, v, kk, bv, lwf, rev=False, tb=tb)
    yb = _rwkv_dir(r, k2, v, kk, bv, lwb, rev=True, tb=tb,
                   fin=(y_fwd, g, bonus, q['rwkv_ln_w'][0][None, :], q['rwkv_ln_b'][0][None, :]))
    return _matmul_res([ya.reshape(M, HGRN_WIDTH), yb.reshape(M, RWKV_WIDTH)], q['w_out'], x2,
                       tm=tm, tn=D_MODEL)


def _attention_layer(x2, q, B, T, layer, rope_tabs):
    M = B * T
    tm = _pick(M, (512, 256, 128))
    cos, sin = rope_tabs
    qkv = _norm_matmul(x2, q['mix_norm'][layer][None, :], q['w_qkv'], tm=tm, tn=QKV_COLS, out_dtype=BF16,
                       rope=(cos, sin, ATT_HEADS + ATT_KV_HEADS))
    o = _attention(qkv.reshape(B, T, QKV_COLS), q['att_sink'][0])
    return _matmul_res([o.reshape(M, ATT_HEADS * ATT_DIM)], q['w_o'], x2, tm=tm, tn=D_MODEL)


def _trunk(x, q):
    B, T, D = x.shape
    M = B * T
    x2 = x.reshape(M, D)
    rope_tabs = _rope_tables(T)
    tm = _pick(T, (1024, 512, 256, 128))
    for layer in range(DEPTH):
        if layer % 2 == 0:
            x2 = _mixer_layer(x2, q, B, T, layer)
        else:
            x2 = _attention_layer(x2, q, B, T, layer, rope_tabs)
        x2 = _ffn(x2, q['ffn_norm'][layer][None, :], q['w_up'], q['ffn_conv_w'][layer],
                  q['ffn_conv_b'][layer][None, :], q['w_down'], q['final_norm'][None, :],
                  layer=layer, seq_len=T, tm=tm, tf=512, final_norm=(layer == DEPTH - 1))
    return x2.reshape(B, T, D)


def kernel(x_prompt, x_sample, mix_norm, ab_w_in, hgrn_lb, hgrn_onorm, rwkv_mu, rwkv_w0, rwkv_w2, rwkv_a0, rwkv_a2, rwkv_g2, rwkv_kk, rwkv_ka, rwkv_rk, rwkv_ln_w, rwkv_ln_b, ab_w_out, att_w_qkv, att_sink, att_w_o, ffn_norm, ffn_w_up, ffn_conv_w, ffn_conv_b, ffn_w_down, final_norm):
    p = {
        'mix_norm': mix_norm, 'ab_w_in': ab_w_in, 'hgrn_lb': hgrn_lb, 'hgrn_onorm': hgrn_onorm,
        'rwkv_mu': rwkv_mu, 'rwkv_w0': rwkv_w0, 'rwkv_w2': rwkv_w2, 'rwkv_a0': rwkv_a0, 'rwkv_a2': rwkv_a2,
        'rwkv_g2': rwkv_g2, 'rwkv_kk': rwkv_kk, 'rwkv_ka': rwkv_ka, 'rwkv_rk': rwkv_rk,
        'rwkv_ln_w': rwkv_ln_w, 'rwkv_ln_b': rwkv_ln_b, 'ab_w_out': ab_w_out,
        'att_w_qkv': att_w_qkv, 'att_sink': att_sink, 'att_w_o': att_w_o,
        'ffn_norm': ffn_norm, 'ffn_w_up': ffn_w_up, 'ffn_conv_w': ffn_conv_w, 'ffn_conv_b': ffn_conv_b,
        'ffn_w_down': ffn_w_down, 'final_norm': final_norm,
    }
    q = _prepare_params(p)
    return (_trunk(x_prompt, q), _trunk(x_sample, q))
```

```python
import functools

import jax
import jax.numpy as jnp
from jax import lax
from jax.experimental import pallas as pl
from jax.experimental.pallas import tpu as pltpu

F32 = jnp.float32
BF16 = jnp.bfloat16

D_MODEL = 2048
DEPTH = 2
HGRN_DIM = 128
HGRN_HEADS = 8
HGRN_WIDTH = 1024
RWKV_DIM = 64
RWKV_HEADS = 16
RWKV_WIDTH = 1024
DECAY_LORA = 64
AAA_LORA = 64
GATE_LORA = 160
LORA_COLS = 2 * DECAY_LORA + AAA_LORA + GATE_LORA
LORA_PAD = 512
LORA_SLAB = 128
MIX_A_COLS = 5 * HGRN_WIDTH
MIX_COLS = MIX_A_COLS + 3 * RWKV_WIDTH + LORA_COLS
MIX_COLS_PAD = MIX_A_COLS + 3 * RWKV_WIDTH + LORA_PAD
ATT_DIM = 128
ATT_HEADS = 16
ATT_KV_HEADS = 4
ATT_GROUP = 4
QKV_COLS = (ATT_HEADS + 2 * ATT_KV_HEADS) * ATT_DIM
BLOCK = 128
ROPE_THETA = 10000.0
D_FF = 5632
RMS_EPS = 1e-6
GN_EPS = 64e-5
CHUNK = 64
PAIR = 2 * RWKV_DIM
N_PAIRS = RWKV_WIDTH // PAIR
LOG2_E = 1.4426950408889634
EXP_NEG_HALF = 0.6065306597126334
RWKV_GROUP = 4

VMEM_LIMIT = 56 * 1024 * 1024


def _cparams(sem):
    return pltpu.CompilerParams(dimension_semantics=sem, vmem_limit_bytes=VMEM_LIMIT)


def _mm(a, b):
    return jnp.dot(a.astype(BF16), b.astype(BF16), preferred_element_type=F32)


def _mm_nt(a, b):
    return lax.dot_general(a.astype(BF16), b.astype(BF16), (((1,), (1,)), ((), ())),
                           preferred_element_type=F32)


def _mm_tn(a, b):
    return lax.dot_general(a.astype(BF16), b.astype(BF16), (((0,), (0,)), ((), ())),
                           preferred_element_type=F32)


def _cumsum_rows(x, rev):
    n = x.shape[0]
    row = lax.broadcasted_iota(jnp.int32, (n, 1), 0)
    s = 1
    while s < n:
        if rev:
            x = x + jnp.where(row < n - s, pltpu.roll(x, n - s, axis=0), 0.0)
        else:
            x = x + jnp.where(row >= s, pltpu.roll(x, s, axis=0), 0.0)
        s *= 2
    return x


_mm_lora = _mm
_mm_neumann = _mm


def _sigmoid(x):
    return 1.0 / (1.0 + jnp.exp(-x))


def _norm_matmul_body(x_ref, g_ref, w_ref, o_ref, h_ref):
    @pl.when(pl.program_id(1) == 0)
    def _():
        x = x_ref[...]
        ms = jnp.mean(x * x, axis=-1, keepdims=True)
        h_ref[...] = (x * lax.rsqrt(ms + RMS_EPS) * g_ref[...]).astype(BF16)

    o_ref[...] = jnp.dot(h_ref[...], w_ref[...], preferred_element_type=F32).astype(o_ref.dtype)


def _norm_matmul_rope_body(x_ref, g_ref, w_ref, cos_ref, sin_ref, o_ref, h_ref, *, n_rope, tn):
    x = x_ref[...]
    ms = jnp.mean(x * x, axis=-1, keepdims=True)
    h_ref[...] = (x * lax.rsqrt(ms + RMS_EPS) * g_ref[...]).astype(BF16)
    cos = cos_ref[...]
    sin = sin_ref[...]
    group = 4 * ATT_DIM
    for gi in range(tn // group):
        acc = jnp.dot(h_ref[...], w_ref[:, gi * group:(gi + 1) * group], preferred_element_type=F32)
        for hh in range(group // ATT_DIM):
            a = acc[:, hh * ATT_DIM:(hh + 1) * ATT_DIM]
            head = gi * (group // ATT_DIM) + hh
            if head < n_rope:
                a = a * cos + pltpu.roll(a, ATT_DIM // 2, axis=1) * sin
            o_ref[:, head * ATT_DIM:(head + 1) * ATT_DIM] = a.astype(o_ref.dtype)


def _norm_matmul(x, g, w, *, tm, tn, out_dtype, rope=None):
    M, K = x.shape
    N = w.shape[1]
    grid = (M // tm, N // tn)
    in_specs = [pl.BlockSpec((tm, K), lambda i, j: (i, 0)),
                pl.BlockSpec((1, K), lambda i, j: (0, 0)),
                pl.BlockSpec((K, tn), lambda i, j: (0, j))]
    args = [x, g, w]
    if rope is None:
        body = _norm_matmul_body
    else:
        cos, sin, n_rope = rope
        tiles_per_seq = cos.shape[0] // tm
        in_specs += [pl.BlockSpec((tm, ATT_DIM), lambda i, j: (i % tiles_per_seq, 0)),
                     pl.BlockSpec((tm, ATT_DIM), lambda i, j: (i % tiles_per_seq, 0))]
        args += [cos, sin]
        body = functools.partial(_norm_matmul_rope_body, n_rope=n_rope, tn=tn)
    return pl.pallas_call(
        body,
        out_shape=jax.ShapeDtypeStruct((M, N), out_dtype),
        grid=grid,
        in_specs=in_specs,
        out_specs=pl.BlockSpec((tm, tn), lambda i, j: (i, j)),
        scratch_shapes=[pltpu.VMEM((tm, K), BF16)],
        compiler_params=_cparams(("parallel", "arbitrary")),
        name="norm_matmul" if rope is None else "norm_matmul_rope",
    )(*args)


def _matmul_res_body(*refs, n_parts):
    a_refs = refs[:n_parts]
    w_ref, r_ref, o_ref = refs[n_parts:]
    acc = r_ref[...]
    k0 = 0
    for a_ref in a_refs:
        kw = a_ref.shape[1]
        acc = acc + jnp.dot(a_ref[...], w_ref[k0:k0 + kw, :], preferred_element_type=F32)
        k0 += kw
    o_ref[...] = acc


def _matmul_res(a_parts, w, res, *, tm, tn):
    M, N = res.shape
    K = w.shape[0]
    in_specs = [pl.BlockSpec((tm, a.shape[1]), lambda i, j: (i, 0)) for a in a_parts]
    in_specs += [pl.BlockSpec((K, tn), lambda i, j: (0, j)),
                 pl.BlockSpec((tm, tn), lambda i, j: (i, j))]
    return pl.pallas_call(
        functools.partial(_matmul_res_body, n_parts=len(a_parts)),
        out_shape=jax.ShapeDtypeStruct((M, N), F32),
        grid=(M // tm, N // tn),
        in_specs=in_specs,
        out_specs=pl.BlockSpec((tm, tn), lambda i, j: (i, j)),
        compiler_params=_cparams(("parallel", "arbitrary")),
        name="matmul_res",
    )(*a_parts, w, res)


def _hgrn_body(*refs, rev, n_chunks, final):
    if final:
        q_ref, i_ref, z_ref, lb_ref, ofwd_ref, g_ref, onorm_ref, o_ref, st_ref = refs
    else:
        q_ref, i_ref, z_ref, lb_ref, o_ref, st_ref = refs

    @pl.when(pl.program_id(1) == 0)
    def _():
        st_ref[...] = jnp.zeros_like(st_ref)

    lb = lb_ref[...]
    row = lax.broadcasted_iota(jnp.int32, (CHUNK, CHUNK), 0)
    col = lax.broadcasted_iota(jnp.int32, (CHUNK, CHUNK), 1)
    keep = (row <= col) if rev else (row >= col)
    last = 0 if rev else CHUNK - 1

    head_slices = [slice(h * HGRN_DIM, (h + 1) * HGRN_DIM) for h in range(HGRN_HEADS)]
    sts = [st_ref[h] for h in range(HGRN_HEADS)]
    order = range(n_chunks - 1, -1, -1) if rev else range(n_chunks)
    for ci in order:
        sl = slice(ci * CHUNK, (ci + 1) * CHUNK)
        f = lb + (1.0 - lb) * _sigmoid(z_ref[sl, :])
        kk = 1.0 - f
        b = _cumsum_rows(jnp.log(f), rev)
        b_last = b[last:last + 1, :]
        q_d = q_ref[sl, :] * jnp.exp(b)
        k_d = kk * jnp.exp(-b)
        k_u = kk * jnp.exp(b_last - b)
        dec = jnp.exp(b_last)
        v = i_ref[sl, :]
        att = [jnp.where(keep, _mm_nt(q_d[:, hs], k_d[:, hs]), 0.0) for hs in head_slices]
        outs = [_mm(att[h], v[:, hs]) + _mm_nt(q_d[:, hs], sts[h]) for h, hs in enumerate(head_slices)]
        upd = [_mm_tn(v[:, hs], k_u[:, hs]) for hs in head_slices]
        sts = [sts[h] * dec[:, hs] + upd[h] for h, hs in enumerate(head_slices)]
        if final:
            g = g_ref[sl, :]
            silu_g = g * _sigmoid(g)
            normed = []
            for h, hs in enumerate(head_slices):
                oa = outs[h] + ofwd_ref[sl, hs]
                normed.append(oa * lax.rsqrt(jnp.mean(oa * oa, axis=-1, keepdims=True) + RMS_EPS))
            o_ref[sl, :] = (jnp.concatenate(normed, axis=1) * onorm_ref[...] * silu_g).astype(o_ref.dtype)
        else:
            o_ref[sl, :] = jnp.concatenate(outs, axis=1)
    st_ref[...] = jnp.stack(sts, axis=0)


def _hgrn_dir(proj, lb, *, rev, tb, ofwd=None, onorm=None):
    B, T, _ = proj.shape
    nt = T // tb
    final = ofwd is not None

    def tmap(n):
        return (nt - 1 - n) if rev else n

    def col(c):
        return pl.BlockSpec((None, tb, HGRN_WIDTH), lambda b, n: (b, tmap(n), c))

    vec = pl.BlockSpec((1, HGRN_WIDTH), lambda b, n: (0, 0))
    in_specs = [col(0), col(1), col(3 if rev else 2), vec]
    args = [proj, proj, proj, lb]
    if final:
        in_specs += [col(0), col(4), vec]
        args += [ofwd, proj, onorm]
    return pl.pallas_call(
        functools.partial(_hgrn_body, rev=rev, n_chunks=tb // CHUNK, final=final),
        out_shape=jax.ShapeDtypeStruct((B, T, HGRN_WIDTH), BF16 if final else F32),
        grid=(B, nt),
        in_specs=in_specs,
        out_specs=pl.BlockSpec((None, tb, HGRN_WIDTH), lambda b, n: (b, tmap(n), 0)),
        scratch_shapes=[pltpu.VMEM((HGRN_HEADS, HGRN_DIM, HGRN_DIM), F32)],
        compiler_params=_cparams(("parallel", "arbitrary")),
        name="hgrn_bwd" if rev else "hgrn_fwd",
    )(*args)


def _head_sum(x):
    m0 = lax.broadcasted_iota(jnp.int32, (x.shape[0], PAIR), 1) < RWKV_DIM
    outs = []
    for p in range(N_PAIRS):
        xs = x[:, p * PAIR:(p + 1) * PAIR]
        s0 = jnp.sum(jnp.where(m0, xs, 0.0), axis=-1, keepdims=True)
        s1 = jnp.sum(jnp.where(m0, 0.0, xs), axis=-1, keepdims=True)
        outs.append(jnp.where(m0, s0, s1))
    return jnp.concatenate(outs, axis=1)


def _rwkv_prep_body(r_ref, k_ref, v_ref, l_ref,
                    rp_ref, kp_ref, vp_ref, lp_ref, rn_ref, kn_ref, vn_ref, ln_ref,
                    mu_r_ref, mu_k_ref, mu_v_ref, mu_l_ref,
                    w0_ref, w2f_ref, w2b_ref, a0_ref, a2_ref, g2_ref, kkw_ref, kaw_ref, rk_ref,
                    ro_ref, ko_ref, vo_ref, kko_ref, bvo_ref, lwf_ref, lwb_ref, go_ref, bo_ref,
                    *, tb):
    n = pl.program_id(1)
    nt = pl.num_programs(1)
    has_prev = jnp.where(n > 0, 1.0, 0.0).astype(F32)
    has_next = jnp.where(n < nt - 1, 1.0, 0.0).astype(F32)
    rows8 = lax.broadcasted_iota(jnp.int32, (8, 1), 0)
    first8 = rows8 == 0
    last8 = rows8 == 7

    def shift(x_ref, p_ref, n_ref, mu_ref):
        x = x_ref[...]
        down = pltpu.roll(x, 1, axis=0)
        up = pltpu.roll(x, tb - 1, axis=0)
        prev = jnp.concatenate([jnp.where(first8, p_ref[7:8, :] * has_prev, down[0:8]), down[8:]], axis=0)
        nxt = jnp.concatenate([up[:tb - 8], jnp.where(last8, n_ref[0:1, :] * has_next, up[tb - 8:])], axis=0)
        mu = mu_ref[...]
        return x * (1.0 - mu) + (prev + nxt) * (0.5 * mu)

    r = shift(r_ref, rp_ref, rn_ref, mu_r_ref)
    k = shift(k_ref, kp_ref, kn_ref, mu_k_ref)
    v = shift(v_ref, vp_ref, vn_ref, mu_v_ref)
    lo = shift(l_ref, lp_ref, ln_ref, mu_l_ref)
    wd = jnp.tanh(lo[:, 0:LORA_SLAB])
    ag = lo[:, LORA_SLAB:3 * LORA_SLAB]

    def log_decay(w0, w2):
        u = w0 + _mm_lora(wd, w2)
        return -EXP_NEG_HALF * _sigmoid(u)

    lwf_ref[...] = log_decay(w0_ref[0:1, :], w2f_ref[...])
    lwb_ref[...] = log_decay(w0_ref[1:2, :], w2b_ref[...])
    a = _sigmoid(a0_ref[...] + _mm_lora(ag[:, 0:LORA_SLAB], a2_ref[...]))
    go_ref[...] = _mm_lora(_sigmoid(ag), g2_ref[...])

    kk = k * kkw_ref[...]
    norm = jnp.maximum(jnp.sqrt(_head_sum(kk * kk)), 1e-12)
    kk = kk / norm
    k2 = k * (1.0 + (a - 1.0) * kaw_ref[...])
    ro_ref[...] = r
    ko_ref[...] = k2
    vo_ref[...] = v
    kko_ref[...] = kk
    bvo_ref[...] = kk * a
    bo_ref[...] = _head_sum(r * k2 * rk_ref[...]) * v


def _rwkv_prep(proj, mu, w0, w2, a0, a2, g2, kkw, kaw, rk, *, tb):
    B, T, _ = proj.shape
    nt = T // tb
    hb = tb // 8
    last8 = T // 8 - 1

    def main(width, c):
        return pl.BlockSpec((None, tb, width), lambda b, n: (b, n, c))

    def prev(width, c):
        return pl.BlockSpec((None, 8, width), lambda b, n: (b, jnp.maximum(n * hb - 1, 0), c))

    def nxt(width, c):
        return pl.BlockSpec((None, 8, width), lambda b, n: (b, jnp.minimum((n + 1) * hb, last8), c))

    def full(a):
        return pl.BlockSpec(a.shape, lambda b, n: (0,) * a.ndim)

    W = RWKV_WIDTH
    c_l = (MIX_A_COLS + 3 * W) // LORA_PAD
    zpad = lambda a, before, total: jnp.pad(a, ((before, total - before - a.shape[0]), (0, 0)))
    params = [mu[:, 0:W], mu[:, W:2 * W], mu[:, 2 * W:3 * W], mu[:, 3 * W:],
              w0, zpad(w2[0], 0, LORA_SLAB), zpad(w2[1], DECAY_LORA, LORA_SLAB), a0,
              zpad(a2, 0, LORA_SLAB), zpad(g2, AAA_LORA, 2 * LORA_SLAB), kkw, kaw, rk]
    in_specs = ([main(W, 5), main(W, 6), main(W, 7), main(LORA_PAD, c_l),
                 prev(W, 5), prev(W, 6), prev(W, 7), prev(LORA_PAD, c_l),
                 nxt(W, 5), nxt(W, 6), nxt(W, 7), nxt(LORA_PAD, c_l)]
                + [full(p) for p in params])
    out = jax.ShapeDtypeStruct((B, T, W), F32)
    return pl.pallas_call(
        functools.partial(_rwkv_prep_body, tb=tb),
        out_shape=[out] * 9,
        grid=(B, nt),
        in_specs=in_specs,
        out_specs=[pl.BlockSpec((None, tb, W), lambda b, n: (b, n, 0))] * 9,
        compiler_params=_cparams(("parallel", "parallel")),
        name="rwkv_prep",
    )(*([proj] * 12), *params)


def _bd(x):
    lane = lax.broadcasted_iota(jnp.int32, x.shape, 1) % PAIR
    m0 = lane < RWKV_DIM
    return jnp.concatenate([jnp.where(m0, x, 0.0), jnp.where(m0, 0.0, x)], axis=0)

def _neumann_inverse(mats):
    n, w = mats[0].shape
    eye = jnp.where(lax.broadcasted_iota(jnp.int32, (n, w), 0) == lax.broadcasted_iota(jnp.int32, (n, w), 1) % n,
                    1.0, 0.0).astype(F32)
    ps = [eye + a for a in mats]
    aks = [_mm_neumann(a, _bd(a)) for a in mats]
    for _ in range(4):
        ss = [_mm_neumann(ak, _bd(jnp.concatenate([ak, p], axis=1))) for p, ak in zip(ps, aks)]
        ps = [p + s[:, w:] for p, s in zip(ps, ss)]
        aks = [s[:, :w] for s in ss]
    return [p + _mm_neumann(ak, _bd(p)) for p, ak in zip(ps, aks)]


def _rwkv_scan_body(*refs, rev, final, n_chunks, group):
    h_ref = refs[-1]

    @pl.when(pl.program_id(1) == 0)
    def _():
        h_ref[...] = jnp.zeros_like(h_ref)

    n_groups = n_chunks // group
    if n_groups == 1:
        _rwkv_group(refs, 0, rev=rev, final=final, group=group)
    else:
        def body(i, carry):
            gi = (n_groups - 1 - i) if rev else i
            _rwkv_group(refs, gi * (group * CHUNK), rev=rev, final=final, group=group)
            return carry

        lax.fori_loop(0, n_groups, body, 0)


def _rwkv_group(refs, base, *, rev, final, group):
    if final:
        (r_ref, k_ref, v_ref, kk_ref, bv_ref, lw_ref, yf_ref, g_ref, bonus_ref, lnw_ref, lnb_ref,
         o_ref, h_ref) = refs
    else:
        r_ref, k_ref, v_ref, kk_ref, bv_ref, lw_ref, o_ref, h_ref = refs

    C = CHUNK
    last = 0 if rev else C - 1
    t_row = lax.broadcasted_iota(jnp.int32, (C, 2 * PAIR), 0)
    t_col = lax.broadcasted_iota(jnp.int32, (C, 2 * PAIR), 1) % C
    strict = (t_row < t_col) if rev else (t_row > t_col)
    incl = (t_row <= t_col) if rev else (t_row >= t_col)
    ch_row = lax.broadcasted_iota(jnp.int32, (PAIR, 2 * PAIR), 0)
    ch_col = lax.broadcasted_iota(jnp.int32, (PAIR, 2 * PAIR), 1) % PAIR
    same_head = (ch_row // RWKV_DIM) == (ch_col // RWKV_DIM)
    eye = (lax.broadcasted_iota(jnp.int32, (PAIR, PAIR), 0)
           == lax.broadcasted_iota(jnp.int32, (PAIR, PAIR), 1))
    zeros = jnp.zeros((C, PAIR), F32)
    pairs = range(N_PAIRS)
    cat = jnp.concatenate

    def rows(c):
        if isinstance(base, int):
            return slice(base + c * C, base + (c + 1) * C)
        return pl.ds(pl.multiple_of(base + c * C, C), C)

    def slabs(x):
        return [x[:, p * PAIR:(p + 1) * PAIR] for p in pairs]

    at, rt, bt, kt, vv, bh, kh, w_tot = [], [], [], [], [], [], [], []
    for c in range(group):
        sl = rows(c)
        lw = lw_ref[sl, :]
        c_inc = _cumsum_rows(lw, rev)
        c_exc = c_inc - lw
        c_tot = c_inc[last:last + 1, :]
        e_ninc = jnp.exp(-c_inc)
        e_hat = jnp.exp(c_tot - c_inc)
        kk = kk_ref[sl, :]
        bv = bv_ref[sl, :]
        k2 = k_ref[sl, :]
        w_tot += slabs(jnp.exp(c_tot))
        at += slabs(-kk * jnp.exp(c_exc))
        rt += slabs(r_ref[sl, :] * jnp.exp(c_inc))
        bt += slabs(bv * e_ninc)
        kt += slabs(k2 * e_ninc)
        vv += slabs(v_ref[sl, :])
        bh += slabs(bv * e_hat)
        kh += slabs(k2 * e_hat)

    items = range(group * N_PAIRS)
    pm = [_mm_nt(cat([at[i], rt[i]], axis=0), cat([_bd(bt[i]), _bd(kt[i])], axis=0)) for i in items]
    a_abk = [jnp.where(strict, pm[i][:C, :], 0.0) for i in items]
    a_rbk = [jnp.where(incl, pm[i][C:, :], 0.0) for i in items]
    tinv = _neumann_inverse([a_abk[i][:, :PAIR] for i in items])
    x1 = [_mm(a_abk[i][:, PAIR:], _bd(vv[i])) for i in items]
    z = [_mm(tinv[i], _bd(cat([at[i], x1[i]], axis=1))) for i in items]
    w2 = [cat([z[i], cat([zeros, vv[i]], axis=1)], axis=0) for i in items]
    mg = [jnp.where(same_head, _mm_tn(cat([bh[i], kh[i]], axis=0), w2[i]), 0.0) for i in items]
    ry = [_mm(a_rbk[i], cat([_bd(z[i]), _bd(w2[i][C:])], axis=0)) for i in items]
    lhs = [cat([rt[i] + ry[i][:, :PAIR], mg[i][:, :PAIR] + jnp.where(eye, w_tot[i], 0.0)], axis=0)
           for i in items]

    hs = [h_ref[p] for p in pairs]
    for c in (range(group - 1, -1, -1) if rev else range(group)):
        sl = rows(c)
        yh = [_mm(lhs[c * N_PAIRS + p], hs[p]) for p in pairs]
        hs = [yh[p][C:] + mg[c * N_PAIRS + p][:, PAIR:] for p in pairs]
        y = cat([yh[p][:C] + ry[c * N_PAIRS + p][:, PAIR:] for p in pairs], axis=1)
        if final:
            y = y + yf_ref[sl, :]
            mean = _head_sum(y) * (1.0 / RWKV_DIM)
            yc = y - mean
            var = _head_sum(yc * yc) * (1.0 / RWKV_DIM)
            yn = yc * lax.rsqrt(var + GN_EPS) * lnw_ref[...] + lnb_ref[...]
            o_ref[sl, :] = ((yn + bonus_ref[sl, :]) * g_ref[sl, :]).astype(o_ref.dtype)
        else:
            o_ref[sl, :] = y
    h_ref[...] = jnp.stack(hs, axis=0)


def _rwkv_dir(r, k2, v, kk, bv, lw, *, rev, tb, fin=None):
    B, T, W = r.shape
    nc = T // tb

    def tmap(n):
        return (nc - 1 - n) if rev else n

    blk = pl.BlockSpec((None, tb, W), lambda b, n: (b, tmap(n), 0))
    vec = pl.BlockSpec((1, W), lambda b, n: (0, 0))
    in_specs = [blk] * 6
    args = [r, k2, v, kk, bv, lw]
    if fin is not None:
        in_specs += [blk, blk, blk, vec, vec]
        args += list(fin)
    return pl.pallas_call(
        functools.partial(_rwkv_scan_body, rev=rev, final=fin is not None, n_chunks=tb // CHUNK,
                          group=RWKV_GROUP),
        out_shape=jax.ShapeDtypeStruct((B, T, W), BF16 if fin is not None else F32),
        grid=(B, nc),
        in_specs=in_specs,
        out_specs=blk,
        scratch_shapes=[pltpu.VMEM((N_PAIRS, PAIR, PAIR), F32)],
        compiler_params=_cparams(("parallel", "arbitrary")),
        name="rwkv_bwd" if rev else "rwkv_fwd",
    )(*args)


def _attn_body(sink_ref, q_ref, kp_ref, kc_ref, kn_ref, vp_ref, vc_ref, vn_ref, o_ref):
    n = pl.program_id(1)
    nb = pl.num_programs(1)
    r = lax.broadcasted_iota(jnp.int32, (BLOCK, 3 * BLOCK), 0)
    c = lax.broadcasted_iota(jnp.int32, (BLOCK, 3 * BLOCK), 1)
    d = c - BLOCK - r
    lo = jnp.where(n > 0, 0, BLOCK)
    hi = jnp.where(n < nb - 1, 3 * BLOCK, 2 * BLOCK)
    valid = (d >= -BLOCK) & (d <= BLOCK) & (c >= lo) & (c < hi)
    scale = ATT_DIM ** -0.5
    for kh in range(ATT_KV_HEADS):
        ks = slice(kh * ATT_DIM, (kh + 1) * ATT_DIM)
        kw = jnp.concatenate([kp_ref[:, ks], kc_ref[:, ks], kn_ref[:, ks]], axis=0)
        vw = jnp.concatenate([vp_ref[:, ks], vc_ref[:, ks], vn_ref[:, ks]], axis=0)
        heads = [kh * ATT_GROUP + g for g in range(ATT_GROUP)]
        qg = jnp.concatenate([q_ref[:, h * ATT_DIM:(h + 1) * ATT_DIM] for h in heads], axis=0)
        s_all = lax.dot_general(qg, kw, (((1,), (1,)), ((), ())), preferred_element_type=F32)
        ps, inv_dens = [], []
        for g, h in enumerate(heads):
            s = jnp.where(valid, s_all[g * BLOCK:(g + 1) * BLOCK, :], -jnp.inf)
            sk = sink_ref[h] * (1.0 / scale)
            m = jnp.maximum(jnp.max(s, axis=-1, keepdims=True), sk)
            e = jnp.exp2((s - m) * (scale * LOG2_E))
            inv_dens.append(1.0 / (jnp.sum(e, axis=-1, keepdims=True) + jnp.exp2((sk - m) * (scale * LOG2_E))))
            ps.append(e.astype(BF16))
        o_all = jnp.dot(jnp.concatenate(ps, axis=0), vw, preferred_element_type=F32)
        for g, h in enumerate(heads):
            o_ref[:, h * ATT_DIM:(h + 1) * ATT_DIM] = (
                o_all[g * BLOCK:(g + 1) * BLOCK, :] * inv_dens[g]).astype(o_ref.dtype)


def _attention(qkv, sink):
    B, T, _ = qkv.shape
    nb = T // BLOCK
    kvw = ATT_KV_HEADS * ATT_DIM
    kc = (ATT_HEADS * ATT_DIM) // kvw
    vc = kc + 1

    def blk(cidx, off):
        def imap(b, n):
            return (b, jnp.clip(n + off, 0, nb - 1), cidx)
        return pl.BlockSpec((None, BLOCK, kvw), imap)

    return pl.pallas_call(
        _attn_body,
        out_shape=jax.ShapeDtypeStruct((B, T, ATT_HEADS * ATT_DIM), BF16),
        grid=(B, nb),
        in_specs=[pl.BlockSpec(memory_space=pltpu.SMEM),
                  pl.BlockSpec((None, BLOCK, ATT_HEADS * ATT_DIM), lambda b, n: (b, n, 0)),
                  blk(kc, -1), blk(kc, 0), blk(kc, 1), blk(vc, -1), blk(vc, 0), blk(vc, 1)],
        out_specs=pl.BlockSpec((None, BLOCK, ATT_HEADS * ATT_DIM), lambda b, n: (b, n, 0)),
        compiler_params=_cparams(("parallel", "parallel")),
        name="attention",
    )(sink, qkv, qkv, qkv, qkv, qkv, qkv, qkv)


def _ffn_body(x_ref, xp_ref, xn_ref, g_ref, wg_ref, wv_ref, cw_ref, cb_ref, wd_ref, fg_ref, o_ref, h_ref,
              *, tm, tiles_per_seq, final_norm):
    i = pl.program_id(0)
    f = pl.program_id(1)
    nf = pl.num_programs(1)

    @pl.when(f == 0)
    def _():
        def norm(x):
            ms = jnp.mean(x * x, axis=-1, keepdims=True)
            return (x * lax.rsqrt(ms + RMS_EPS) * g_ref[...]).astype(BF16)
        x = x_ref[...]
        h_ref[0:tm, :] = norm(x)
        h_ref[tm:tm + 16, :] = norm(jnp.concatenate([xp_ref[...], xn_ref[...]], axis=0))
        o_ref[...] = x

    t = i % tiles_per_seq
    has_prev = jnp.where(t > 0, 1.0, 0.0).astype(F32)
    has_next = jnp.where(t < tiles_per_seq - 1, 1.0, 0.0).astype(F32)
    ge = jnp.dot(h_ref[...], wg_ref[...], preferred_element_type=F32)
    gm = ge[0:tm, :]
    rows = lax.broadcasted_iota(jnp.int32, (tm, 1), 0)
    g_prev = jnp.where(rows == 0, ge[tm + 7:tm + 8, :] * has_prev, pltpu.roll(gm, 1, axis=0))
    g_next = jnp.where(rows == tm - 1, ge[tm + 8:tm + 9, :] * has_next, pltpu.roll(gm, tm - 1, axis=0))
    gate = g_prev * cw_ref[0:1, :] + gm * cw_ref[1:2, :] + g_next * cw_ref[2:3, :] + cb_ref[...]
    val = jnp.dot(h_ref[0:tm, :], wv_ref[...], preferred_element_type=F32)
    act = (gate * _sigmoid(gate) * val).astype(BF16)
    o_ref[...] += jnp.dot(act, wd_ref[...], preferred_element_type=F32)

    if final_norm:
        @pl.when(f == nf - 1)
        def _():
            y = o_ref[...]
            ms = jnp.mean(y * y, axis=-1, keepdims=True)
            o_ref[...] = y * lax.rsqrt(ms + RMS_EPS) * fg_ref[...]


def _ffn(x, g, w_up, conv_w, conv_b, w_down, final_g, *, layer, seq_len, tm, tf, final_norm):
    M, D = x.shape
    nf = D_FF // tf
    hb = tm // 8
    last8 = M // 8 - 1
    in_specs = [
        pl.BlockSpec((tm, D), lambda i, f: (i, 0)),
        pl.BlockSpec((8, D), lambda i, f: (jnp.maximum(i * hb - 1, 0), 0)),
        pl.BlockSpec((8, D), lambda i, f: (jnp.minimum((i + 1) * hb, last8), 0)),
        pl.BlockSpec((1, D), lambda i, f: (0, 0)),
        pl.BlockSpec((None, D, tf), lambda i, f: (layer, 0, f)),
        pl.BlockSpec((None, D, tf), lambda i, f: (layer, 0, nf + f)),
        pl.BlockSpec((3, tf), lambda i, f: (0, f)),
        pl.BlockSpec((1, tf), lambda i, f: (0, f)),
        pl.BlockSpec((None, tf, D), lambda i, f: (layer, f, 0)),
        pl.BlockSpec((1, D), lambda i, f: (0, 0)),
    ]
    return pl.pallas_call(
        functools.partial(_ffn_body, tm=tm, tiles_per_seq=seq_len // tm, final_norm=final_norm),
        out_shape=jax.ShapeDtypeStruct((M, D), F32),
        grid=(M // tm, nf),
        in_specs=in_specs,
        out_specs=pl.BlockSpec((tm, D), lambda i, f: (i, 0)),
        scratch_shapes=[pltpu.VMEM((tm + 16, D), BF16)],
        compiler_params=_cparams(("parallel", "arbitrary")),
        name="conv_ffn",
    )(x, x, x, g, w_up, w_up, conv_w, conv_b, w_down, final_g)


def _pick(n, prefs):
    for p in prefs:
        if n % p == 0:
            return p
    raise ValueError(f"no tile for {n}")


def _rope_tables(T):
    half = ATT_DIM // 2
    inv = ROPE_THETA ** (-jnp.arange(half, dtype=F32) / half)
    ang = jnp.arange(T, dtype=F32)[:, None] * inv[None, :]
    cos = jnp.cos(ang)
    sin = jnp.sin(ang)
    return jnp.concatenate([cos, cos], axis=1), jnp.concatenate([-sin, sin], axis=1)


def _prepare_params(p):
    q = dict(p)
    w_in = p['ab_w_in'][0]
    q['w_in'] = jnp.pad(w_in, ((0, 0), (0, MIX_COLS_PAD - MIX_COLS))).astype(BF16)
    mu = p['rwkv_mu'][0]
    q['mu'] = jnp.pad(mu, (0, LORA_PAD - LORA_COLS))[None, :]
    q['lb'] = jnp.cumsum(jax.nn.softmax(p['hgrn_lb'].astype(F32), axis=0), axis=0)
    q['w_out'] = p['ab_w_out'][0].astype(BF16)
    q['w_qkv'] = p['att_w_qkv'][0].astype(BF16)
    q['w_o'] = p['att_w_o'][0].astype(BF16)
    q['w_up'] = p['ffn_w_up'].astype(BF16)
    q['w_down'] = p['ffn_w_down'].astype(BF16)
    return q


def _mixer_layer(x2, q, B, T, layer):
    M = B * T
    tm = _pick(M, (512, 256, 128))
    proj = _norm_matmul(x2, q['mix_norm'][layer][None, :], q['w_in'], tm=tm, tn=MIX_COLS_PAD // 4,
                        out_dtype=F32)
    proj = proj.reshape(B, T, MIX_COLS_PAD)
    lb = q['lb'][layer][None, :]
    tb = _pick(T, (256, 128, 64))
    tb_hgrn = _pick(T, (512, 256, 128, 64))
    o_fwd = _hgrn_dir(proj, lb, rev=False, tb=tb_hgrn)
    ya = _hgrn_dir(proj, lb, rev=True, tb=tb_hgrn, ofwd=o_fwd, onorm=q['hgrn_onorm'][0][None, :])

    r, k2, v, kk, bv, lwf, lwb, g, bonus = _rwkv_prep(
        proj, q['mu'], q['rwkv_w0'][0], q['rwkv_w2'][0], q['rwkv_a0'][0][None, :], q['rwkv_a2'][0],
        q['rwkv_g2'][0], q['rwkv_kk'][0][None, :], q['rwkv_ka'][0][None, :],
        q['rwkv_rk'][0].reshape(1, RWKV_WIDTH), tb=tb)
    y_fwd = _rwkv_dir(r, k2, v, kk, bv, lwf, rev=False, tb=tb)
    yb = _rwkv_dir(r, k2, v, kk, bv, lwb, rev=True, tb=tb,
                   fin=(y_fwd, g, bonus, q['rwkv_ln_w'][0][None, :], q['rwkv_ln_b'][0][None, :]))
    return _matmul_res([ya.reshape(M, HGRN_WIDTH), yb.reshape(M, RWKV_WIDTH)], q['w_out'], x2,
                       tm=tm, tn=D_MODEL)


def _attention_layer(x2, q, B, T, layer, rope_tabs):
    M = B * T
    tm = _pick(M, (512, 256, 128))
    cos, sin = rope_tabs
    qkv = _norm_matmul(x2, q['mix_norm'][layer][None, :], q['w_qkv'], tm=tm, tn=QKV_COLS, out_dtype=BF16,
                       rope=(cos, sin, ATT_HEADS + ATT_KV_HEADS))
    o = _attention(qkv.reshape(B, T, QKV_COLS), q['att_sink'][0])
    return _matmul_res([o.reshape(M, ATT_HEADS * ATT_DIM)], q['w_o'], x2, tm=tm, tn=D_MODEL)


def _trunk(x, q):
    B, T, D = x.shape
    M = B * T
    x2 = x.reshape(M, D)
    rope_tabs = _rope_tables(T)
    tm = _pick(T, (1024, 512, 256, 128))
    for layer in range(DEPTH):
        if layer % 2 == 0:
            x2 = _mixer_layer(x2, q, B, T, layer)
        else:
            x2 = _attention_layer(x2, q, B, T, layer, rope_tabs)
        x2 = _ffn(x2, q['ffn_norm'][layer][None, :], q['w_up'], q['ffn_conv_w'][layer],
                  q['ffn_conv_b'][layer][None, :], q['w_down'], q['final_norm'][None, :],
                  layer=layer, seq_len=T, tm=tm, tf=512, final_norm=(layer == DEPTH - 1))
    return x2.reshape(B, T, D)


def kernel(x_prompt, x_sample, mix_norm, ab_w_in, hgrn_lb, hgrn_onorm, rwkv_mu, rwkv_w0, rwkv_w2, rwkv_a0, rwkv_a2, rwkv_g2, rwkv_kk, rwkv_ka, rwkv_rk, rwkv_ln_w, rwkv_ln_b, ab_w_out, att_w_qkv, att_sink, att_w_o, ffn_norm, ffn_w_up, ffn_conv_w, ffn_conv_b, ffn_w_down, final_norm):
    p = {
        'mix_norm': mix_norm, 'ab_w_in': ab_w_in, 'hgrn_lb': hgrn_lb, 'hgrn_onorm': hgrn_onorm,
        'rwkv_mu': rwkv_mu, 'rwkv_w0': rwkv_w0, 'rwkv_w2': rwkv_w2, 'rwkv_a0': rwkv_a0, 'rwkv_a2': rwkv_a2,
        'rwkv_g2': rwkv_g2, 'rwkv_kk': rwkv_kk, 'rwkv_ka': rwkv_ka, 'rwkv_rk': rwkv_rk,
        'rwkv_ln_w': rwkv_ln_w, 'rwkv_ln_b': rwkv_ln_b, 'ab_w_out': ab_w_out,
        'att_w_qkv': att_w_qkv, 'att_sink': att_sink, 'att_w_o': att_w_o,
        'ffn_norm': ffn_norm, 'ffn_w_up': ffn_w_up, 'ffn_conv_w': ffn_conv_w, 'ffn_conv_b': ffn_conv_b,
        'ffn_w_down': ffn_w_down, 'final_norm': final_norm,
    }
    q = _prepare_params(p)
    return (_trunk(x_prompt, q), _trunk(x_sample, q))
```

```python
import functools

import jax
import jax.numpy as jnp
from jax import lax
from jax.experimental import pallas as pl
from jax.experimental.pallas import tpu as pltpu

F32 = jnp.float32
BF16 = jnp.bfloat16

D_MODEL = 2048
DEPTH = 2
HGRN_DIM = 128
HGRN_HEADS = 8
HGRN_WIDTH = 1024
RWKV_DIM = 64
RWKV_HEADS = 16
RWKV_WIDTH = 1024
DECAY_LORA = 64
AAA_LORA = 64
GATE_LORA = 160
LORA_COLS = 2 * DECAY_LORA + AAA_LORA + GATE_LORA
LORA_PAD = 512
LORA_SLAB = 128
MIX_A_COLS = 5 * HGRN_WIDTH
MIX_COLS = MIX_A_COLS + 3 * RWKV_WIDTH + LORA_COLS
MIX_COLS_PAD = MIX_A_COLS + 3 * RWKV_WIDTH + LORA_PAD
ATT_DIM = 128
ATT_HEADS = 16
ATT_KV_HEADS = 4
ATT_GROUP = 4
QKV_COLS = (ATT_HEADS + 2 * ATT_KV_HEADS) * ATT_DIM
BLOCK = 128
ROPE_THETA = 10000.0
D_FF = 5632
RMS_EPS = 1e-6
GN_EPS = 64e-5
CHUNK = 64
PAIR = 2 * RWKV_DIM
N_PAIRS = RWKV_WIDTH // PAIR
LOG2_E = 1.4426950408889634
EXP_NEG_HALF = 0.6065306597126334
RWKV_GROUP = 4

VMEM_LIMIT = 56 * 1024 * 1024


def _cparams(sem):
    return pltpu.CompilerParams(dimension_semantics=sem, vmem_limit_bytes=VMEM_LIMIT)


def _mm(a, b):
    return jnp.dot(a.astype(BF16), b.astype(BF16), preferred_element_type=F32)


def _mm_nt(a, b):
    return lax.dot_general(a.astype(BF16), b.astype(BF16), (((1,), (1,)), ((), ())),
                           preferred_element_type=F32)


def _mm_tn(a, b):
    return lax.dot_general(a.astype(BF16), b.astype(BF16), (((0,), (0,)), ((), ())),
                           preferred_element_type=F32)


def _cumsum_rows(x, rev):
    n = x.shape[0]
    row = lax.broadcasted_iota(jnp.int32, (n, 1), 0)
    s = 1
    while s < n:
        if rev:
            x = x + jnp.where(row < n - s, pltpu.roll(x, n - s, axis=0), 0.0)
        else:
            x = x + jnp.where(row >= s, pltpu.roll(x, s, axis=0), 0.0)
        s *= 2
    return x


_mm_lora = _mm
_mm_neumann = _mm


def _sigmoid(x):
    return 1.0 / (1.0 + jnp.exp(-x))


def _norm_matmul_body(x_ref, g_ref, w_ref, o_ref, h_ref):
    @pl.when(pl.program_id(1) == 0)
    def _():
        x = x_ref[...]
        ms = jnp.mean(x * x, axis=-1, keepdims=True)
        h_ref[...] = (x * lax.rsqrt(ms + RMS_EPS) * g_ref[...]).astype(BF16)

    o_ref[...] = jnp.dot(h_ref[...], w_ref[...], preferred_element_type=F32).astype(o_ref.dtype)


def _norm_matmul_rope_body(x_ref, g_ref, w_ref, cos_ref, sin_ref, o_ref, h_ref, *, n_rope, tn):
    x = x_ref[...]
    ms = jnp.mean(x * x, axis=-1, keepdims=True)
    h_ref[...] = (x * lax.rsqrt(ms + RMS_EPS) * g_ref[...]).astype(BF16)
    cos = cos_ref[...]
    sin = sin_ref[...]
    group = 4 * ATT_DIM
    for gi in range(tn // group):
        acc = jnp.dot(h_ref[...], w_ref[:, gi * group:(gi + 1) * group], preferred_element_type=F32)
        for hh in range(group // ATT_DIM):
            a = acc[:, hh * ATT_DIM:(hh + 1) * ATT_DIM]
            head = gi * (group // ATT_DIM) + hh
            if head < n_rope:
                a = a * cos + pltpu.roll(a, ATT_DIM // 2, axis=1) * sin
            o_ref[:, head * ATT_DIM:(head + 1) * ATT_DIM] = a.astype(o_ref.dtype)


def _norm_matmul(x, g, w, *, tm, tn, out_dtype, rope=None):
    M, K = x.shape
    N = w.shape[1]
    grid = (M // tm, N // tn)
    in_specs = [pl.BlockSpec((tm, K), lambda i, j: (i, 0)),
                pl.BlockSpec((1, K), lambda i, j: (0, 0)),
                pl.BlockSpec((K, tn), lambda i, j: (0, j))]
    args = [x, g, w]
    if rope is None:
        body = _norm_matmul_body
    else:
        cos, sin, n_rope = rope
        tiles_per_seq = cos.shape[0] // tm
        in_specs += [pl.BlockSpec((tm, ATT_DIM), lambda i, j: (i % tiles_per_seq, 0)),
                     pl.BlockSpec((tm, ATT_DIM), lambda i, j: (i % tiles_per_seq, 0))]
        args += [cos, sin]
        body = functools.partial(_norm_matmul_rope_body, n_rope=n_rope, tn=tn)
    return pl.pallas_call(
        body,
        out_shape=jax.ShapeDtypeStruct((M, N), out_dtype),
        grid=grid,
        in_specs=in_specs,
        out_specs=pl.BlockSpec((tm, tn), lambda i, j: (i, j)),
        scratch_shapes=[pltpu.VMEM((tm, K), BF16)],
        compiler_params=_cparams(("parallel", "arbitrary")),
        name="norm_matmul" if rope is None else "norm_matmul_rope",
    )(*args)


def _matmul_res_body(*refs, n_parts):
    a_refs = refs[:n_parts]
    w_ref, r_ref, o_ref = refs[n_parts:]
    acc = r_ref[...]
    k0 = 0
    for a_ref in a_refs:
        kw = a_ref.shape[1]
        acc = acc + jnp.dot(a_ref[...], w_ref[k0:k0 + kw, :], preferred_element_type=F32)
        k0 += kw
    o_ref[...] = acc


def _matmul_res(a_parts, w, res, *, tm, tn):
    M, N = res.shape
    K = w.shape[0]
    in_specs = [pl.BlockSpec((tm, a.shape[1]), lambda i, j: (i, 0)) for a in a_parts]
    in_specs += [pl.BlockSpec((K, tn), lambda i, j: (0, j)),
                 pl.BlockSpec((tm, tn), lambda i, j: (i, j))]
    return pl.pallas_call(
        functools.partial(_matmul_res_body, n_parts=len(a_parts)),
        out_shape=jax.ShapeDtypeStruct((M, N), F32),
        grid=(M // tm, N // tn),
        in_specs=in_specs,
        out_specs=pl.BlockSpec((tm, tn), lambda i, j: (i, j)),
        compiler_params=_cparams(("parallel", "arbitrary")),
        name="matmul_res",
    )(*a_parts, w, res)


def _hgrn_body(*refs, rev, n_chunks, final):
    if final:
        q_ref, i_ref, z_ref, lb_ref, ofwd_ref, g_ref, onorm_ref, o_ref, st_ref = refs
    else:
        q_ref, i_ref, z_ref, lb_ref, o_ref, st_ref = refs

    @pl.when(pl.program_id(1) == 0)
    def _():
        st_ref[...] = jnp.zeros_like(st_ref)

    lb = lb_ref[...]
    row = lax.broadcasted_iota(jnp.int32, (CHUNK, CHUNK), 0)
    col = lax.broadcasted_iota(jnp.int32, (CHUNK, CHUNK), 1)
    keep = (row <= col) if rev else (row >= col)
    last = 0 if rev else CHUNK - 1

    head_slices = [slice(h * HGRN_DIM, (h + 1) * HGRN_DIM) for h in range(HGRN_HEADS)]
    sts = [st_ref[h] for h in range(HGRN_HEADS)]
    order = range(n_chunks - 1, -1, -1) if rev else range(n_chunks)
    for ci in order:
        sl = slice(ci * CHUNK, (ci + 1) * CHUNK)
        f = lb + (1.0 - lb) * _sigmoid(z_ref[sl, :])
        kk = 1.0 - f
        b = _cumsum_rows(jnp.log(f), rev)
        b_last = b[last:last + 1, :]
        q_d = q_ref[sl, :] * jnp.exp(b)
        k_d = kk * jnp.exp(-b)
        k_u = kk * jnp.exp(b_last - b)
        dec = jnp.exp(b_last)
        v = i_ref[sl, :]
        att = [jnp.where(keep, _mm_nt(q_d[:, hs], k_d[:, hs]), 0.0) for hs in head_slices]
        outs = [_mm(att[h], v[:, hs]) + _mm_nt(q_d[:, hs], sts[h]) for h, hs in enumerate(head_slices)]
        upd = [_mm_tn(v[:, hs], k_u[:, hs]) for hs in head_slices]
        sts = [sts[h] * dec[:, hs] + upd[h] for h, hs in enumerate(head_slices)]
        if final:
            g = g_ref[sl, :]
            silu_g = g * _sigmoid(g)
            normed = []
            for h, hs in enumerate(head_slices):
                oa = outs[h] + ofwd_ref[sl, hs]
                normed.append(oa * lax.rsqrt(jnp.mean(oa * oa, axis=-1, keepdims=True) + RMS_EPS))
            o_ref[sl, :] = (jnp.concatenate(normed, axis=1) * onorm_ref[...] * silu_g).astype(o_ref.dtype)
        else:
            o_ref[sl, :] = jnp.concatenate(outs, axis=1)
    st_ref[...] = jnp.stack(sts, axis=0)


def _hgrn_dir(proj, lb, *, rev, tb, ofwd=None, onorm=None):
    B, T, _ = proj.shape
    nt = T // tb
    final = ofwd is not None

    def tmap(n):
        return (nt - 1 - n) if rev else n

    def col(c):
        return pl.BlockSpec((None, tb, HGRN_WIDTH), lambda b, n: (b, tmap(n), c))

    vec = pl.BlockSpec((1, HGRN_WIDTH), lambda b, n: (0, 0))
    in_specs = [col(0), col(1), col(3 if rev else 2), vec]
    args = [proj, proj, proj, lb]
    if final:
        in_specs += [col(0), col(4), vec]
        args += [ofwd, proj, onorm]
    return pl.pallas_call(
        functools.partial(_hgrn_body, rev=rev, n_chunks=tb // CHUNK, final=final),
        out_shape=jax.ShapeDtypeStruct((B, T, HGRN_WIDTH), BF16 if final else F32),
        grid=(B, nt),
        in_specs=in_specs,
        out_specs=pl.BlockSpec((None, tb, HGRN_WIDTH), lambda b, n: (b, tmap(n), 0)),
        scratch_shapes=[pltpu.VMEM((HGRN_HEADS, HGRN_DIM, HGRN_DIM), F32)],
        compiler_params=_cparams(("parallel", "arbitrary")),
        name="hgrn_bwd" if rev else "hgrn_fwd",
    )(*args)


def _head_sum(x):
    m0 = lax.broadcasted_iota(jnp.int32, (x.shape[0], PAIR), 1) < RWKV_DIM
    outs = []
    for p in range(N_PAIRS):
        xs = x[:, p * PAIR:(p + 1) * PAIR]
        s0 = jnp.sum(jnp.where(m0, xs, 0.0), axis=-1, keepdims=True)
        s1 = jnp.sum(jnp.where(m0, 0.0, xs), axis=-1, keepdims=True)
        outs.append(jnp.where(m0, s0, s1))
    return jnp.concatenate(outs, axis=1)


def _rwkv_prep_body(r_ref, k_ref, v_ref, l_ref,
                    rp_ref, kp_ref, vp_ref, lp_ref, rn_ref, kn_ref, vn_ref, ln_ref,
                    mu_r_ref, mu_k_ref, mu_v_ref, mu_l_ref,
                    w0_ref, w2f_ref, w2b_ref, a0_ref, a2_ref, g2_ref, kkw_ref, kaw_ref, rk_ref,
                    ro_ref, ko_ref, vo_ref, kko_ref, bvo_ref, lwf_ref, lwb_ref, go_ref, bo_ref,
                    *, tb):
    n = pl.program_id(1)
    nt = pl.num_programs(1)
    has_prev = jnp.where(n > 0, 1.0, 0.0).astype(F32)
    has_next = jnp.where(n < nt - 1, 1.0, 0.0).astype(F32)
    rows8 = lax.broadcasted_iota(jnp.int32, (8, 1), 0)
    first8 = rows8 == 0
    last8 = rows8 == 7

    def shift(x_ref, p_ref, n_ref, mu_ref):
        x = x_ref[...]
        down = pltpu.roll(x, 1, axis=0)
        up = pltpu.roll(x, tb - 1, axis=0)
        prev = jnp.concatenate([jnp.where(first8, p_ref[7:8, :] * has_prev, down[0:8]), down[8:]], axis=0)
        nxt = jnp.concatenate([up[:tb - 8], jnp.where(last8, n_ref[0:1, :] * has_next, up[tb - 8:])], axis=0)
        mu = mu_ref[...]
        return x * (1.0 - mu) + (prev + nxt) * (0.5 * mu)

    r = shift(r_ref, rp_ref, rn_ref, mu_r_ref)
    k = shift(k_ref, kp_ref, kn_ref, mu_k_ref)
    v = shift(v_ref, vp_ref, vn_ref, mu_v_ref)
    lo = shift(l_ref, lp_ref, ln_ref, mu_l_ref)
    wd = jnp.tanh(lo[:, 0:LORA_SLAB])
    ag = lo[:, LORA_SLAB:3 * LORA_SLAB]

    def log_decay(w0, w2):
        u = w0 + _mm_lora(wd, w2)
        return -EXP_NEG_HALF * _sigmoid(u)

    lwf_ref[...] = log_decay(w0_ref[0:1, :], w2f_ref[...])
    lwb_ref[...] = log_decay(w0_ref[1:2, :], w2b_ref[...])
    a = _sigmoid(a0_ref[...] + _mm_lora(ag[:, 0:LORA_SLAB], a2_ref[...]))
    go_ref[...] = _mm_lora(_sigmoid(ag), g2_ref[...]).astype(go_ref.dtype)

    kk = k * kkw_ref[...]
    norm = jnp.maximum(jnp.sqrt(_head_sum(kk * kk)), 1e-12)
    kk = kk / norm
    k2 = k * (1.0 + (a - 1.0) * kaw_ref[...])
    ro_ref[...] = r.astype(ro_ref.dtype)
    ko_ref[...] = k2.astype(ko_ref.dtype)
    vo_ref[...] = v.astype(vo_ref.dtype)
    kko_ref[...] = kk.astype(kko_ref.dtype)
    bvo_ref[...] = (kk * a).astype(bvo_ref.dtype)
    bo_ref[...] = (_head_sum(r * k2 * rk_ref[...]) * v).astype(bo_ref.dtype)


def _rwkv_prep(proj, mu, w0, w2, a0, a2, g2, kkw, kaw, rk, *, tb):
    B, T, _ = proj.shape
    nt = T // tb
    hb = tb // 8
    last8 = T // 8 - 1

    def main(width, c):
        return pl.BlockSpec((None, tb, width), lambda b, n: (b, n, c))

    def prev(width, c):
        return pl.BlockSpec((None, 8, width), lambda b, n: (b, jnp.maximum(n * hb - 1, 0), c))

    def nxt(width, c):
        return pl.BlockSpec((None, 8, width), lambda b, n: (b, jnp.minimum((n + 1) * hb, last8), c))

    def full(a):
        return pl.BlockSpec(a.shape, lambda b, n: (0,) * a.ndim)

    W = RWKV_WIDTH
    c_l = (MIX_A_COLS + 3 * W) // LORA_PAD
    zpad = lambda a, before, total: jnp.pad(a, ((before, total - before - a.shape[0]), (0, 0)))
    params = [mu[:, 0:W], mu[:, W:2 * W], mu[:, 2 * W:3 * W], mu[:, 3 * W:],
              w0, zpad(w2[0], 0, LORA_SLAB), zpad(w2[1], DECAY_LORA, LORA_SLAB), a0,
              zpad(a2, 0, LORA_SLAB), zpad(g2, AAA_LORA, 2 * LORA_SLAB), kkw, kaw, rk]
    in_specs = ([main(W, 5), main(W, 6), main(W, 7), main(LORA_PAD, c_l),
                 prev(W, 5), prev(W, 6), prev(W, 7), prev(LORA_PAD, c_l),
                 nxt(W, 5), nxt(W, 6), nxt(W, 7), nxt(LORA_PAD, c_l)]
                + [full(p) for p in params])
    out_dtypes = [BF16] * 5 + [F32, F32, BF16, BF16]
    return pl.pallas_call(
        functools.partial(_rwkv_prep_body, tb=tb),
        out_shape=[jax.ShapeDtypeStruct((B, T, W), dt) for dt in out_dtypes],
        grid=(B, nt),
        in_specs=in_specs,
        out_specs=[pl.BlockSpec((None, tb, W), lambda b, n: (b, n, 0))] * 9,
        compiler_params=_cparams(("parallel", "parallel")),
        name="rwkv_prep",
    )(*([proj] * 12), *params)


def _bd(x):
    lane = lax.broadcasted_iota(jnp.int32, x.shape, 1) % PAIR
    m0 = lane < RWKV_DIM
    return jnp.concatenate([jnp.where(m0, x, 0.0), jnp.where(m0, 0.0, x)], axis=0)

def _neumann_inverse(mats):
    n, w = mats[0].shape
    eye = jnp.where(lax.broadcasted_iota(jnp.int32, (n, w), 0) == lax.broadcasted_iota(jnp.int32, (n, w), 1) % n,
                    1.0, 0.0).astype(F32)
    ps = [eye + a for a in mats]
    aks = [_mm_neumann(a, _bd(a)) for a in mats]
    for _ in range(4):
        ss = [_mm_neumann(ak, _bd(jnp.concatenate([ak, p], axis=1))) for p, ak in zip(ps, aks)]
        ps = [p + s[:, w:] for p, s in zip(ps, ss)]
        aks = [s[:, :w] for s in ss]
    return [p + _mm_neumann(ak, _bd(p)) for p, ak in zip(ps, aks)]


def _rwkv_scan_body(*refs, rev, final, n_chunks, group):
    h_ref = refs[-1]

    @pl.when(pl.program_id(1) == 0)
    def _():
        h_ref[...] = jnp.zeros_like(h_ref)

    n_groups = n_chunks // group
    if n_groups == 1:
        _rwkv_group(refs, 0, rev=rev, final=final, group=group)
    else:
        def body(i, carry):
            gi = (n_groups - 1 - i) if rev else i
            _rwkv_group(refs, gi * (group * CHUNK), rev=rev, final=final, group=group)
            return carry

        lax.fori_loop(0, n_groups, body, 0)


def _rwkv_group(refs, base, *, rev, final, group):
    if final:
        (r_ref, k_ref, v_ref, kk_ref, bv_ref, lw_ref, yf_ref, g_ref, bonus_ref, lnw_ref, lnb_ref,
         o_ref, h_ref) = refs
    else:
        r_ref, k_ref, v_ref, kk_ref, bv_ref, lw_ref, o_ref, h_ref = refs

    C = CHUNK
    last = 0 if rev else C - 1
    t_row = lax.broadcasted_iota(jnp.int32, (C, 2 * PAIR), 0)
    t_col = lax.broadcasted_iota(jnp.int32, (C, 2 * PAIR), 1) % C
    strict = (t_row < t_col) if rev else (t_row > t_col)
    incl = (t_row <= t_col) if rev else (t_row >= t_col)
    ch_row = lax.broadcasted_iota(jnp.int32, (PAIR, 2 * PAIR), 0)
    ch_col = lax.broadcasted_iota(jnp.int32, (PAIR, 2 * PAIR), 1) % PAIR
    same_head = (ch_row // RWKV_DIM) == (ch_col // RWKV_DIM)
    eye = (lax.broadcasted_iota(jnp.int32, (PAIR, PAIR), 0)
           == lax.broadcasted_iota(jnp.int32, (PAIR, PAIR), 1))
    zeros = jnp.zeros((C, PAIR), F32)
    pairs = range(N_PAIRS)
    cat = jnp.concatenate

    def rows(c):
        if isinstance(base, int):
            return slice(base + c * C, base + (c + 1) * C)
        return pl.ds(pl.multiple_of(base + c * C, C), C)

    def slabs(x):
        return [x[:, p * PAIR:(p + 1) * PAIR] for p in pairs]

    at, rt, bt, kt, vv, bh, kh, w_tot = [], [], [], [], [], [], [], []
    for c in range(group):
        sl = rows(c)
        lw = lw_ref[sl, :]
        c_inc = _cumsum_rows(lw, rev)
        c_exc = c_inc - lw
        c_tot = c_inc[last:last + 1, :]
        e_ninc = jnp.exp(-c_inc)
        e_hat = jnp.exp(c_tot - c_inc)
        kk = kk_ref[sl, :].astype(F32)
        bv = bv_ref[sl, :].astype(F32)
        k2 = k_ref[sl, :].astype(F32)
        w_tot += slabs(jnp.exp(c_tot))
        at += slabs(-kk * jnp.exp(c_exc))
        rt += slabs(r_ref[sl, :].astype(F32) * jnp.exp(c_inc))
        bt += slabs(bv * e_ninc)
        kt += slabs(k2 * e_ninc)
        vv += slabs(v_ref[sl, :].astype(F32))
        bh += slabs(bv * e_hat)
        kh += slabs(k2 * e_hat)

    items = range(group * N_PAIRS)
    pm = [_mm_nt(cat([at[i], rt[i]], axis=0), cat([_bd(bt[i]), _bd(kt[i])], axis=0)) for i in items]
    a_abk = [jnp.where(strict, pm[i][:C, :], 0.0) for i in items]
    a_rbk = [jnp.where(incl, pm[i][C:, :], 0.0) for i in items]
    tinv = _neumann_inverse([a_abk[i][:, :PAIR] for i in items])
    x1 = [_mm(a_abk[i][:, PAIR:], _bd(vv[i])) for i in items]
    z = [_mm(tinv[i], _bd(cat([at[i], x1[i]], axis=1))) for i in items]
    w2 = [cat([z[i], cat([zeros, vv[i]], axis=1)], axis=0) for i in items]
    mg = [jnp.where(same_head, _mm_tn(cat([bh[i], kh[i]], axis=0), w2[i]), 0.0) for i in items]
    ry = [_mm(a_rbk[i], cat([_bd(z[i]), _bd(w2[i][C:])], axis=0)) for i in items]
    lhs = [cat([rt[i] + ry[i][:, :PAIR], mg[i][:, :PAIR] + jnp.where(eye, w_tot[i], 0.0)], axis=0)
           for i in items]

    hs = [h_ref[p] for p in pairs]
    for c in (range(group - 1, -1, -1) if rev else range(group)):
        sl = rows(c)
        yh = [_mm(lhs[c * N_PAIRS + p], hs[p]) for p in pairs]
        hs = [yh[p][C:] + mg[c * N_PAIRS + p][:, PAIR:] for p in pairs]
        y = cat([yh[p][:C] + ry[c * N_PAIRS + p][:, PAIR:] for p in pairs], axis=1)
        if final:
            y = y + yf_ref[sl, :]
            mean = _head_sum(y) * (1.0 / RWKV_DIM)
            yc = y - mean
            var = _head_sum(yc * yc) * (1.0 / RWKV_DIM)
            yn = yc * lax.rsqrt(var + GN_EPS) * lnw_ref[...] + lnb_ref[...]
            o_ref[sl, :] = ((yn + bonus_ref[sl, :].astype(F32)) * g_ref[sl, :].astype(F32)).astype(o_ref.dtype)
        else:
            o_ref[sl, :] = y
    h_ref[...] = jnp.stack(hs, axis=0)


def _rwkv_dir(r, k2, v, kk, bv, lw, *, rev, tb, fin=None):
    B, T, W = r.shape
    nc = T // tb

    def tmap(n):
        return (nc - 1 - n) if rev else n

    blk = pl.BlockSpec((None, tb, W), lambda b, n: (b, tmap(n), 0))
    vec = pl.BlockSpec((1, W), lambda b, n: (0, 0))
    in_specs = [blk] * 6
    args = [r, k2, v, kk, bv, lw]
    if fin is not None:
        in_specs += [blk, blk, blk, vec, vec]
        args += list(fin)
    return pl.pallas_call(
        functools.partial(_rwkv_scan_body, rev=rev, final=fin is not None, n_chunks=tb // CHUNK,
                          group=RWKV_GROUP),
        out_shape=jax.ShapeDtypeStruct((B, T, W), BF16 if fin is not None else F32),
        grid=(B, nc),
        in_specs=in_specs,
        out_specs=blk,
        scratch_shapes=[pltpu.VMEM((N_PAIRS, PAIR, PAIR), F32)],
        compiler_params=_cparams(("parallel", "arbitrary")),
        name="rwkv_bwd" if rev else "rwkv_fwd",
    )(*args)


def _attn_body(sink_ref, q_ref, kp_ref, kc_ref, kn_ref, vp_ref, vc_ref, vn_ref, o_ref):
    n = pl.program_id(1)
    nb = pl.num_programs(1)
    r = lax.broadcasted_iota(jnp.int32, (BLOCK, 3 * BLOCK), 0)
    c = lax.broadcasted_iota(jnp.int32, (BLOCK, 3 * BLOCK), 1)
    d = c - BLOCK - r
    lo = jnp.where(n > 0, 0, BLOCK)
    hi = jnp.where(n < nb - 1, 3 * BLOCK, 2 * BLOCK)
    valid = (d >= -BLOCK) & (d <= BLOCK) & (c >= lo) & (c < hi)
    scale = ATT_DIM ** -0.5
    for kh in range(ATT_KV_HEADS):
        ks = slice(kh * ATT_DIM, (kh + 1) * ATT_DIM)
        kw = jnp.concatenate([kp_ref[:, ks], kc_ref[:, ks], kn_ref[:, ks]], axis=0)
        vw = jnp.concatenate([vp_ref[:, ks], vc_ref[:, ks], vn_ref[:, ks]], axis=0)
        heads = [kh * ATT_GROUP + g for g in range(ATT_GROUP)]
        qg = jnp.concatenate([q_ref[:, h * ATT_DIM:(h + 1) * ATT_DIM] for h in heads], axis=0)
        s_all = lax.dot_general(qg, kw, (((1,), (1,)), ((), ())), preferred_element_type=F32)
        ps, inv_dens = [], []
        for g, h in enumerate(heads):
            s = jnp.where(valid, s_all[g * BLOCK:(g + 1) * BLOCK, :], -jnp.inf)
            sk = sink_ref[h] * (1.0 / scale)
            m = jnp.maximum(jnp.max(s, axis=-1, keepdims=True), sk)
            e = jnp.exp2((s - m) * (scale * LOG2_E))
            inv_dens.append(1.0 / (jnp.sum(e, axis=-1, keepdims=True) + jnp.exp2((sk - m) * (scale * LOG2_E))))
            ps.append(e.astype(BF16))
        o_all = jnp.dot(jnp.concatenate(ps, axis=0), vw, preferred_element_type=F32)
        for g, h in enumerate(heads):
            o_ref[:, h * ATT_DIM:(h + 1) * ATT_DIM] = (
                o_all[g * BLOCK:(g + 1) * BLOCK, :] * inv_dens[g]).astype(o_ref.dtype)


def _attention(qkv, sink):
    B, T, _ = qkv.shape
    nb = T // BLOCK
    kvw = ATT_KV_HEADS * ATT_DIM
    kc = (ATT_HEADS * ATT_DIM) // kvw
    vc = kc + 1

    def blk(cidx, off):
        def imap(b, n):
            return (b, jnp.clip(n + off, 0, nb - 1), cidx)
        return pl.BlockSpec((None, BLOCK, kvw), imap)

    return pl.pallas_call(
        _attn_body,
        out_shape=jax.ShapeDtypeStruct((B, T, ATT_HEADS * ATT_DIM), BF16),
        grid=(B, nb),
        in_specs=[pl.BlockSpec(memory_space=pltpu.SMEM),
                  pl.BlockSpec((None, BLOCK, ATT_HEADS * ATT_DIM), lambda b, n: (b, n, 0)),
                  blk(kc, -1), blk(kc, 0), blk(kc, 1), blk(vc, -1), blk(vc, 0), blk(vc, 1)],
        out_specs=pl.BlockSpec((None, BLOCK, ATT_HEADS * ATT_DIM), lambda b, n: (b, n, 0)),
        compiler_params=_cparams(("parallel", "parallel")),
        name="attention",
    )(sink, qkv, qkv, qkv, qkv, qkv, qkv, qkv)


def _ffn_body(x_ref, xp_ref, xn_ref, g_ref, wg_ref, wv_ref, cw_ref, cb_ref, wd_ref, fg_ref, o_ref, h_ref,
              *, tm, tiles_per_seq, final_norm):
    i = pl.program_id(0)
    f = pl.program_id(1)
    nf = pl.num_programs(1)

    @pl.when(f == 0)
    def _():
        def norm(x):
            ms = jnp.mean(x * x, axis=-1, keepdims=True)
            return (x * lax.rsqrt(ms + RMS_EPS) * g_ref[...]).astype(BF16)
        x = x_ref[...]
        h_ref[0:tm, :] = norm(x)
        h_ref[tm:tm + 16, :] = norm(jnp.concatenate([xp_ref[...], xn_ref[...]], axis=0))
        o_ref[...] = x

    t = i % tiles_per_seq
    has_prev = jnp.where(t > 0, 1.0, 0.0).astype(F32)
    has_next = jnp.where(t < tiles_per_seq - 1, 1.0, 0.0).astype(F32)
    ge = jnp.dot(h_ref[...], wg_ref[...], preferred_element_type=F32)
    gm = ge[0:tm, :]
    rows = lax.broadcasted_iota(jnp.int32, (tm, 1), 0)
    g_prev = jnp.where(rows == 0, ge[tm + 7:tm + 8, :] * has_prev, pltpu.roll(gm, 1, axis=0))
    g_next = jnp.where(rows == tm - 1, ge[tm + 8:tm + 9, :] * has_next, pltpu.roll(gm, tm - 1, axis=0))
    gate = g_prev * cw_ref[0:1, :] + gm * cw_ref[1:2, :] + g_next * cw_ref[2:3, :] + cb_ref[...]
    val = jnp.dot(h_ref[0:tm, :], wv_ref[...], preferred_element_type=F32)
    act = (gate * _sigmoid(gate) * val).astype(BF16)
    o_ref[...] += jnp.dot(act, wd_ref[...], preferred_element_type=F32)

    if final_norm:
        @pl.when(f == nf - 1)
        def _():
            y = o_ref[...]
            ms = jnp.mean(y * y, axis=-1, keepdims=True)
            o_ref[...] = y * lax.rsqrt(ms + RMS_EPS) * fg_ref[...]


def _ffn(x, g, w_up, conv_w, conv_b, w_down, final_g, *, layer, seq_len, tm, tf, final_norm):
    M, D = x.shape
    nf = D_FF // tf
    hb = tm // 8
    last8 = M // 8 - 1
    in_specs = [
        pl.BlockSpec((tm, D), lambda i, f: (i, 0)),
        pl.BlockSpec((8, D), lambda i, f: (jnp.maximum(i * hb - 1, 0), 0)),
        pl.BlockSpec((8, D), lambda i, f: (jnp.minimum((i + 1) * hb, last8), 0)),
        pl.BlockSpec((1, D), lambda i, f: (0, 0)),
        pl.BlockSpec((None, D, tf), lambda i, f: (layer, 0, f)),
        pl.BlockSpec((None, D, tf), lambda i, f: (layer, 0, nf + f)),
        pl.BlockSpec((3, tf), lambda i, f: (0, f)),
        pl.BlockSpec((1, tf), lambda i, f: (0, f)),
        pl.BlockSpec((None, tf, D), lambda i, f: (layer, f, 0)),
        pl.BlockSpec((1, D), lambda i, f: (0, 0)),
    ]
    return pl.pallas_call(
        functools.partial(_ffn_body, tm=tm, tiles_per_seq=seq_len // tm, final_norm=final_norm),
        out_shape=jax.ShapeDtypeStruct((M, D), F32),
        grid=(M // tm, nf),
        in_specs=in_specs,
        out_specs=pl.BlockSpec((tm, D), lambda i, f: (i, 0)),
        scratch_shapes=[pltpu.VMEM((tm + 16, D), BF16)],
        compiler_params=_cparams(("parallel", "arbitrary")),
        name="conv_ffn",
    )(x, x, x, g, w_up, w_up, conv_w, conv_b, w_down, final_g)


def _pick(n, prefs):
    for p in prefs:
        if n % p == 0:
            return p
    raise ValueError(f"no tile for {n}")


def _rope_tables(T):
    half = ATT_DIM // 2
    inv = ROPE_THETA ** (-jnp.arange(half, dtype=F32) / half)
    ang = jnp.arange(T, dtype=F32)[:, None] * inv[None, :]
    cos = jnp.cos(ang)
    sin = jnp.sin(ang)
    return jnp.concatenate([cos, cos], axis=1), jnp.concatenate([-sin, sin], axis=1)


def _prepare_params(p):
    q = dict(p)
    w_in = p['ab_w_in'][0]
    q['w_in'] = jnp.pad(w_in, ((0, 0), (0, MIX_COLS_PAD - MIX_COLS))).astype(BF16)
    mu = p['rwkv_mu'][0]
    q['mu'] = jnp.pad(mu, (0, LORA_PAD - LORA_COLS))[None, :]
    q['lb'] = jnp.cumsum(jax.nn.softmax(p['hgrn_lb'].astype(F32), axis=0), axis=0)
    q['w_out'] = p['ab_w_out'][0].astype(BF16)
    q['w_qkv'] = p['att_w_qkv'][0].astype(BF16)
    q['w_o'] = p['att_w_o'][0].astype(BF16)
    q['w_up'] = p['ffn_w_up'].astype(BF16)
    q['w_down'] = p['ffn_w_down'].astype(BF16)
    return q


def _mixer_layer(x2, q, B, T, layer):
    M = B * T
    tm = _pick(M, (512, 256, 128))
    proj = _norm_matmul(x2, q['mix_norm'][layer][None, :], q['w_in'], tm=tm, tn=MIX_COLS_PAD // 4,
                        out_dtype=F32)
    proj = proj.reshape(B, T, MIX_COLS_PAD)
    lb = q['lb'][layer][None, :]
    tb = _pick(T, (256, 128, 64))
    tb_hgrn = _pick(T, (512, 256, 128, 64))
    o_fwd = _hgrn_dir(proj, lb, rev=False, tb=tb_hgrn)
    ya = _hgrn_dir(proj, lb, rev=True, tb=tb_hgrn, ofwd=o_fwd, onorm=q['hgrn_onorm'][0][None, :])

    r, k2, v, kk, bv, lwf, lwb, g, bonus = _rwkv_prep(
        proj, q['mu'], q['rwkv_w0'][0], q['rwkv_w2'][0], q['rwkv_a0'][0][None, :], q['rwkv_a2'][0],
        q['rwkv_g2'][0], q['rwkv_kk'][0][None, :], q['rwkv_ka'][0][None, :],
        q['rwkv_rk'][0].reshape(1, RWKV_WIDTH), tb=tb)
    y_fwd = _rwkv_dir(r, k2, v, kk, bv, lwf, rev=False, tb=tb)
    yb = _rwkv_dir(r, k2, v, kk, bv, lwb, rev=True, tb=tb,
                   fin=(y_fwd, g, bonus, q['rwkv_ln_w'][0][None, :], q['rwkv_ln_b'][0][None, :]))
    return _matmul_res([ya.reshape(M, HGRN_WIDTH), yb.reshape(M, RWKV_WIDTH)], q['w_out'], x2,
                       tm=tm, tn=D_MODEL)


def _attention_layer(x2, q, B, T, layer, rope_tabs):
    M = B * T
    tm = _pick(M, (512, 256, 128))
    cos, sin = rope_tabs
    qkv = _norm_matmul(x2, q['mix_norm'][layer][None, :], q['w_qkv'], tm=tm, tn=QKV_COLS, out_dtype=BF16,
                       rope=(cos, sin, ATT_HEADS + ATT_KV_HEADS))
    o = _attention(qkv.reshape(B, T, QKV_COLS), q['att_sink'][0])
    return _matmul_res([o.reshape(M, ATT_HEADS * ATT_DIM)], q['w_o'], x2, tm=tm, tn=D_MODEL)


def _trunk(x, q):
    B, T, D = x.shape
    M = B * T
    x2 = x.reshape(M, D)
    rope_tabs = _rope_tables(T)
    tm = _pick(T, (1024, 512, 256, 128))
    for layer in range(DEPTH):
        if layer % 2 == 0:
            x2 = _mixer_layer(x2, q, B, T, layer)
        else:
            x2 = _attention_layer(x2, q, B, T, layer, rope_tabs)
        x2 = _ffn(x2, q['ffn_norm'][layer][None, :], q['w_up'], q['ffn_conv_w'][layer],
                  q['ffn_conv_b'][layer][None, :], q['w_down'], q['final_norm'][None, :],
                  layer=layer, seq_len=T, tm=tm, tf=512, final_norm=(layer == DEPTH - 1))
    return x2.reshape(B, T, D)


def kernel(x_prompt, x_sample, mix_norm, ab_w_in, hgrn_lb, hgrn_onorm, rwkv_mu, rwkv_w0, rwkv_w2, rwkv_a0, rwkv_a2, rwkv_g2, rwkv_kk, rwkv_ka, rwkv_rk, rwkv_ln_w, rwkv_ln_b, ab_w_out, att_w_qkv, att_sink, att_w_o, ffn_norm, ffn_w_up, ffn_conv_w, ffn_conv_b, ffn_w_down, final_norm):
    p = {
        'mix_norm': mix_norm, 'ab_w_in': ab_w_in, 'hgrn_lb': hgrn_lb, 'hgrn_onorm': hgrn_onorm,
        'rwkv_mu': rwkv_mu, 'rwkv_w0': rwkv_w0, 'rwkv_w2': rwkv_w2, 'rwkv_a0': rwkv_a0, 'rwkv_a2': rwkv_a2,
        'rwkv_g2': rwkv_g2, 'rwkv_kk': rwkv_kk, 'rwkv_ka': rwkv_ka, 'rwkv_rk': rwkv_rk,
        'rwkv_ln_w': rwkv_ln_w, 'rwkv_ln_b': rwkv_ln_b, 'ab_w_out': ab_w_out,
        'att_w_qkv': att_w_qkv, 'att_sink': att_sink, 'att_w_o': att_w_o,
        'ffn_norm': ffn_norm, 'ffn_w_up': ffn_w_up, 'ffn_conv_w': ffn_conv_w, 'ffn_conv_b': ffn_conv_b,
        'ffn_w_down': ffn_w_down, 'final_norm': final_norm,
    }
    q = _prepare_params(p)
    return (_trunk(x_prompt, q), _trunk(x_sample, q))
```

```python
import functools

import jax
import jax.numpy as jnp
from jax import lax
from jax.experimental import pallas as pl
from jax.experimental.pallas import tpu as pltpu

F32 = jnp.float32
BF16 = jnp.bfloat16

D_MODEL = 2048
DEPTH = 2
HGRN_DIM = 128
HGRN_HEADS = 8
HGRN_WIDTH = 1024
RWKV_DIM = 64
RWKV_HEADS = 16
RWKV_WIDTH = 1024
DECAY_LORA = 64
AAA_LORA = 64
GATE_LORA = 160
LORA_COLS = 2 * DECAY_LORA + AAA_LORA + GATE_LORA
LORA_PAD = 512
LORA_SLAB = 128
MIX_A_COLS = 5 * HGRN_WIDTH
MIX_COLS = MIX_A_COLS + 3 * RWKV_WIDTH + LORA_COLS
MIX_COLS_PAD = MIX_A_COLS + 3 * RWKV_WIDTH + LORA_PAD
ATT_DIM = 128
ATT_HEADS = 16
ATT_KV_HEADS = 4
ATT_GROUP = 4
QKV_COLS = (ATT_HEADS + 2 * ATT_KV_HEADS) * ATT_DIM
BLOCK = 128
ROPE_THETA = 10000.0
D_FF = 5632
RMS_EPS = 1e-6
GN_EPS = 64e-5
CHUNK = 64
PAIR = 2 * RWKV_DIM
N_PAIRS = RWKV_WIDTH // PAIR
LOG2_E = 1.4426950408889634
EXP_NEG_HALF = 0.6065306597126334
RWKV_GROUP = 4

VMEM_LIMIT = 56 * 1024 * 1024


def _cparams(sem):
    return pltpu.CompilerParams(dimension_semantics=sem, vmem_limit_bytes=VMEM_LIMIT)


def _mm(a, b):
    return jnp.dot(a.astype(BF16), b.astype(BF16), preferred_element_type=F32)


def _mm_nt(a, b):
    return lax.dot_general(a.astype(BF16), b.astype(BF16), (((1,), (1,)), ((), ())),
                           preferred_element_type=F32)


def _mm_tn(a, b):
    return lax.dot_general(a.astype(BF16), b.astype(BF16), (((0,), (0,)), ((), ())),
                           preferred_element_type=F32)


def _cumsum_rows(x, rev):
    n = x.shape[0]
    row = lax.broadcasted_iota(jnp.int32, (n, 1), 0)
    s = 1
    while s < n:
        if rev:
            x = x + jnp.where(row < n - s, pltpu.roll(x, n - s, axis=0), 0.0)
        else:
            x = x + jnp.where(row >= s, pltpu.roll(x, s, axis=0), 0.0)
        s *= 2
    return x


_mm_lora = _mm
_mm_neumann = _mm


def _sigmoid(x):
    return 1.0 / (1.0 + jnp.exp(-x))


def _norm_matmul_body(x_ref, g_ref, w_ref, o_ref, h_ref):
    @pl.when(pl.program_id(1) == 0)
    def _():
        x = x_ref[...]
        ms = jnp.mean(x * x, axis=-1, keepdims=True)
        h_ref[...] = (x * lax.rsqrt(ms + RMS_EPS) * g_ref[...]).astype(BF16)

    o_ref[...] = jnp.dot(h_ref[...], w_ref[...], preferred_element_type=F32).astype(o_ref.dtype)


def _norm_matmul_rope_body(x_ref, g_ref, w_ref, cos_ref, sin_ref, o_ref, h_ref, *, n_rope, tn):
    x = x_ref[...]
    ms = jnp.mean(x * x, axis=-1, keepdims=True)
    h_ref[...] = (x * lax.rsqrt(ms + RMS_EPS) * g_ref[...]).astype(BF16)
    cos = cos_ref[...]
    sin = sin_ref[...]
    group = 4 * ATT_DIM
    for gi in range(tn // group):
        acc = jnp.dot(h_ref[...], w_ref[:, gi * group:(gi + 1) * group], preferred_element_type=F32)
        for hh in range(group // ATT_DIM):
            a = acc[:, hh * ATT_DIM:(hh + 1) * ATT_DIM]
            head = gi * (group // ATT_DIM) + hh
            if head < n_rope:
                a = a * cos + pltpu.roll(a, ATT_DIM // 2, axis=1) * sin
            o_ref[:, head * ATT_DIM:(head + 1) * ATT_DIM] = a.astype(o_ref.dtype)


def _norm_matmul(x, g, w, *, tm, tn, out_dtype, rope=None):
    M, K = x.shape
    N = w.shape[1]
    grid = (M // tm, N // tn)
    in_specs = [pl.BlockSpec((tm, K), lambda i, j: (i, 0)),
                pl.BlockSpec((1, K), lambda i, j: (0, 0)),
                pl.BlockSpec((K, tn), lambda i, j: (0, j))]
    args = [x, g, w]
    if rope is None:
        body = _norm_matmul_body
    else:
        cos, sin, n_rope = rope
        tiles_per_seq = cos.shape[0] // tm
        in_specs += [pl.BlockSpec((tm, ATT_DIM), lambda i, j: (i % tiles_per_seq, 0)),
                     pl.BlockSpec((tm, ATT_DIM), lambda i, j: (i % tiles_per_seq, 0))]
        args += [cos, sin]
        body = functools.partial(_norm_matmul_rope_body, n_rope=n_rope, tn=tn)
    return pl.pallas_call(
        body,
        out_shape=jax.ShapeDtypeStruct((M, N), out_dtype),
        grid=grid,
        in_specs=in_specs,
        out_specs=pl.BlockSpec((tm, tn), lambda i, j: (i, j)),
        scratch_shapes=[pltpu.VMEM((tm, K), BF16)],
        compiler_params=_cparams(("parallel", "arbitrary")),
        name="norm_matmul" if rope is None else "norm_matmul_rope",
    )(*args)


def _matmul_res_body(*refs, n_parts):
    a_refs = refs[:n_parts]
    w_ref, r_ref, o_ref = refs[n_parts:]
    acc = r_ref[...]
    k0 = 0
    for a_ref in a_refs:
        kw = a_ref.shape[1]
        acc = acc + jnp.dot(a_ref[...], w_ref[k0:k0 + kw, :], preferred_element_type=F32)
        k0 += kw
    o_ref[...] = acc


def _matmul_res(a_parts, w, res, *, tm, tn):
    M, N = res.shape
    K = w.shape[0]
    in_specs = [pl.BlockSpec((tm, a.shape[1]), lambda i, j: (i, 0)) for a in a_parts]
    in_specs += [pl.BlockSpec((K, tn), lambda i, j: (0, j)),
                 pl.BlockSpec((tm, tn), lambda i, j: (i, j))]
    return pl.pallas_call(
        functools.partial(_matmul_res_body, n_parts=len(a_parts)),
        out_shape=jax.ShapeDtypeStruct((M, N), F32),
        grid=(M // tm, N // tn),
        in_specs=in_specs,
        out_specs=pl.BlockSpec((tm, tn), lambda i, j: (i, j)),
        compiler_params=_cparams(("parallel", "arbitrary")),
        name="matmul_res",
    )(*a_parts, w, res)


def _hgrn_body(*refs, rev, n_chunks, final):
    if final:
        q_ref, i_ref, z_ref, lb_ref, ofwd_ref, g_ref, onorm_ref, o_ref, st_ref = refs
    else:
        q_ref, i_ref, z_ref, lb_ref, o_ref, st_ref = refs

    @pl.when(pl.program_id(1) == 0)
    def _():
        st_ref[...] = jnp.zeros_like(st_ref)

    lb = lb_ref[...]
    row = lax.broadcasted_iota(jnp.int32, (CHUNK, CHUNK), 0)
    col = lax.broadcasted_iota(jnp.int32, (CHUNK, CHUNK), 1)
    keep = (row <= col) if rev else (row >= col)
    last = 0 if rev else CHUNK - 1

    head_slices = [slice(h * HGRN_DIM, (h + 1) * HGRN_DIM) for h in range(HGRN_HEADS)]
    sts = [st_ref[h] for h in range(HGRN_HEADS)]
    order = range(n_chunks - 1, -1, -1) if rev else range(n_chunks)
    for ci in order:
        sl = slice(ci * CHUNK, (ci + 1) * CHUNK)
        f = lb + (1.0 - lb) * _sigmoid(z_ref[sl, :])
        kk = 1.0 - f
        b = _cumsum_rows(jnp.log(f), rev)
        b_last = b[last:last + 1, :]
        q_d = q_ref[sl, :] * jnp.exp(b)
        k_d = kk * jnp.exp(-b)
        k_u = kk * jnp.exp(b_last - b)
        dec = jnp.exp(b_last)
        v = i_ref[sl, :]
        att = [jnp.where(keep, _mm_nt(q_d[:, hs], k_d[:, hs]), 0.0) for hs in head_slices]
        outs = [_mm(att[h], v[:, hs]) + _mm_nt(q_d[:, hs], sts[h]) for h, hs in enumerate(head_slices)]
        upd = [_mm_tn(v[:, hs], k_u[:, hs]) for hs in head_slices]
        sts = [sts[h] * dec[:, hs] + upd[h] for h, hs in enumerate(head_slices)]
        if final:
            g = g_ref[sl, :]
            silu_g = g * _sigmoid(g)
            normed = []
            for h, hs in enumerate(head_slices):
                oa = outs[h] + ofwd_ref[sl, hs]
                normed.append(oa * lax.rsqrt(jnp.mean(oa * oa, axis=-1, keepdims=True) + RMS_EPS))
            o_ref[sl, :] = (jnp.concatenate(normed, axis=1) * onorm_ref[...] * silu_g).astype(o_ref.dtype)
        else:
            o_ref[sl, :] = jnp.concatenate(outs, axis=1)
    st_ref[...] = jnp.stack(sts, axis=0)


def _hgrn_dir(proj, lb, *, rev, tb, ofwd=None, onorm=None):
    B, T, _ = proj.shape
    nt = T // tb
    final = ofwd is not None

    def tmap(n):
        return (nt - 1 - n) if rev else n

    def col(c):
        return pl.BlockSpec((None, tb, HGRN_WIDTH), lambda b, n: (b, tmap(n), c))

    vec = pl.BlockSpec((1, HGRN_WIDTH), lambda b, n: (0, 0))
    in_specs = [col(0), col(1), col(3 if rev else 2), vec]
    args = [proj, proj, proj, lb]
    if final:
        in_specs += [col(0), col(4), vec]
        args += [ofwd, proj, onorm]
    return pl.pallas_call(
        functools.partial(_hgrn_body, rev=rev, n_chunks=tb // CHUNK, final=final),
        out_shape=jax.ShapeDtypeStruct((B, T, HGRN_WIDTH), BF16 if final else F32),
        grid=(B, nt),
        in_specs=in_specs,
        out_specs=pl.BlockSpec((None, tb, HGRN_WIDTH), lambda b, n: (b, tmap(n), 0)),
        scratch_shapes=[pltpu.VMEM((HGRN_HEADS, HGRN_DIM, HGRN_DIM), F32)],
        compiler_params=_cparams(("parallel", "arbitrary")),
        name="hgrn_bwd" if rev else "hgrn_fwd",
    )(*args)


def _head_sum(x):
    m0 = lax.broadcasted_iota(jnp.int32, (x.shape[0], PAIR), 1) < RWKV_DIM
    outs = []
    for p in range(N_PAIRS):
        xs = x[:, p * PAIR:(p + 1) * PAIR]
        s0 = jnp.sum(jnp.where(m0, xs, 0.0), axis=-1, keepdims=True)
        s1 = jnp.sum(jnp.where(m0, 0.0, xs), axis=-1, keepdims=True)
        outs.append(jnp.where(m0, s0, s1))
    return jnp.concatenate(outs, axis=1)


def _rwkv_prep_body(r_ref, k_ref, v_ref, l_ref,
                    rp_ref, kp_ref, vp_ref, lp_ref, rn_ref, kn_ref, vn_ref, ln_ref,
                    mu_r_ref, mu_k_ref, mu_v_ref, mu_l_ref,
                    w0_ref, w2f_ref, w2b_ref, a0_ref, a2_ref, g2_ref, kkw_ref, kaw_ref, rk_ref,
                    ro_ref, ko_ref, vo_ref, kko_ref, bvo_ref, lwf_ref, lwb_ref, go_ref, bo_ref,
                    *, tb):
    n = pl.program_id(1)
    nt = pl.num_programs(1)
    has_prev = jnp.where(n > 0, 1.0, 0.0).astype(F32)
    has_next = jnp.where(n < nt - 1, 1.0, 0.0).astype(F32)
    rows8 = lax.broadcasted_iota(jnp.int32, (8, 1), 0)
    first8 = rows8 == 0
    last8 = rows8 == 7

    def shift(x_ref, p_ref, n_ref, mu_ref):
        x = x_ref[...]
        down = pltpu.roll(x, 1, axis=0)
        up = pltpu.roll(x, tb - 1, axis=0)
        prev = jnp.concatenate([jnp.where(first8, p_ref[7:8, :] * has_prev, down[0:8]), down[8:]], axis=0)
        nxt = jnp.concatenate([up[:tb - 8], jnp.where(last8, n_ref[0:1, :] * has_next, up[tb - 8:])], axis=0)
        mu = mu_ref[...]
        return x * (1.0 - mu) + (prev + nxt) * (0.5 * mu)

    r = shift(r_ref, rp_ref, rn_ref, mu_r_ref)
    k = shift(k_ref, kp_ref, kn_ref, mu_k_ref)
    v = shift(v_ref, vp_ref, vn_ref, mu_v_ref)
    lo = shift(l_ref, lp_ref, ln_ref, mu_l_ref)
    wd = jnp.tanh(lo[:, 0:LORA_SLAB])
    ag = lo[:, LORA_SLAB:3 * LORA_SLAB]

    def log_decay(w0, w2):
        u = w0 + _mm_lora(wd, w2)
        return -EXP_NEG_HALF * _sigmoid(u)

    lwf_ref[...] = log_decay(w0_ref[0:1, :], w2f_ref[...])
    lwb_ref[...] = log_decay(w0_ref[1:2, :], w2b_ref[...])
    a = _sigmoid(a0_ref[...] + _mm_lora(ag[:, 0:LORA_SLAB], a2_ref[...]))
    go_ref[...] = _mm_lora(_sigmoid(ag), g2_ref[...]).astype(go_ref.dtype)

    kk = k * kkw_ref[...]
    norm = jnp.maximum(jnp.sqrt(_head_sum(kk * kk)), 1e-12)
    kk = kk / norm
    k2 = k * (1.0 + (a - 1.0) * kaw_ref[...])
    ro_ref[...] = r.astype(ro_ref.dtype)
    ko_ref[...] = k2.astype(ko_ref.dtype)
    vo_ref[...] = v.astype(vo_ref.dtype)
    kko_ref[...] = kk.astype(kko_ref.dtype)
    bvo_ref[...] = (kk * a).astype(bvo_ref.dtype)
    bo_ref[...] = (_head_sum(r * k2 * rk_ref[...]) * v).astype(bo_ref.dtype)


def _rwkv_prep(proj, mu, w0, w2, a0, a2, g2, kkw, kaw, rk, *, tb):
    B, T, _ = proj.shape
    nt = T // tb
    hb = tb // 8
    last8 = T // 8 - 1

    def main(width, c):
        return pl.BlockSpec((None, tb, width), lambda b, n: (b, n, c))

    def prev(width, c):
        return pl.BlockSpec((None, 8, width), lambda b, n: (b, jnp.maximum(n * hb - 1, 0), c))

    def nxt(width, c):
        return pl.BlockSpec((None, 8, width), lambda b, n: (b, jnp.minimum((n + 1) * hb, last8), c))

    def full(a):
        return pl.BlockSpec(a.shape, lambda b, n: (0,) * a.ndim)

    W = RWKV_WIDTH
    c_l = (MIX_A_COLS + 3 * W) // LORA_PAD
    zpad = lambda a, before, total: jnp.pad(a, ((before, total - before - a.shape[0]), (0, 0)))
    params = [mu[:, 0:W], mu[:, W:2 * W], mu[:, 2 * W:3 * W], mu[:, 3 * W:],
              w0, zpad(w2[0], 0, LORA_SLAB), zpad(w2[1], DECAY_LORA, LORA_SLAB), a0,
              zpad(a2, 0, LORA_SLAB), zpad(g2, AAA_LORA, 2 * LORA_SLAB), kkw, kaw, rk]
    in_specs = ([main(W, 5), main(W, 6), main(W, 7), main(LORA_PAD, c_l),
                 prev(W, 5), prev(W, 6), prev(W, 7), prev(LORA_PAD, c_l),
                 nxt(W, 5), nxt(W, 6), nxt(W, 7), nxt(LORA_PAD, c_l)]
                + [full(p) for p in params])
    out_dtypes = [BF16] * 5 + [F32, F32, BF16, BF16]
    return pl.pallas_call(
        functools.partial(_rwkv_prep_body, tb=tb),
        out_shape=[jax.ShapeDtypeStruct((B, T, W), dt) for dt in out_dtypes],
        grid=(B, nt),
        in_specs=in_specs,
        out_specs=[pl.BlockSpec((None, tb, W), lambda b, n: (b, n, 0))] * 9,
        compiler_params=_cparams(("parallel", "parallel")),
        name="rwkv_prep",
    )(*([proj] * 12), *params)


def _bd(x):
    lane = lax.broadcasted_iota(jnp.int32, x.shape, 1) % PAIR
    m0 = lane < RWKV_DIM
    return jnp.concatenate([jnp.where(m0, x, 0.0), jnp.where(m0, 0.0, x)], axis=0)

def _neumann_inverse(mats):
    n, w = mats[0].shape
    eye = jnp.where(lax.broadcasted_iota(jnp.int32, (n, w), 0) == lax.broadcasted_iota(jnp.int32, (n, w), 1) % n,
                    1.0, 0.0).astype(F32)
    ps = [eye + a for a in mats]
    aks = [_mm_neumann(a, _bd(a)) for a in mats]
    for _ in range(4):
        ss = [_mm_neumann(ak, _bd(jnp.concatenate([ak, p], axis=1))) for p, ak in zip(ps, aks)]
        ps = [p + s[:, w:] for p, s in zip(ps, ss)]
        aks = [s[:, :w] for s in ss]
    return [p + _mm_neumann(ak, _bd(p)) for p, ak in zip(ps, aks)]


def _rwkv_scan_body(*refs, rev, final, n_chunks, group):
    h_ref = refs[-1]

    @pl.when(pl.program_id(1) == 0)
    def _():
        h_ref[...] = jnp.zeros_like(h_ref)

    n_groups = n_chunks // group
    if n_groups == 1:
        _rwkv_group(refs, 0, rev=rev, final=final, group=group)
    else:
        def body(i, carry):
            gi = (n_groups - 1 - i) if rev else i
            _rwkv_group(refs, gi * (group * CHUNK), rev=rev, final=final, group=group)
            return carry

        lax.fori_loop(0, n_groups, body, 0)


def _rwkv_group(refs, base, *, rev, final, group):
    if final:
        (r_ref, k_ref, v_ref, kk_ref, bv_ref, lw_ref, yf_ref, g_ref, bonus_ref, lnw_ref, lnb_ref,
         o_ref, h_ref) = refs
    else:
        r_ref, k_ref, v_ref, kk_ref, bv_ref, lw_ref, o_ref, h_ref = refs

    C = CHUNK
    last = 0 if rev else C - 1
    t_row = lax.broadcasted_iota(jnp.int32, (C, 2 * PAIR), 0)
    t_col = lax.broadcasted_iota(jnp.int32, (C, 2 * PAIR), 1) % C
    strict = (t_row < t_col) if rev else (t_row > t_col)
    incl = (t_row <= t_col) if rev else (t_row >= t_col)
    ch_row = lax.broadcasted_iota(jnp.int32, (PAIR, 2 * PAIR), 0)
    ch_col = lax.broadcasted_iota(jnp.int32, (PAIR, 2 * PAIR), 1) % PAIR
    same_head = (ch_row // RWKV_DIM) == (ch_col // RWKV_DIM)
    eye = (lax.broadcasted_iota(jnp.int32, (PAIR, PAIR), 0)
           == lax.broadcasted_iota(jnp.int32, (PAIR, PAIR), 1))
    zeros = jnp.zeros((C, PAIR), F32)
    pairs = range(N_PAIRS)
    cat = jnp.concatenate

    def rows(c):
        if isinstance(base, int):
            return slice(base + c * C, base + (c + 1) * C)
        return pl.ds(pl.multiple_of(base + c * C, C), C)

    def slabs(x):
        return [x[:, p * PAIR:(p + 1) * PAIR] for p in pairs]

    at, rt, bt, kt, vv, bh, kh, w_tot = [], [], [], [], [], [], [], []
    for c in range(group):
        sl = rows(c)
        lw = lw_ref[sl, :]
        c_inc = _cumsum_rows(lw, rev)
        c_exc = c_inc - lw
        c_tot = c_inc[last:last + 1, :]
        e_ninc = jnp.exp(-c_inc)
        e_hat = jnp.exp(c_tot - c_inc)
        kk = kk_ref[sl, :].astype(F32)
        bv = bv_ref[sl, :].astype(F32)
        k2 = k_ref[sl, :].astype(F32)
        w_tot += slabs(jnp.exp(c_tot))
        at += slabs(-kk * jnp.exp(c_exc))
        rt += slabs(r_ref[sl, :].astype(F32) * jnp.exp(c_inc))
        bt += slabs(bv * e_ninc)
        kt += slabs(k2 * e_ninc)
        vv += slabs(v_ref[sl, :].astype(F32))
        bh += slabs(bv * e_hat)
        kh += slabs(k2 * e_hat)

    items = range(group * N_PAIRS)
    pm = [_mm_nt(cat([at[i], rt[i]], axis=0), cat([_bd(bt[i]), _bd(kt[i])], axis=0)) for i in items]
    a_abk = [jnp.where(strict, pm[i][:C, :], 0.0) for i in items]
    a_rbk = [jnp.where(incl, pm[i][C:, :], 0.0) for i in items]
    tinv = _neumann_inverse([a_abk[i][:, :PAIR] for i in items])
    x1 = [_mm(a_abk[i][:, PAIR:], _bd(vv[i])) for i in items]
    z = [_mm(tinv[i], _bd(cat([at[i], x1[i]], axis=1))) for i in items]
    w2 = [cat([z[i], cat([zeros, vv[i]], axis=1)], axis=0) for i in items]
    mg = [jnp.where(same_head, _mm_tn(cat([bh[i], kh[i]], axis=0), w2[i]), 0.0) for i in items]
    ry = [_mm(a_rbk[i], cat([_bd(z[i]), _bd(w2[i][C:])], axis=0)) for i in items]
    lhs = [cat([rt[i] + ry[i][:, :PAIR], mg[i][:, :PAIR] + jnp.where(eye, w_tot[i], 0.0)], axis=0)
           for i in items]

    hs = [h_ref[p] for p in pairs]
    for c in (range(group - 1, -1, -1) if rev else range(group)):
        sl = rows(c)
        yh = [_mm(lhs[c * N_PAIRS + p], hs[p]) for p in pairs]
        hs = [yh[p][C:] + mg[c * N_PAIRS + p][:, PAIR:] for p in pairs]
        y = cat([yh[p][:C] + ry[c * N_PAIRS + p][:, PAIR:] for p in pairs], axis=1)
        if final:
            y = y + yf_ref[sl, :]
            mean = _head_sum(y) * (1.0 / RWKV_DIM)
            yc = y - mean
            var = _head_sum(yc * yc) * (1.0 / RWKV_DIM)
            yn = yc * lax.rsqrt(var + GN_EPS) * lnw_ref[...] + lnb_ref[...]
            o_ref[sl, :] = ((yn + bonus_ref[sl, :].astype(F32)) * g_ref[sl, :].astype(F32)).astype(o_ref.dtype)
        else:
            o_ref[sl, :] = y
    h_ref[...] = jnp.stack(hs, axis=0)


def _rwkv_dir(r, k2, v, kk, bv, lw, *, rev, tb, fin=None):
    B, T, W = r.shape
    nc = T // tb

    def tmap(n):
        return (nc - 1 - n) if rev else n

    blk = pl.BlockSpec((None, tb, W), lambda b, n: (b, tmap(n), 0))
    vec = pl.BlockSpec((1, W), lambda b, n: (0, 0))
    in_specs = [blk] * 6
    args = [r, k2, v, kk, bv, lw]
    if fin is not None:
        in_specs += [blk, blk, blk, vec, vec]
        args += list(fin)
    return pl.pallas_call(
        functools.partial(_rwkv_scan_body, rev=rev, final=fin is not None, n_chunks=tb // CHUNK,
                          group=RWKV_GROUP),
        out_shape=jax.ShapeDtypeStruct((B, T, W), BF16 if fin is not None else F32),
        grid=(B, nc),
        in_specs=in_specs,
        out_specs=blk,
        scratch_shapes=[pltpu.VMEM((N_PAIRS, PAIR, PAIR), F32)],
        compiler_params=_cparams(("parallel", "arbitrary")),
        name="rwkv_bwd" if rev else "rwkv_fwd",
    )(*args)


def _attn_body(sink_ref, q_ref, kp_ref, kc_ref, kn_ref, vp_ref, vc_ref, vn_ref, o_ref):
    n = pl.program_id(1)
    nb = pl.num_programs(1)
    r = lax.broadcasted_iota(jnp.int32, (BLOCK, 3 * BLOCK), 0)
    c = lax.broadcasted_iota(jnp.int32, (BLOCK, 3 * BLOCK), 1)
    d = c - BLOCK - r
    lo = jnp.where(n > 0, 0, BLOCK)
    hi = jnp.where(n < nb - 1, 3 * BLOCK, 2 * BLOCK)
    valid = (d >= -BLOCK) & (d <= BLOCK) & (c >= lo) & (c < hi)
    scale = ATT_DIM ** -0.5
    for kh in range(ATT_KV_HEADS):
        ks = slice(kh * ATT_DIM, (kh + 1) * ATT_DIM)
        kw = jnp.concatenate([kp_ref[:, ks], kc_ref[:, ks], kn_ref[:, ks]], axis=0)
        vw = jnp.concatenate([vp_ref[:, ks], vc_ref[:, ks], vn_ref[:, ks]], axis=0)
        heads = [kh * ATT_GROUP + g for g in range(ATT_GROUP)]
        qg = jnp.concatenate([q_ref[:, h * ATT_DIM:(h + 1) * ATT_DIM] for h in heads], axis=0)
        s_all = lax.dot_general(qg, kw, (((1,), (1,)), ((), ())), preferred_element_type=F32)
        ps, inv_dens = [], []
        for g, h in enumerate(heads):
            s = jnp.where(valid, s_all[g * BLOCK:(g + 1) * BLOCK, :], -jnp.inf)
            sk = sink_ref[h] * (1.0 / scale)
            m = jnp.maximum(jnp.max(s, axis=-1, keepdims=True), sk)
            e = jnp.exp2((s - m) * (scale * LOG2_E))
            inv_dens.append(1.0 / (jnp.sum(e, axis=-1, keepdims=True) + jnp.exp2((sk - m) * (scale * LOG2_E))))
            ps.append(e.astype(BF16))
        o_all = jnp.dot(jnp.concatenate(ps, axis=0), vw, preferred_element_type=F32)
        for g, h in enumerate(heads):
            o_ref[:, h * ATT_DIM:(h + 1) * ATT_DIM] = (
                o_all[g * BLOCK:(g + 1) * BLOCK, :] * inv_dens[g]).astype(o_ref.dtype)


def _attention(qkv, sink):
    B, T, _ = qkv.shape
    nb = T // BLOCK
    kvw = ATT_KV_HEADS * ATT_DIM
    kc = (ATT_HEADS * ATT_DIM) // kvw
    vc = kc + 1

    def blk(cidx, off):
        def imap(b, n):
            return (b, jnp.clip(n + off, 0, nb - 1), cidx)
        return pl.BlockSpec((None, BLOCK, kvw), imap)

    return pl.pallas_call(
        _attn_body,
        out_shape=jax.ShapeDtypeStruct((B, T, ATT_HEADS * ATT_DIM), BF16),
        grid=(B, nb),
        in_specs=[pl.BlockSpec(memory_space=pltpu.SMEM),
                  pl.BlockSpec((None, BLOCK, ATT_HEADS * ATT_DIM), lambda b, n: (b, n, 0)),
                  blk(kc, -1), blk(kc, 0), blk(kc, 1), blk(vc, -1), blk(vc, 0), blk(vc, 1)],
        out_specs=pl.BlockSpec((None, BLOCK, ATT_HEADS * ATT_DIM), lambda b, n: (b, n, 0)),
        compiler_params=_cparams(("parallel", "parallel")),
        name="attention",
    )(sink, qkv, qkv, qkv, qkv, qkv, qkv, qkv)


def _ffn_body(x_ref, xp_ref, xn_ref, g_ref, wg_ref, wv_ref, cw_ref, cb_ref, wd_ref, fg_ref, o_ref, h_ref,
              *, tm, tiles_per_seq, final_norm):
    i = pl.program_id(0)
    f = pl.program_id(1)
    nf = pl.num_programs(1)

    t = i % tiles_per_seq
    has_prev = jnp.where(t > 0, 1.0, 0.0).astype(F32)
    has_next = jnp.where(t < tiles_per_seq - 1, 1.0, 0.0).astype(F32)
    rows = lax.broadcasted_iota(jnp.int32, (tm, 1), 0)

    def activation(gm, g_halo, val):
        g_prev = jnp.where(rows == 0, g_halo[7:8, :] * has_prev, pltpu.roll(gm, 1, axis=0))
        g_next = jnp.where(rows == tm - 1, g_halo[8:9, :] * has_next, pltpu.roll(gm, tm - 1, axis=0))
        gate = g_prev * cw_ref[0:1, :] + gm * cw_ref[1:2, :] + g_next * cw_ref[2:3, :] + cb_ref[...]
        return (gate * _sigmoid(gate) * val).astype(BF16)

    @pl.when(f == 0)
    def _():
        def norm(x):
            ms = jnp.mean(x * x, axis=-1, keepdims=True)
            return (x * lax.rsqrt(ms + RMS_EPS) * g_ref[...]).astype(BF16)

        n_chunks = 4
        rc = tm // n_chunks
        h_halo = norm(jnp.concatenate([xp_ref[...], xn_ref[...]], axis=0))
        h_ref[tm:tm + 16, :] = h_halo
        gms, vals, g_halo = [], [], None
        for c in range(n_chunks):
            h_c = norm(x_ref[c * rc:(c + 1) * rc, :])
            h_ref[c * rc:(c + 1) * rc, :] = h_c
            if c == n_chunks - 1:
                ge = jnp.dot(jnp.concatenate([h_c, h_halo], axis=0), wg_ref[...], preferred_element_type=F32)
                gms.append(ge[:rc])
                g_halo = ge[rc:]
            else:
                gms.append(jnp.dot(h_c, wg_ref[...], preferred_element_type=F32))
            vals.append(jnp.dot(h_c, wv_ref[...], preferred_element_type=F32))
        act = activation(jnp.concatenate(gms, axis=0), g_halo, jnp.concatenate(vals, axis=0))
        o_ref[...] = x_ref[...] + jnp.dot(act, wd_ref[...], preferred_element_type=F32)

    @pl.when(f > 0)
    def _():
        ge = jnp.dot(h_ref[...], wg_ref[...], preferred_element_type=F32)
        val = jnp.dot(h_ref[0:tm, :], wv_ref[...], preferred_element_type=F32)
        act = activation(ge[0:tm, :], ge[tm:, :], val)
        o_ref[...] += jnp.dot(act, wd_ref[...], preferred_element_type=F32)

    if final_norm:
        @pl.when(f == nf - 1)
        def _():
            y = o_ref[...]
            ms = jnp.mean(y * y, axis=-1, keepdims=True)
            o_ref[...] = y * lax.rsqrt(ms + RMS_EPS) * fg_ref[...]


def _ffn(x, g, w_up, conv_w, conv_b, w_down, final_g, *, layer, seq_len, tm, tf, final_norm):
    M, D = x.shape
    nf = D_FF // tf
    hb = tm // 8
    last8 = M // 8 - 1
    in_specs = [
        pl.BlockSpec((tm, D), lambda i, f: (i, 0)),
        pl.BlockSpec((8, D), lambda i, f: (jnp.maximum(i * hb - 1, 0), 0)),
        pl.BlockSpec((8, D), lambda i, f: (jnp.minimum((i + 1) * hb, last8), 0)),
        pl.BlockSpec((1, D), lambda i, f: (0, 0)),
        pl.BlockSpec((None, D, tf), lambda i, f: (layer, 0, f)),
        pl.BlockSpec((None, D, tf), lambda i, f: (layer, 0, nf + f)),
        pl.BlockSpec((3, tf), lambda i, f: (0, f)),
        pl.BlockSpec((1, tf), lambda i, f: (0, f)),
        pl.BlockSpec((None, tf, D), lambda i, f: (layer, f, 0)),
        pl.BlockSpec((1, D), lambda i, f: (0, 0)),
    ]
    return pl.pallas_call(
        functools.partial(_ffn_body, tm=tm, tiles_per_seq=seq_len // tm, final_norm=final_norm),
        out_shape=jax.ShapeDtypeStruct((M, D), F32),
        grid=(M // tm, nf),
        in_specs=in_specs,
        out_specs=pl.BlockSpec((tm, D), lambda i, f: (i, 0)),
        scratch_shapes=[pltpu.VMEM((tm + 16, D), BF16)],
        compiler_params=_cparams(("parallel", "arbitrary")),
        name="conv_ffn",
    )(x, x, x, g, w_up, w_up, conv_w, conv_b, w_down, final_g)


def _pick(n, prefs):
    for p in prefs:
        if n % p == 0:
            return p
    raise ValueError(f"no tile for {n}")


def _rope_tables(T):
    half = ATT_DIM // 2
    inv = ROPE_THETA ** (-jnp.arange(half, dtype=F32) / half)
    ang = jnp.arange(T, dtype=F32)[:, None] * inv[None, :]
    cos = jnp.cos(ang)
    sin = jnp.sin(ang)
    return jnp.concatenate([cos, cos], axis=1), jnp.concatenate([-sin, sin], axis=1)


def _prepare_params(p):
    q = dict(p)
    w_in = p['ab_w_in'][0]
    q['w_in'] = jnp.pad(w_in, ((0, 0), (0, MIX_COLS_PAD - MIX_COLS))).astype(BF16)
    mu = p['rwkv_mu'][0]
    q['mu'] = jnp.pad(mu, (0, LORA_PAD - LORA_COLS))[None, :]
    q['lb'] = jnp.cumsum(jax.nn.softmax(p['hgrn_lb'].astype(F32), axis=0), axis=0)
    q['w_out'] = p['ab_w_out'][0].astype(BF16)
    q['w_qkv'] = p['att_w_qkv'][0].astype(BF16)
    q['w_o'] = p['att_w_o'][0].astype(BF16)
    q['w_up'] = p['ffn_w_up'].astype(BF16)
    q['w_down'] = p['ffn_w_down'].astype(BF16)
    return q


def _mixer_layer(x2, q, B, T, layer):
    M = B * T
    tm = _pick(M, (512, 256, 128))
    proj = _norm_matmul(x2, q['mix_norm'][layer][None, :], q['w_in'], tm=tm, tn=MIX_COLS_PAD // 4,
                        out_dtype=F32)
    proj = proj.reshape(B, T, MIX_COLS_PAD)
    lb = q['lb'][layer][None, :]
    tb = _pick(T, (256, 128, 64))
    tb_hgrn = _pick(T, (512, 256, 128, 64))
    o_fwd = _hgrn_dir(proj, lb, rev=False, tb=tb_hgrn)
    ya = _hgrn_dir(proj, lb, rev=True, tb=tb_hgrn, ofwd=o_fwd, onorm=q['hgrn_onorm'][0][None, :])

    r, k2, v, kk, bv, lwf, lwb, g, bonus = _rwkv_prep(
        proj, q['mu'], q['rwkv_w0'][0], q['rwkv_w2'][0], q['rwkv_a0'][0][None, :], q['rwkv_a2'][0],
        q['rwkv_g2'][0], q['rwkv_kk'][0][None, :], q['rwkv_ka'][0][None, :],
        q['rwkv_rk'][0].reshape(1, RWKV_WIDTH), tb=tb)
    y_fwd = _rwkv_dir(r, k2, v, kk, bv, lwf, rev=False, tb=tb)
    yb = _rwkv_dir(r, k2, v, kk, bv, lwb, rev=True, tb=tb,
                   fin=(y_fwd, g, bonus, q['rwkv_ln_w'][0][None, :], q['rwkv_ln_b'][0][None, :]))
    return _matmul_res([ya.reshape(M, HGRN_WIDTH), yb.reshape(M, RWKV_WIDTH)], q['w_out'], x2,
                       tm=tm, tn=D_MODEL)


def _attention_layer(x2, q, B, T, layer, rope_tabs):
    M = B * T
    tm = _pick(M, (512, 256, 128))
    cos, sin = rope_tabs
    qkv = _norm_matmul(x2, q['mix_norm'][layer][None, :], q['w_qkv'], tm=tm, tn=QKV_COLS, out_dtype=BF16,
                       rope=(cos, sin, ATT_HEADS + ATT_KV_HEADS))
    o = _attention(qkv.reshape(B, T, QKV_COLS), q['att_sink'][0])
    return _matmul_res([o.reshape(M, ATT_HEADS * ATT_DIM)], q['w_o'], x2, tm=tm, tn=D_MODEL)


def _trunk(x, q):
    B, T, D = x.shape
    M = B * T
    x2 = x.reshape(M, D)
    rope_tabs = _rope_tables(T)
    tm = _pick(T, (1024, 512, 256, 128))
    for layer in range(DEPTH):
        if layer % 2 == 0:
            x2 = _mixer_layer(x2, q, B, T, layer)
        else:
            x2 = _attention_layer(x2, q, B, T, layer, rope_tabs)
        x2 = _ffn(x2, q['ffn_norm'][layer][None, :], q['w_up'], q['ffn_conv_w'][layer],
                  q['ffn_conv_b'][layer][None, :], q['w_down'], q['final_norm'][None, :],
                  layer=layer, seq_len=T, tm=tm, tf=512, final_norm=(layer == DEPTH - 1))
    return x2.reshape(B, T, D)


def kernel(x_prompt, x_sample, mix_norm, ab_w_in, hgrn_lb, hgrn_onorm, rwkv_mu, rwkv_w0, rwkv_w2, rwkv_a0, rwkv_a2, rwkv_g2, rwkv_kk, rwkv_ka, rwkv_rk, rwkv_ln_w, rwkv_ln_b, ab_w_out, att_w_qkv, att_sink, att_w_o, ffn_norm, ffn_w_up, ffn_conv_w, ffn_conv_b, ffn_w_down, final_norm):
    p = {
        'mix_norm': mix_norm, 'ab_w_in': ab_w_in, 'hgrn_lb': hgrn_lb, 'hgrn_onorm': hgrn_onorm,
        'rwkv_mu': rwkv_mu, 'rwkv_w0': rwkv_w0, 'rwkv_w2': rwkv_w2, 'rwkv_a0': rwkv_a0, 'rwkv_a2': rwkv_a2,
        'rwkv_g2': rwkv_g2, 'rwkv_kk': rwkv_kk, 'rwkv_ka': rwkv_ka, 'rwkv_rk': rwkv_rk,
        'rwkv_ln_w': rwkv_ln_w, 'rwkv_ln_b': rwkv_ln_b, 'ab_w_out': ab_w_out,
        'att_w_qkv': att_w_qkv, 'att_sink': att_sink, 'att_w_o': att_w_o,
        'ffn_norm': ffn_norm, 'ffn_w_up': ffn_w_up, 'ffn_conv_w': ffn_conv_w, 'ffn_conv_b': ffn_conv_b,
        'ffn_w_down': ffn_w_down, 'final_norm': final_norm,
    }
    q = _prepare_params(p)
    return (_trunk(x_prompt, q), _trunk(x_sample, q))
```

```python
import functools

import jax
import jax.numpy as jnp
from jax import lax
from jax.experimental import pallas as pl
from jax.experimental.pallas import tpu as pltpu

F32 = jnp.float32
BF16 = jnp.bfloat16

D_MODEL = 2048
DEPTH = 2
HGRN_DIM = 128
HGRN_HEADS = 8
HGRN_WIDTH = 1024
RWKV_DIM = 64
RWKV_HEADS = 16
RWKV_WIDTH = 1024
DECAY_LORA = 64
AAA_LORA = 64
GATE_LORA = 160
LORA_COLS = 2 * DECAY_LORA + AAA_LORA + GATE_LORA
LORA_PAD = 512
LORA_SLAB = 128
MIX_A_COLS = 5 * HGRN_WIDTH
MIX_COLS = MIX_A_COLS + 3 * RWKV_WIDTH + LORA_COLS
MIX_COLS_PAD = MIX_A_COLS + 3 * RWKV_WIDTH + LORA_PAD
ATT_DIM = 128
ATT_HEADS = 16
ATT_KV_HEADS = 4
ATT_GROUP = 4
QKV_COLS = (ATT_HEADS + 2 * ATT_KV_HEADS) * ATT_DIM
BLOCK = 128
ROPE_THETA = 10000.0
D_FF = 5632
RMS_EPS = 1e-6
GN_EPS = 64e-5
CHUNK = 64
PAIR = 2 * RWKV_DIM
N_PAIRS = RWKV_WIDTH // PAIR
LOG2_E = 1.4426950408889634
EXP_NEG_HALF = 0.6065306597126334
RWKV_GROUP = 8

VMEM_LIMIT = 56 * 1024 * 1024


def _cparams(sem):
    return pltpu.CompilerParams(dimension_semantics=sem, vmem_limit_bytes=VMEM_LIMIT)


def _mm(a, b):
    return jnp.dot(a.astype(BF16), b.astype(BF16), preferred_element_type=F32)


def _mm_nt(a, b):
    return lax.dot_general(a.astype(BF16), b.astype(BF16), (((1,), (1,)), ((), ())),
                           preferred_element_type=F32)


def _mm_tn(a, b):
    return lax.dot_general(a.astype(BF16), b.astype(BF16), (((0,), (0,)), ((), ())),
                           preferred_element_type=F32)


def _cumsum_rows(x, rev):
    n = x.shape[0]
    row = lax.broadcasted_iota(jnp.int32, (n, 1), 0)
    s = 1
    while s < n:
        if rev:
            x = x + jnp.where(row < n - s, pltpu.roll(x, n - s, axis=0), 0.0)
        else:
            x = x + jnp.where(row >= s, pltpu.roll(x, s, axis=0), 0.0)
        s *= 2
    return x


_mm_lora = _mm
_mm_neumann = _mm


def _sigmoid(x):
    return 1.0 / (1.0 + jnp.exp(-x))


def _norm_matmul_body(x_ref, g_ref, w_ref, o_ref, h_ref):
    @pl.when(pl.program_id(1) == 0)
    def _():
        x = x_ref[...]
        ms = jnp.mean(x * x, axis=-1, keepdims=True)
        h_ref[...] = (x * lax.rsqrt(ms + RMS_EPS) * g_ref[...]).astype(BF16)

    o_ref[...] = jnp.dot(h_ref[...], w_ref[...], preferred_element_type=F32).astype(o_ref.dtype)


def _norm_matmul_rope_body(x_ref, g_ref, w_ref, cos_ref, sin_ref, o_ref, h_ref, *, n_rope, tn):
    x = x_ref[...]
    ms = jnp.mean(x * x, axis=-1, keepdims=True)
    h_ref[...] = (x * lax.rsqrt(ms + RMS_EPS) * g_ref[...]).astype(BF16)
    cos = cos_ref[...]
    sin = sin_ref[...]
    group = 4 * ATT_DIM
    for gi in range(tn // group):
        acc = jnp.dot(h_ref[...], w_ref[:, gi * group:(gi + 1) * group], preferred_element_type=F32)
        for hh in range(group // ATT_DIM):
            a = acc[:, hh * ATT_DIM:(hh + 1) * ATT_DIM]
            head = gi * (group // ATT_DIM) + hh
            if head < n_rope:
                a = a * cos + pltpu.roll(a, ATT_DIM // 2, axis=1) * sin
            o_ref[:, head * ATT_DIM:(head + 1) * ATT_DIM] = a.astype(o_ref.dtype)


def _norm_matmul(x, g, w, *, tm, tn, out_dtype, rope=None):
    M, K = x.shape
    N = w.shape[1]
    grid = (M // tm, N // tn)
    in_specs = [pl.BlockSpec((tm, K), lambda i, j: (i, 0)),
                pl.BlockSpec((1, K), lambda i, j: (0, 0)),
                pl.BlockSpec((K, tn), lambda i, j: (0, j))]
    args = [x, g, w]
    if rope is None:
        body = _norm_matmul_body
    else:
        cos, sin, n_rope = rope
        tiles_per_seq = cos.shape[0] // tm
        in_specs += [pl.BlockSpec((tm, ATT_DIM), lambda i, j: (i % tiles_per_seq, 0)),
                     pl.BlockSpec((tm, ATT_DIM), lambda i, j: (i % tiles_per_seq, 0))]
        args += [cos, sin]
        body = functools.partial(_norm_matmul_rope_body, n_rope=n_rope, tn=tn)
    return pl.pallas_call(
        body,
        out_shape=jax.ShapeDtypeStruct((M, N), out_dtype),
        grid=grid,
        in_specs=in_specs,
        out_specs=pl.BlockSpec((tm, tn), lambda i, j: (i, j)),
        scratch_shapes=[pltpu.VMEM((tm, K), BF16)],
        compiler_params=_cparams(("parallel", "arbitrary")),
        name="norm_matmul" if rope is None else "norm_matmul_rope",
    )(*args)


def _matmul_res_body(*refs, n_parts):
    a_refs = refs[:n_parts]
    w_ref, r_ref, o_ref = refs[n_parts:]
    acc = r_ref[...]
    k0 = 0
    for a_ref in a_refs:
        kw = a_ref.shape[1]
        acc = acc + jnp.dot(a_ref[...], w_ref[k0:k0 + kw, :], preferred_element_type=F32)
        k0 += kw
    o_ref[...] = acc


def _matmul_res(a_parts, w, res, *, tm, tn):
    M, N = res.shape
    K = w.shape[0]
    in_specs = [pl.BlockSpec((tm, a.shape[1]), lambda i, j: (i, 0)) for a in a_parts]
    in_specs += [pl.BlockSpec((K, tn), lambda i, j: (0, j)),
                 pl.BlockSpec((tm, tn), lambda i, j: (i, j))]
    return pl.pallas_call(
        functools.partial(_matmul_res_body, n_parts=len(a_parts)),
        out_shape=jax.ShapeDtypeStruct((M, N), F32),
        grid=(M // tm, N // tn),
        in_specs=in_specs,
        out_specs=pl.BlockSpec((tm, tn), lambda i, j: (i, j)),
        compiler_params=_cparams(("parallel", "arbitrary")),
        name="matmul_res",
    )(*a_parts, w, res)


def _hgrn_body(*refs, rev, n_chunks, final):
    if final:
        q_ref, i_ref, z_ref, lb_ref, ofwd_ref, g_ref, onorm_ref, o_ref, st_ref = refs
    else:
        q_ref, i_ref, z_ref, lb_ref, o_ref, st_ref = refs

    @pl.when(pl.program_id(1) == 0)
    def _():
        st_ref[...] = jnp.zeros_like(st_ref)

    lb = lb_ref[...]
    row = lax.broadcasted_iota(jnp.int32, (CHUNK, CHUNK), 0)
    col = lax.broadcasted_iota(jnp.int32, (CHUNK, CHUNK), 1)
    keep = (row <= col) if rev else (row >= col)
    last = 0 if rev else CHUNK - 1

    head_slices = [slice(h * HGRN_DIM, (h + 1) * HGRN_DIM) for h in range(HGRN_HEADS)]
    sts = [st_ref[h] for h in range(HGRN_HEADS)]
    order = range(n_chunks - 1, -1, -1) if rev else range(n_chunks)
    for ci in order:
        sl = slice(ci * CHUNK, (ci + 1) * CHUNK)
        f = lb + (1.0 - lb) * _sigmoid(z_ref[sl, :])
        kk = 1.0 - f
        b = _cumsum_rows(jnp.log(f), rev)
        b_last = b[last:last + 1, :]
        q_d = q_ref[sl, :] * jnp.exp(b)
        k_d = kk * jnp.exp(-b)
        k_u = kk * jnp.exp(b_last - b)
        dec = jnp.exp(b_last)
        v = i_ref[sl, :]
        att = [jnp.where(keep, _mm_nt(q_d[:, hs], k_d[:, hs]), 0.0) for hs in head_slices]
        outs = [_mm(att[h], v[:, hs]) + _mm_nt(q_d[:, hs], sts[h]) for h, hs in enumerate(head_slices)]
        upd = [_mm_tn(v[:, hs], k_u[:, hs]) for hs in head_slices]
        sts = [sts[h] * dec[:, hs] + upd[h] for h, hs in enumerate(head_slices)]
        if final:
            g = g_ref[sl, :]
            silu_g = g * _sigmoid(g)
            normed = []
            for h, hs in enumerate(head_slices):
                oa = outs[h] + ofwd_ref[sl, hs]
                normed.append(oa * lax.rsqrt(jnp.mean(oa * oa, axis=-1, keepdims=True) + RMS_EPS))
            o_ref[sl, :] = (jnp.concatenate(normed, axis=1) * onorm_ref[...] * silu_g).astype(o_ref.dtype)
        else:
            o_ref[sl, :] = jnp.concatenate(outs, axis=1)
    st_ref[...] = jnp.stack(sts, axis=0)


def _hgrn_dir(proj, lb, *, rev, tb, ofwd=None, onorm=None):
    B, T, _ = proj.shape
    nt = T // tb
    final = ofwd is not None

    def tmap(n):
        return (nt - 1 - n) if rev else n

    def col(c):
        return pl.BlockSpec((None, tb, HGRN_WIDTH), lambda b, n: (b, tmap(n), c))

    vec = pl.BlockSpec((1, HGRN_WIDTH), lambda b, n: (0, 0))
    in_specs = [col(0), col(1), col(3 if rev else 2), vec]
    args = [proj, proj, proj, lb]
    if final:
        in_specs += [col(0), col(4), vec]
        args += [ofwd, proj, onorm]
    return pl.pallas_call(
        functools.partial(_hgrn_body, rev=rev, n_chunks=tb // CHUNK, final=final),
        out_shape=jax.ShapeDtypeStruct((B, T, HGRN_WIDTH), BF16 if final else F32),
        grid=(B, nt),
        in_specs=in_specs,
        out_specs=pl.BlockSpec((None, tb, HGRN_WIDTH), lambda b, n: (b, tmap(n), 0)),
        scratch_shapes=[pltpu.VMEM((HGRN_HEADS, HGRN_DIM, HGRN_DIM), F32)],
        compiler_params=_cparams(("parallel", "arbitrary")),
        name="hgrn_bwd" if rev else "hgrn_fwd",
    )(*args)


def _head_sum(x):
    m0 = lax.broadcasted_iota(jnp.int32, (x.shape[0], PAIR), 1) < RWKV_DIM
    outs = []
    for p in range(N_PAIRS):
        xs = x[:, p * PAIR:(p + 1) * PAIR]
        s0 = jnp.sum(jnp.where(m0, xs, 0.0), axis=-1, keepdims=True)
        s1 = jnp.sum(jnp.where(m0, 0.0, xs), axis=-1, keepdims=True)
        outs.append(jnp.where(m0, s0, s1))
    return jnp.concatenate(outs, axis=1)


def _rwkv_prep_body(r_ref, k_ref, v_ref, l_ref,
                    rp_ref, kp_ref, vp_ref, lp_ref, rn_ref, kn_ref, vn_ref, ln_ref,
                    mu_r_ref, mu_k_ref, mu_v_ref, mu_l_ref,
                    w0_ref, w2f_ref, w2b_ref, a0_ref, a2_ref, g2_ref, kkw_ref, kaw_ref, rk_ref,
                    ro_ref, ko_ref, vo_ref, kko_ref, bvo_ref, lwf_ref, lwb_ref, go_ref, bo_ref,
                    *, tb):
    n = pl.program_id(1)
    nt = pl.num_programs(1)
    has_prev = jnp.where(n > 0, 1.0, 0.0).astype(F32)
    has_next = jnp.where(n < nt - 1, 1.0, 0.0).astype(F32)
    rows8 = lax.broadcasted_iota(jnp.int32, (8, 1), 0)
    first8 = rows8 == 0
    last8 = rows8 == 7

    def shift(x_ref, p_ref, n_ref, mu_ref):
        x = x_ref[...]
        down = pltpu.roll(x, 1, axis=0)
        up = pltpu.roll(x, tb - 1, axis=0)
        prev = jnp.concatenate([jnp.where(first8, p_ref[7:8, :] * has_prev, down[0:8]), down[8:]], axis=0)
        nxt = jnp.concatenate([up[:tb - 8], jnp.where(last8, n_ref[0:1, :] * has_next, up[tb - 8:])], axis=0)
        mu = mu_ref[...]
        return x * (1.0 - mu) + (prev + nxt) * (0.5 * mu)

    r = shift(r_ref, rp_ref, rn_ref, mu_r_ref)
    k = shift(k_ref, kp_ref, kn_ref, mu_k_ref)
    v = shift(v_ref, vp_ref, vn_ref, mu_v_ref)
    lo = shift(l_ref, lp_ref, ln_ref, mu_l_ref)
    wd = jnp.tanh(lo[:, 0:LORA_SLAB])
    ag = lo[:, LORA_SLAB:3 * LORA_SLAB]

    def log_decay(w0, w2):
        u = w0 + _mm_lora(wd, w2)
        return -EXP_NEG_HALF * _sigmoid(u)

    lwf_ref[...] = log_decay(w0_ref[0:1, :], w2f_ref[...])
    lwb_ref[...] = log_decay(w0_ref[1:2, :], w2b_ref[...])
    a = _sigmoid(a0_ref[...] + _mm_lora(ag[:, 0:LORA_SLAB], a2_ref[...]))
    go_ref[...] = _mm_lora(_sigmoid(ag), g2_ref[...]).astype(go_ref.dtype)

    kk = k * kkw_ref[...]
    norm = jnp.maximum(jnp.sqrt(_head_sum(kk * kk)), 1e-12)
    kk = kk / norm
    k2 = k * (1.0 + (a - 1.0) * kaw_ref[...])
    ro_ref[...] = r.astype(ro_ref.dtype)
    ko_ref[...] = k2.astype(ko_ref.dtype)
    vo_ref[...] = v.astype(vo_ref.dtype)
    kko_ref[...] = kk.astype(kko_ref.dtype)
    bvo_ref[...] = (kk * a).astype(bvo_ref.dtype)
    bo_ref[...] = (_head_sum(r * k2 * rk_ref[...]) * v).astype(bo_ref.dtype)


def _rwkv_prep(proj, mu, w0, w2, a0, a2, g2, kkw, kaw, rk, *, tb):
    B, T, _ = proj.shape
    nt = T // tb
    hb = tb // 8
    last8 = T // 8 - 1

    def main(width, c):
        return pl.BlockSpec((None, tb, width), lambda b, n: (b, n, c))

    def prev(width, c):
        return pl.BlockSpec((None, 8, width), lambda b, n: (b, jnp.maximum(n * hb - 1, 0), c))

    def nxt(width, c):
        return pl.BlockSpec((None, 8, width), lambda b, n: (b, jnp.minimum((n + 1) * hb, last8), c))

    def full(a):
        return pl.BlockSpec(a.shape, lambda b, n: (0,) * a.ndim)

    W = RWKV_WIDTH
    c_l = (MIX_A_COLS + 3 * W) // LORA_PAD
    zpad = lambda a, before, total: jnp.pad(a, ((before, total - before - a.shape[0]), (0, 0)))
    params = [mu[:, 0:W], mu[:, W:2 * W], mu[:, 2 * W:3 * W], mu[:, 3 * W:],
              w0, zpad(w2[0], 0, LORA_SLAB), zpad(w2[1], DECAY_LORA, LORA_SLAB), a0,
              zpad(a2, 0, LORA_SLAB), zpad(g2, AAA_LORA, 2 * LORA_SLAB), kkw, kaw, rk]
    in_specs = ([main(W, 5), main(W, 6), main(W, 7), main(LORA_PAD, c_l),
                 prev(W, 5), prev(W, 6), prev(W, 7), prev(LORA_PAD, c_l),
                 nxt(W, 5), nxt(W, 6), nxt(W, 7), nxt(LORA_PAD, c_l)]
                + [full(p) for p in params])
    out_dtypes = [BF16] * 5 + [F32, F32, BF16, BF16]
    return pl.pallas_call(
        functools.partial(_rwkv_prep_body, tb=tb),
        out_shape=[jax.ShapeDtypeStruct((B, T, W), dt) for dt in out_dtypes],
        grid=(B, nt),
        in_specs=in_specs,
        out_specs=[pl.BlockSpec((None, tb, W), lambda b, n: (b, n, 0))] * 9,
        compiler_params=_cparams(("parallel", "parallel")),
        name="rwkv_prep",
    )(*([proj] * 12), *params)


def _bd(x):
    lane = lax.broadcasted_iota(jnp.int32, x.shape, 1) % PAIR
    m0 = lane < RWKV_DIM
    return jnp.concatenate([jnp.where(m0, x, 0.0), jnp.where(m0, 0.0, x)], axis=0)

def _neumann_inverse(mats):
    n, w = mats[0].shape
    eye = jnp.where(lax.broadcasted_iota(jnp.int32, (n, w), 0) == lax.broadcasted_iota(jnp.int32, (n, w), 1) % n,
                    1.0, 0.0).astype(F32)
    ps = [eye + a for a in mats]
    aks = [_mm_neumann(a, _bd(a)) for a in mats]
    for _ in range(4):
        ss = [_mm_neumann(ak, _bd(jnp.concatenate([ak, p], axis=1))) for p, ak in zip(ps, aks)]
        ps = [p + s[:, w:] for p, s in zip(ps, ss)]
        aks = [s[:, :w] for s in ss]
    return [p + _mm_neumann(ak, _bd(p)) for p, ak in zip(ps, aks)]


def _rwkv_scan_body(*refs, rev, final, n_chunks, group):
    h_ref = refs[-1]

    @pl.when(pl.program_id(1) == 0)
    def _():
        h_ref[...] = jnp.zeros_like(h_ref)

    n_groups = n_chunks // group
    if n_groups == 1:
        _rwkv_group(refs, 0, rev=rev, final=final, group=group)
    else:
        def body(i, carry):
            gi = (n_groups - 1 - i) if rev else i
            _rwkv_group(refs, gi * (group * CHUNK), rev=rev, final=final, group=group)
            return carry

        lax.fori_loop(0, n_groups, body, 0)


def _rwkv_group(refs, base, *, rev, final, group):
    if final:
        (r_ref, k_ref, v_ref, kk_ref, bv_ref, lw_ref, yf_ref, g_ref, bonus_ref, lnw_ref, lnb_ref,
         o_ref, h_ref) = refs
    else:
        r_ref, k_ref, v_ref, kk_ref, bv_ref, lw_ref, o_ref, h_ref = refs

    C = CHUNK
    last = 0 if rev else C - 1
    t_row = lax.broadcasted_iota(jnp.int32, (C, 2 * PAIR), 0)
    t_col = lax.broadcasted_iota(jnp.int32, (C, 2 * PAIR), 1) % C
    strict = (t_row < t_col) if rev else (t_row > t_col)
    incl = (t_row <= t_col) if rev else (t_row >= t_col)
    ch_row = lax.broadcasted_iota(jnp.int32, (PAIR, 2 * PAIR), 0)
    ch_col = lax.broadcasted_iota(jnp.int32, (PAIR, 2 * PAIR), 1) % PAIR
    same_head = (ch_row // RWKV_DIM) == (ch_col // RWKV_DIM)
    eye = (lax.broadcasted_iota(jnp.int32, (PAIR, PAIR), 0)
           == lax.broadcasted_iota(jnp.int32, (PAIR, PAIR), 1))
    zeros = jnp.zeros((C, PAIR), F32)
    pairs = range(N_PAIRS)
    cat = jnp.concatenate

    def rows(c):
        if isinstance(base, int):
            return slice(base + c * C, base + (c + 1) * C)
        return pl.ds(pl.multiple_of(base + c * C, C), C)

    def slabs(x):
        return [x[:, p * PAIR:(p + 1) * PAIR] for p in pairs]

    at, rt, bt, kt, vv, bh, kh, w_tot = [], [], [], [], [], [], [], []
    for c in range(group):
        sl = rows(c)
        lw = lw_ref[sl, :]
        c_inc = _cumsum_rows(lw, rev)
        c_exc = c_inc - lw
        c_tot = c_inc[last:last + 1, :]
        e_ninc = jnp.exp(-c_inc)
        e_hat = jnp.exp(c_tot - c_inc)
        kk = kk_ref[sl, :].astype(F32)
        bv = bv_ref[sl, :].astype(F32)
        k2 = k_ref[sl, :].astype(F32)
        w_tot += slabs(jnp.exp(c_tot))
        at += slabs(-kk * jnp.exp(c_exc))
        rt += slabs(r_ref[sl, :].astype(F32) * jnp.exp(c_inc))
        bt += slabs(bv * e_ninc)
        kt += slabs(k2 * e_ninc)
        vv += slabs(v_ref[sl, :].astype(F32))
        bh += slabs(bv * e_hat)
        kh += slabs(k2 * e_hat)

    items = range(group * N_PAIRS)
    pm = [_mm_nt(cat([at[i], rt[i]], axis=0), cat([_bd(bt[i]), _bd(kt[i])], axis=0)) for i in items]
    a_abk = [jnp.where(strict, pm[i][:C, :], 0.0) for i in items]
    a_rbk = [jnp.where(incl, pm[i][C:, :], 0.0) for i in items]
    tinv = _neumann_inverse([a_abk[i][:, :PAIR] for i in items])
    x1 = [_mm(a_abk[i][:, PAIR:], _bd(vv[i])) for i in items]
    z = [_mm(tinv[i], _bd(cat([at[i], x1[i]], axis=1))) for i in items]
    w2 = [cat([z[i], cat([zeros, vv[i]], axis=1)], axis=0) for i in items]
    mg = [jnp.where(same_head, _mm_tn(cat([bh[i], kh[i]], axis=0), w2[i]), 0.0) for i in items]
    ry = [_mm(a_rbk[i], cat([_bd(z[i]), _bd(w2[i][C:])], axis=0)) for i in items]
    lhs = [cat([rt[i] + ry[i][:, :PAIR], mg[i][:, :PAIR] + jnp.where(eye, w_tot[i], 0.0)], axis=0)
           for i in items]

    hs = [h_ref[p] for p in pairs]
    for c in (range(group - 1, -1, -1) if rev else range(group)):
        sl = rows(c)
        yh = [_mm(lhs[c * N_PAIRS + p], hs[p]) for p in pairs]
        hs = [yh[p][C:] + mg[c * N_PAIRS + p][:, PAIR:] for p in pairs]
        y = cat([yh[p][:C] + ry[c * N_PAIRS + p][:, PAIR:] for p in pairs], axis=1)
        if final:
            y = y + yf_ref[sl, :]
            mean = _head_sum(y) * (1.0 / RWKV_DIM)
            yc = y - mean
            var = _head_sum(yc * yc) * (1.0 / RWKV_DIM)
            yn = yc * lax.rsqrt(var + GN_EPS) * lnw_ref[...] + lnb_ref[...]
            o_ref[sl, :] = ((yn + bonus_ref[sl, :].astype(F32)) * g_ref[sl, :].astype(F32)).astype(o_ref.dtype)
        else:
            o_ref[sl, :] = y
    h_ref[...] = jnp.stack(hs, axis=0)


def _rwkv_dir(r, k2, v, kk, bv, lw, *, rev, tb, fin=None):
    B, T, W = r.shape
    nc = T // tb

    def tmap(n):
        return (nc - 1 - n) if rev else n

    blk = pl.BlockSpec((None, tb, W), lambda b, n: (b, tmap(n), 0))
    vec = pl.BlockSpec((1, W), lambda b, n: (0, 0))
    in_specs = [blk] * 6
    args = [r, k2, v, kk, bv, lw]
    if fin is not None:
        in_specs += [blk, blk, blk, vec, vec]
        args += list(fin)
    return pl.pallas_call(
        functools.partial(_rwkv_scan_body, rev=rev, final=fin is not None, n_chunks=tb // CHUNK,
                          group=min(RWKV_GROUP, tb // CHUNK)),
        out_shape=jax.ShapeDtypeStruct((B, T, W), BF16 if fin is not None else F32),
        grid=(B, nc),
        in_specs=in_specs,
        out_specs=blk,
        scratch_shapes=[pltpu.VMEM((N_PAIRS, PAIR, PAIR), F32)],
        compiler_params=_cparams(("parallel", "arbitrary")),
        name="rwkv_bwd" if rev else "rwkv_fwd",
    )(*args)


def _attn_body(sink_ref, q_ref, kp_ref, kc_ref, kn_ref, vp_ref, vc_ref, vn_ref, o_ref):
    n = pl.program_id(1)
    nb = pl.num_programs(1)
    r = lax.broadcasted_iota(jnp.int32, (BLOCK, 3 * BLOCK), 0)
    c = lax.broadcasted_iota(jnp.int32, (BLOCK, 3 * BLOCK), 1)
    d = c - BLOCK - r
    lo = jnp.where(n > 0, 0, BLOCK)
    hi = jnp.where(n < nb - 1, 3 * BLOCK, 2 * BLOCK)
    valid = (d >= -BLOCK) & (d <= BLOCK) & (c >= lo) & (c < hi)
    scale = ATT_DIM ** -0.5
    for kh in range(ATT_KV_HEADS):
        ks = slice(kh * ATT_DIM, (kh + 1) * ATT_DIM)
        kw = jnp.concatenate([kp_ref[:, ks], kc_ref[:, ks], kn_ref[:, ks]], axis=0)
        vw = jnp.concatenate([vp_ref[:, ks], vc_ref[:, ks], vn_ref[:, ks]], axis=0)
        heads = [kh * ATT_GROUP + g for g in range(ATT_GROUP)]
        qg = jnp.concatenate([q_ref[:, h * ATT_DIM:(h + 1) * ATT_DIM] for h in heads], axis=0)
        s_all = lax.dot_general(qg, kw, (((1,), (1,)), ((), ())), preferred_element_type=F32)
        ps, inv_dens = [], []
        for g, h in enumerate(heads):
            sg = s_all[g * BLOCK:(g + 1) * BLOCK, :]
            s = jnp.concatenate([jnp.where(valid[:, :BLOCK], sg[:, :BLOCK], -jnp.inf), sg[:, BLOCK:2 * BLOCK],
                                 jnp.where(valid[:, 2 * BLOCK:], sg[:, 2 * BLOCK:], -jnp.inf)], axis=1)
            sk = sink_ref[h] * (1.0 / scale)
            m = jnp.maximum(jnp.max(s, axis=-1, keepdims=True), sk)
            e = jnp.exp2((s - m) * (scale * LOG2_E))
            inv_dens.append(1.0 / (jnp.sum(e, axis=-1, keepdims=True) + jnp.exp2((sk - m) * (scale * LOG2_E))))
            ps.append(e.astype(BF16))
        o_all = jnp.dot(jnp.concatenate(ps, axis=0), vw, preferred_element_type=F32)
        for g, h in enumerate(heads):
            o_ref[:, h * ATT_DIM:(h + 1) * ATT_DIM] = (
                o_all[g * BLOCK:(g + 1) * BLOCK, :] * inv_dens[g]).astype(o_ref.dtype)


def _attention(qkv, sink):
    B, T, _ = qkv.shape
    nb = T // BLOCK
    kvw = ATT_KV_HEADS * ATT_DIM
    kc = (ATT_HEADS * ATT_DIM) // kvw
    vc = kc + 1

    def blk(cidx, off):
        def imap(b, n):
            return (b, jnp.clip(n + off, 0, nb - 1), cidx)
        return pl.BlockSpec((None, BLOCK, kvw), imap)

    return pl.pallas_call(
        _attn_body,
        out_shape=jax.ShapeDtypeStruct((B, T, ATT_HEADS * ATT_DIM), BF16),
        grid=(B, nb),
        in_specs=[pl.BlockSpec(memory_space=pltpu.SMEM),
                  pl.BlockSpec((None, BLOCK, ATT_HEADS * ATT_DIM), lambda b, n: (b, n, 0)),
                  blk(kc, -1), blk(kc, 0), blk(kc, 1), blk(vc, -1), blk(vc, 0), blk(vc, 1)],
        out_specs=pl.BlockSpec((None, BLOCK, ATT_HEADS * ATT_DIM), lambda b, n: (b, n, 0)),
        compiler_params=_cparams(("parallel", "parallel")),
        name="attention",
    )(sink, qkv, qkv, qkv, qkv, qkv, qkv, qkv)


def _ffn_body(x_ref, xp_ref, xn_ref, g_ref, wg_ref, wv_ref, cw_ref, cb_ref, wd_ref, fg_ref, o_ref, h_ref,
              *, tm, tiles_per_seq, final_norm):
    i = pl.program_id(0)
    f = pl.program_id(1)
    nf = pl.num_programs(1)

    t = i % tiles_per_seq
    has_prev = jnp.where(t > 0, 1.0, 0.0).astype(F32)
    has_next = jnp.where(t < tiles_per_seq - 1, 1.0, 0.0).astype(F32)
    rows = lax.broadcasted_iota(jnp.int32, (tm, 1), 0)

    def activation(gm, g_halo, val):
        g_prev = jnp.where(rows == 0, g_halo[7:8, :] * has_prev, pltpu.roll(gm, 1, axis=0))
        g_next = jnp.where(rows == tm - 1, g_halo[8:9, :] * has_next, pltpu.roll(gm, tm - 1, axis=0))
        gate = g_prev * cw_ref[0:1, :] + gm * cw_ref[1:2, :] + g_next * cw_ref[2:3, :] + cb_ref[...]
        return (gate * _sigmoid(gate) * val).astype(BF16)

    @pl.when(f == 0)
    def _():
        def norm(x):
            ms = jnp.mean(x * x, axis=-1, keepdims=True)
            return (x * lax.rsqrt(ms + RMS_EPS) * g_ref[...]).astype(BF16)

        n_chunks = 4
        rc = tm // n_chunks
        h_halo = norm(jnp.concatenate([xp_ref[...], xn_ref[...]], axis=0))
        h_ref[tm:tm + 16, :] = h_halo
        gms, vals, g_halo = [], [], None
        for c in range(n_chunks):
            h_c = norm(x_ref[c * rc:(c + 1) * rc, :])
            h_ref[c * rc:(c + 1) * rc, :] = h_c
            if c == n_chunks - 1:
                ge = jnp.dot(jnp.concatenate([h_c, h_halo], axis=0), wg_ref[...], preferred_element_type=F32)
                gms.append(ge[:rc])
                g_halo = ge[rc:]
            else:
                gms.append(jnp.dot(h_c, wg_ref[...], preferred_element_type=F32))
            vals.append(jnp.dot(h_c, wv_ref[...], preferred_element_type=F32))
        act = activation(jnp.concatenate(gms, axis=0), g_halo, jnp.concatenate(vals, axis=0))
        o_ref[...] = x_ref[...] + jnp.dot(act, wd_ref[...], preferred_element_type=F32)

    @pl.when(f > 0)
    def _():
        ge = jnp.dot(h_ref[...], wg_ref[...], preferred_element_type=F32)
        val = jnp.dot(h_ref[0:tm, :], wv_ref[...], preferred_element_type=F32)
        act = activation(ge[0:tm, :], ge[tm:, :], val)
        o_ref[...] += jnp.dot(act, wd_ref[...], preferred_element_type=F32)

    if final_norm:
        @pl.when(f == nf - 1)
        def _():
            y = o_ref[...]
            ms = jnp.mean(y * y, axis=-1, keepdims=True)
            o_ref[...] = y * lax.rsqrt(ms + RMS_EPS) * fg_ref[...]


def _ffn(x, g, w_up, conv_w, conv_b, w_down, final_g, *, layer, seq_len, tm, tf, final_norm):
    M, D = x.shape
    nf = D_FF // tf
    hb = tm // 8
    last8 = M // 8 - 1
    in_specs = [
        pl.BlockSpec((tm, D), lambda i, f: (i, 0)),
        pl.BlockSpec((8, D), lambda i, f: (jnp.maximum(i * hb - 1, 0), 0)),
        pl.BlockSpec((8, D), lambda i, f: (jnp.minimum((i + 1) * hb, last8), 0)),
        pl.BlockSpec((1, D), lambda i, f: (0, 0)),
        pl.BlockSpec((None, D, tf), lambda i, f: (layer, 0, f)),
        pl.BlockSpec((None, D, tf), lambda i, f: (layer, 0, nf + f)),
        pl.BlockSpec((3, tf), lambda i, f: (0, f)),
        pl.BlockSpec((1, tf), lambda i, f: (0, f)),
        pl.BlockSpec((None, tf, D), lambda i, f: (layer, f, 0)),
        pl.BlockSpec((1, D), lambda i, f: (0, 0)),
    ]
    return pl.pallas_call(
        functools.partial(_ffn_body, tm=tm, tiles_per_seq=seq_len // tm, final_norm=final_norm),
        out_shape=jax.ShapeDtypeStruct((M, D), F32),
        grid=(M // tm, nf),
        in_specs=in_specs,
        out_specs=pl.BlockSpec((tm, D), lambda i, f: (i, 0)),
        scratch_shapes=[pltpu.VMEM((tm + 16, D), BF16)],
        compiler_params=_cparams(("parallel", "arbitrary")),
        name="conv_ffn",
    )(x, x, x, g, w_up, w_up, conv_w, conv_b, w_down, final_g)


def _pick(n, prefs):
    for p in prefs:
        if n % p == 0:
            return p
    raise ValueError(f"no tile for {n}")


def _rope_tables(T):
    half = ATT_DIM // 2
    inv = ROPE_THETA ** (-jnp.arange(half, dtype=F32) / half)
    ang = jnp.arange(T, dtype=F32)[:, None] * inv[None, :]
    cos = jnp.cos(ang)
    sin = jnp.sin(ang)
    return jnp.concatenate([cos, cos], axis=1), jnp.concatenate([-sin, sin], axis=1)


def _prepare_params(p):
    q = dict(p)
    w_in = p['ab_w_in'][0]
    q['w_in'] = jnp.pad(w_in, ((0, 0), (0, MIX_COLS_PAD - MIX_COLS))).astype(BF16)
    mu = p['rwkv_mu'][0]
    q['mu'] = jnp.pad(mu, (0, LORA_PAD - LORA_COLS))[None, :]
    q['lb'] = jnp.cumsum(jax.nn.softmax(p['hgrn_lb'].astype(F32), axis=0), axis=0)
    q['w_out'] = p['ab_w_out'][0].astype(BF16)
    q['w_qkv'] = p['att_w_qkv'][0].astype(BF16)
    q['w_o'] = p['att_w_o'][0].astype(BF16)
    q['w_up'] = p['ffn_w_up'].astype(BF16)
    q['w_down'] = p['ffn_w_down'].astype(BF16)
    return q


def _mixer_layer(x2, q, B, T, layer):
    M = B * T
    tm = _pick(M, (512, 256, 128))
    proj = _norm_matmul(x2, q['mix_norm'][layer][None, :], q['w_in'], tm=tm, tn=MIX_COLS_PAD // 4,
                        out_dtype=F32)
    proj = proj.reshape(B, T, MIX_COLS_PAD)
    lb = q['lb'][layer][None, :]
    tb = _pick(T, (256, 128, 64))
    tb_scan = _pick(T, (512, 256, 128, 64))
    o_fwd = _hgrn_dir(proj, lb, rev=False, tb=tb_scan)
    ya = _hgrn_dir(proj, lb, rev=True, tb=tb_scan, ofwd=o_fwd, onorm=q['hgrn_onorm'][0][None, :])

    r, k2, v, kk, bv, lwf, lwb, g, bonus = _rwkv_prep(
        proj, q['mu'], q['rwkv_w0'][0], q['rwkv_w2'][0], q['rwkv_a0'][0][None, :], q['rwkv_a2'][0],
        q['rwkv_g2'][0], q['rwkv_kk'][0][None, :], q['rwkv_ka'][0][None, :],
        q['rwkv_rk'][0].reshape(1, RWKV_WIDTH), tb=tb)
    y_fwd = _rwkv_dir(r, k2, v, kk, bv, lwf, rev=False, tb=tb_scan)
    yb = _rwkv_dir(r, k2, v, kk, bv, lwb, rev=True, tb=tb_scan,
                   fin=(y_fwd, g, bonus, q['rwkv_ln_w'][0][None, :], q['rwkv_ln_b'][0][None, :]))
    return _matmul_res([ya.reshape(M, HGRN_WIDTH), yb.reshape(M, RWKV_WIDTH)], q['w_out'], x2,
                       tm=tm, tn=D_MODEL)


def _attention_layer(x2, q, B, T, layer, rope_tabs):
    M = B * T
    tm = _pick(M, (512, 256, 128))
    cos, sin = rope_tabs
    qkv = _norm_matmul(x2, q['mix_norm'][layer][None, :], q['w_qkv'], tm=tm, tn=QKV_COLS, out_dtype=BF16,
                       rope=(cos, sin, ATT_HEADS + ATT_KV_HEADS))
    o = _attention(qkv.reshape(B, T, QKV_COLS), q['att_sink'][0])
    return _matmul_res([o.reshape(M, ATT_HEADS * ATT_DIM)], q['w_o'], x2, tm=tm, tn=D_MODEL)


def _trunk(x, q):
    B, T, D = x.shape
    M = B * T
    x2 = x.reshape(M, D)
    rope_tabs = _rope_tables(T)
    tm = _pick(T, (1024, 512, 256, 128))
    for layer in range(DEPTH):
        if layer % 2 == 0:
            x2 = _mixer_layer(x2, q, B, T, layer)
        else:
            x2 = _attention_layer(x2, q, B, T, layer, rope_tabs)
        x2 = _ffn(x2, q['ffn_norm'][layer][None, :], q['w_up'], q['ffn_conv_w'][layer],
                  q['ffn_conv_b'][layer][None, :], q['w_down'], q['final_norm'][None, :],
                  layer=layer, seq_len=T, tm=tm, tf=512, final_norm=(layer == DEPTH - 1))
    return x2.reshape(B, T, D)


def kernel(x_prompt, x_sample, mix_norm, ab_w_in, hgrn_lb, hgrn_onorm, rwkv_mu, rwkv_w0, rwkv_w2, rwkv_a0, rwkv_a2, rwkv_g2, rwkv_kk, rwkv_ka, rwkv_rk, rwkv_ln_w, rwkv_ln_b, ab_w_out, att_w_qkv, att_sink, att_w_o, ffn_norm, ffn_w_up, ffn_conv_w, ffn_conv_b, ffn_w_down, final_norm):
    p = {
        'mix_norm': mix_norm, 'ab_w_in': ab_w_in, 'hgrn_lb': hgrn_lb, 'hgrn_onorm': hgrn_onorm,
        'rwkv_mu': rwkv_mu, 'rwkv_w0': rwkv_w0, 'rwkv_w2': rwkv_w2, 'rwkv_a0': rwkv_a0, 'rwkv_a2': rwkv_a2,
        'rwkv_g2': rwkv_g2, 'rwkv_kk': rwkv_kk, 'rwkv_ka': rwkv_ka, 'rwkv_rk': rwkv_rk,
        'rwkv_ln_w': rwkv_ln_w, 'rwkv_ln_b': rwkv_ln_b, 'ab_w_out': ab_w_out,
        'att_w_qkv': att_w_qkv, 'att_sink': att_sink, 'att_w_o': att_w_o,
        'ffn_norm': ffn_norm, 'ffn_w_up': ffn_w_up, 'ffn_conv_w': ffn_conv_w, 'ffn_conv_b': ffn_conv_b,
        'ffn_w_down': ffn_w_down, 'final_norm': final_norm,
    }
    q = _prepare_params(p)
    return (_trunk(x_prompt, q), _trunk(x_sample, q))
```

```python
import functools

import jax
import jax.numpy as jnp
from jax import lax
from jax.experimental import pallas as pl
from jax.experimental.pallas import tpu as pltpu

F32 = jnp.float32
BF16 = jnp.bfloat16

D_MODEL = 2048
DEPTH = 2
HGRN_DIM = 128
HGRN_HEADS = 8
HGRN_WIDTH = 1024
RWKV_DIM = 64
RWKV_HEADS = 16
RWKV_WIDTH = 1024
DECAY_LORA = 64
AAA_LORA = 64
GATE_LORA = 160
LORA_COLS = 2 * DECAY_LORA + AAA_LORA + GATE_LORA
LORA_PAD = 512
LORA_SLAB = 128
MIX_A_COLS = 5 * HGRN_WIDTH
MIX_COLS = MIX_A_COLS + 3 * RWKV_WIDTH + LORA_COLS
MIX_COLS_PAD = MIX_A_COLS + 3 * RWKV_WIDTH + LORA_PAD
ATT_DIM = 128
ATT_HEADS = 16
ATT_KV_HEADS = 4
ATT_GROUP = 4
QKV_COLS = (ATT_HEADS + 2 * ATT_KV_HEADS) * ATT_DIM
BLOCK = 128
ROPE_THETA = 10000.0
D_FF = 5632
RMS_EPS = 1e-6
GN_EPS = 64e-5
CHUNK = 64
PAIR = 2 * RWKV_DIM
N_PAIRS = RWKV_WIDTH // PAIR
LOG2_E = 1.4426950408889634
EXP_NEG_HALF = 0.6065306597126334
RWKV_GROUP = 8

V7X_VMEM_BYTES = 64 * 1024 * 1024
VMEM_LIMIT = 56 * 1024 * 1024


def _cparams(sem, vmem_limit=VMEM_LIMIT):
    return pltpu.CompilerParams(dimension_semantics=sem, vmem_limit_bytes=vmem_limit)


def _mm(a, b):
    return jnp.dot(a.astype(BF16), b.astype(BF16), preferred_element_type=F32)


def _mm_nt(a, b):
    return lax.dot_general(a.astype(BF16), b.astype(BF16), (((1,), (1,)), ((), ())),
                           preferred_element_type=F32)


def _mm_tn(a, b):
    return lax.dot_general(a.astype(BF16), b.astype(BF16), (((0,), (0,)), ((), ())),
                           preferred_element_type=F32)


def _cumsum_rows(x, rev):
    n = x.shape[0]
    row = lax.broadcasted_iota(jnp.int32, (n, 1), 0)
    s = 1
    while s < n:
        if rev:
            x = x + jnp.where(row < n - s, pltpu.roll(x, n - s, axis=0), 0.0)
        else:
            x = x + jnp.where(row >= s, pltpu.roll(x, s, axis=0), 0.0)
        s *= 2
    return x


_mm_lora = _mm
_mm_neumann = _mm


def _sigmoid(x):
    return 1.0 / (1.0 + jnp.exp(-x))


def _norm_matmul_body(x_ref, g_ref, w_ref, o_ref, h_ref):
    @pl.when(pl.program_id(1) == 0)
    def _():
        x = x_ref[...]
        ms = jnp.mean(x * x, axis=-1, keepdims=True)
        h_ref[...] = (x * lax.rsqrt(ms + RMS_EPS) * g_ref[...]).astype(BF16)

    o_ref[...] = jnp.dot(h_ref[...], w_ref[...], preferred_element_type=F32).astype(o_ref.dtype)


def _norm_matmul_rope_body(x_ref, g_ref, w_ref, cos_ref, sin_ref, o_ref, h_ref, *, n_rope, tn):
    x = x_ref[...]
    ms = jnp.mean(x * x, axis=-1, keepdims=True)
    h_ref[...] = (x * lax.rsqrt(ms + RMS_EPS) * g_ref[...]).astype(BF16)
    cos = cos_ref[...]
    sin = sin_ref[...]
    group = 4 * ATT_DIM
    for gi in range(tn // group):
        acc = jnp.dot(h_ref[...], w_ref[:, gi * group:(gi + 1) * group], preferred_element_type=F32)
        for hh in range(group // ATT_DIM):
            a = acc[:, hh * ATT_DIM:(hh + 1) * ATT_DIM]
            head = gi * (group // ATT_DIM) + hh
            if head < n_rope:
                a = a * cos + pltpu.roll(a, ATT_DIM // 2, axis=1) * sin
            o_ref[:, head * ATT_DIM:(head + 1) * ATT_DIM] = a.astype(o_ref.dtype)


def _norm_matmul(x, g, w, *, tm, tn, out_dtype, rope=None):
    M, K = x.shape
    N = w.shape[1]
    grid = (M // tm, N // tn)
    in_specs = [pl.BlockSpec((tm, K), lambda i, j: (i, 0)),
                pl.BlockSpec((1, K), lambda i, j: (0, 0)),
                pl.BlockSpec((K, tn), lambda i, j: (0, j))]
    args = [x, g, w]
    blocks = 2 * (tm * K * 4 + K * tn * 2 + tm * tn * jnp.dtype(out_dtype).itemsize) + tm * K * 2
    vmem_limit = max(VMEM_LIMIT, min(int(blocks * 1.08), V7X_VMEM_BYTES - (2 << 20)))
    if rope is None:
        body = _norm_matmul_body
    else:
        cos, sin, n_rope = rope
        tiles_per_seq = cos.shape[0] // tm
        in_specs += [pl.BlockSpec((tm, ATT_DIM), lambda i, j: (i % tiles_per_seq, 0)),
                     pl.BlockSpec((tm, ATT_DIM), lambda i, j: (i % tiles_per_seq, 0))]
        args += [cos, sin]
        body = functools.partial(_norm_matmul_rope_body, n_rope=n_rope, tn=tn)
    return pl.pallas_call(
        body,
        out_shape=jax.ShapeDtypeStruct((M, N), out_dtype),
        grid=grid,
        in_specs=in_specs,
        out_specs=pl.BlockSpec((tm, tn), lambda i, j: (i, j)),
        scratch_shapes=[pltpu.VMEM((tm, K), BF16)],
        compiler_params=_cparams(("parallel", "arbitrary"), vmem_limit),
        name="norm_matmul" if rope is None else "norm_matmul_rope",
    )(*args)


def _matmul_res_body(*refs, n_parts):
    a_refs = refs[:n_parts]
    w_ref, r_ref, o_ref = refs[n_parts:]
    acc = r_ref[...]
    k0 = 0
    for a_ref in a_refs:
        kw = a_ref.shape[1]
        acc = acc + jnp.dot(a_ref[...], w_ref[k0:k0 + kw, :], preferred_element_type=F32)
        k0 += kw
    o_ref[...] = acc


def _matmul_res(a_parts, w, res, *, tm, tn):
    M, N = res.shape
    K = w.shape[0]
    in_specs = [pl.BlockSpec((tm, a.shape[1]), lambda i, j: (i, 0)) for a in a_parts]
    in_specs += [pl.BlockSpec((K, tn), lambda i, j: (0, j)),
                 pl.BlockSpec((tm, tn), lambda i, j: (i, j))]
    return pl.pallas_call(
        functools.partial(_matmul_res_body, n_parts=len(a_parts)),
        out_shape=jax.ShapeDtypeStruct((M, N), F32),
        grid=(M // tm, N // tn),
        in_specs=in_specs,
        out_specs=pl.BlockSpec((tm, tn), lambda i, j: (i, j)),
        compiler_params=_cparams(("parallel", "arbitrary")),
        name="matmul_res",
    )(*a_parts, w, res)


def _hgrn_body(*refs, rev, n_chunks, final):
    if final:
        q_ref, i_ref, z_ref, lb_ref, ofwd_ref, g_ref, onorm_ref, o_ref, st_ref = refs
    else:
        q_ref, i_ref, z_ref, lb_ref, o_ref, st_ref = refs

    @pl.when(pl.program_id(1) == 0)
    def _():
        st_ref[...] = jnp.zeros_like(st_ref)

    lb = lb_ref[...]
    row = lax.broadcasted_iota(jnp.int32, (CHUNK, CHUNK), 0)
    col = lax.broadcasted_iota(jnp.int32, (CHUNK, CHUNK), 1)
    keep = (row <= col) if rev else (row >= col)
    last = 0 if rev else CHUNK - 1

    head_slices = [slice(h * HGRN_DIM, (h + 1) * HGRN_DIM) for h in range(HGRN_HEADS)]
    sts = [st_ref[h] for h in range(HGRN_HEADS)]
    order = range(n_chunks - 1, -1, -1) if rev else range(n_chunks)
    for ci in order:
        sl = slice(ci * CHUNK, (ci + 1) * CHUNK)
        f = lb + (1.0 - lb) * _sigmoid(z_ref[sl, :])
        kk = 1.0 - f
        b = _cumsum_rows(jnp.log(f), rev)
        b_last = b[last:last + 1, :]
        q_d = q_ref[sl, :] * jnp.exp(b)
        k_d = kk * jnp.exp(-b)
        k_u = kk * jnp.exp(b_last - b)
        dec = jnp.exp(b_last)
        v = i_ref[sl, :]
        att = [jnp.where(keep, _mm_nt(q_d[:, hs], k_d[:, hs]), 0.0) for hs in head_slices]
        outs = [_mm(att[h], v[:, hs]) + _mm_nt(q_d[:, hs], sts[h]) for h, hs in enumerate(head_slices)]
        upd = [_mm_tn(v[:, hs], k_u[:, hs]) for hs in head_slices]
        sts = [sts[h] * dec[:, hs] + upd[h] for h, hs in enumerate(head_slices)]
        if final:
            g = g_ref[sl, :]
            silu_g = g * _sigmoid(g)
            normed = []
            for h, hs in enumerate(head_slices):
                oa = outs[h] + ofwd_ref[sl, hs]
                normed.append(oa * lax.rsqrt(jnp.mean(oa * oa, axis=-1, keepdims=True) + RMS_EPS))
            o_ref[sl, :] = (jnp.concatenate(normed, axis=1) * onorm_ref[...] * silu_g).astype(o_ref.dtype)
        else:
            o_ref[sl, :] = jnp.concatenate(outs, axis=1)
    st_ref[...] = jnp.stack(sts, axis=0)


def _hgrn_dir(proj, lb, *, rev, tb, ofwd=None, onorm=None):
    B, T, _ = proj.shape
    nt = T // tb
    final = ofwd is not None

    def tmap(n):
        return (nt - 1 - n) if rev else n

    def col(c):
        return pl.BlockSpec((None, tb, HGRN_WIDTH), lambda b, n: (b, tmap(n), c))

    vec = pl.BlockSpec((1, HGRN_WIDTH), lambda b, n: (0, 0))
    in_specs = [col(0), col(1), col(3 if rev else 2), vec]
    args = [proj, proj, proj, lb]
    if final:
        in_specs += [col(0), col(4), vec]
        args += [ofwd, proj, onorm]
    return pl.pallas_call(
        functools.partial(_hgrn_body, rev=rev, n_chunks=tb // CHUNK, final=final),
        out_shape=jax.ShapeDtypeStruct((B, T, HGRN_WIDTH), BF16 if final else F32),
        grid=(B, nt),
        in_specs=in_specs,
        out_specs=pl.BlockSpec((None, tb, HGRN_WIDTH), lambda b, n: (b, tmap(n), 0)),
        scratch_shapes=[pltpu.VMEM((HGRN_HEADS, HGRN_DIM, HGRN_DIM), F32)],
        compiler_params=_cparams(("parallel", "arbitrary")),
        name="hgrn_bwd" if rev else "hgrn_fwd",
    )(*args)


def _head_sum(x):
    m0 = lax.broadcasted_iota(jnp.int32, (x.shape[0], PAIR), 1) < RWKV_DIM
    outs = []
    for p in range(N_PAIRS):
        xs = x[:, p * PAIR:(p + 1) * PAIR]
        s0 = jnp.sum(jnp.where(m0, xs, 0.0), axis=-1, keepdims=True)
        s1 = jnp.sum(jnp.where(m0, 0.0, xs), axis=-1, keepdims=True)
        outs.append(jnp.where(m0, s0, s1))
    return jnp.concatenate(outs, axis=1)


def _rwkv_prep_body(r_ref, k_ref, v_ref, l_ref,
                    rp_ref, kp_ref, vp_ref, lp_ref, rn_ref, kn_ref, vn_ref, ln_ref,
                    mu_r_ref, mu_k_ref, mu_v_ref, mu_l_ref,
                    w0_ref, w2f_ref, w2b_ref, a0_ref, a2_ref, g2_ref, kkw_ref, kaw_ref, rk_ref,
                    ro_ref, ko_ref, vo_ref, kko_ref, bvo_ref, lwf_ref, lwb_ref, go_ref, bo_ref,
                    *, tb):
    n = pl.program_id(1)
    nt = pl.num_programs(1)
    has_prev = jnp.where(n > 0, 1.0, 0.0).astype(F32)
    has_next = jnp.where(n < nt - 1, 1.0, 0.0).astype(F32)
    rows8 = lax.broadcasted_iota(jnp.int32, (8, 1), 0)
    first8 = rows8 == 0
    last8 = rows8 == 7

    def shift(x_ref, p_ref, n_ref, mu_ref):
        x = x_ref[...]
        down = pltpu.roll(x, 1, axis=0)
        up = pltpu.roll(x, tb - 1, axis=0)
        prev = jnp.concatenate([jnp.where(first8, p_ref[7:8, :] * has_prev, down[0:8]), down[8:]], axis=0)
        nxt = jnp.concatenate([up[:tb - 8], jnp.where(last8, n_ref[0:1, :] * has_next, up[tb - 8:])], axis=0)
        mu = mu_ref[...]
        return x * (1.0 - mu) + (prev + nxt) * (0.5 * mu)

    r = shift(r_ref, rp_ref, rn_ref, mu_r_ref)
    k = shift(k_ref, kp_ref, kn_ref, mu_k_ref)
    v = shift(v_ref, vp_ref, vn_ref, mu_v_ref)
    lo = shift(l_ref, lp_ref, ln_ref, mu_l_ref)
    wd = jnp.tanh(lo[:, 0:LORA_SLAB])
    ag = lo[:, LORA_SLAB:3 * LORA_SLAB]

    def log_decay(w0, w2):
        u = w0 + _mm_lora(wd, w2)
        return -EXP_NEG_HALF * _sigmoid(u)

    lwf_ref[...] = log_decay(w0_ref[0:1, :], w2f_ref[...])
    lwb_ref[...] = log_decay(w0_ref[1:2, :], w2b_ref[...])
    a = _sigmoid(a0_ref[...] + _mm_lora(ag[:, 0:LORA_SLAB], a2_ref[...]))
    go_ref[...] = _mm_lora(_sigmoid(ag), g2_ref[...]).astype(go_ref.dtype)

    kk = k * kkw_ref[...]
    norm = jnp.maximum(jnp.sqrt(_head_sum(kk * kk)), 1e-12)
    kk = kk / norm
    k2 = k * (1.0 + (a - 1.0) * kaw_ref[...])
    ro_ref[...] = r.astype(ro_ref.dtype)
    ko_ref[...] = k2.astype(ko_ref.dtype)
    vo_ref[...] = v.astype(vo_ref.dtype)
    kko_ref[...] = kk.astype(kko_ref.dtype)
    bvo_ref[...] = (kk * a).astype(bvo_ref.dtype)
    bo_ref[...] = (_head_sum(r * k2 * rk_ref[...]) * v).astype(bo_ref.dtype)


def _rwkv_prep(proj, mu, w0, w2, a0, a2, g2, kkw, kaw, rk, *, tb):
    B, T, _ = proj.shape
    nt = T // tb
    hb = tb // 8
    last8 = T // 8 - 1

    def main(width, c):
        return pl.BlockSpec((None, tb, width), lambda b, n: (b, n, c))

    def prev(width, c):
        return pl.BlockSpec((None, 8, width), lambda b, n: (b, jnp.maximum(n * hb - 1, 0), c))

    def nxt(width, c):
        return pl.BlockSpec((None, 8, width), lambda b, n: (b, jnp.minimum((n + 1) * hb, last8), c))

    def full(a):
        return pl.BlockSpec(a.shape, lambda b, n: (0,) * a.ndim)

    W = RWKV_WIDTH
    c_l = (MIX_A_COLS + 3 * W) // LORA_PAD
    zpad = lambda a, before, total: jnp.pad(a, ((before, total - before - a.shape[0]), (0, 0)))
    params = [mu[:, 0:W], mu[:, W:2 * W], mu[:, 2 * W:3 * W], mu[:, 3 * W:],
              w0, zpad(w2[0], 0, LORA_SLAB), zpad(w2[1], DECAY_LORA, LORA_SLAB), a0,
              zpad(a2, 0, LORA_SLAB), zpad(g2, AAA_LORA, 2 * LORA_SLAB), kkw, kaw, rk]
    in_specs = ([main(W, 5), main(W, 6), main(W, 7), main(LORA_PAD, c_l),
                 prev(W, 5), prev(W, 6), prev(W, 7), prev(LORA_PAD, c_l),
                 nxt(W, 5), nxt(W, 6), nxt(W, 7), nxt(LORA_PAD, c_l)]
                + [full(p) for p in params])
    out_dtypes = [BF16] * 5 + [F32, F32, BF16, BF16]
    return pl.pallas_call(
        functools.partial(_rwkv_prep_body, tb=tb),
        out_shape=[jax.ShapeDtypeStruct((B, T, W), dt) for dt in out_dtypes],
        grid=(B, nt),
        in_specs=in_specs,
        out_specs=[pl.BlockSpec((None, tb, W), lambda b, n: (b, n, 0))] * 9,
        compiler_params=_cparams(("parallel", "parallel")),
        name="rwkv_prep",
    )(*([proj] * 12), *params)


def _bd(x):
    lane = lax.broadcasted_iota(jnp.int32, x.shape, 1) % PAIR
    m0 = lane < RWKV_DIM
    return jnp.concatenate([jnp.where(m0, x, 0.0), jnp.where(m0, 0.0, x)], axis=0)

def _neumann_inverse(mats):
    n, w = mats[0].shape
    eye = jnp.where(lax.broadcasted_iota(jnp.int32, (n, w), 0) == lax.broadcasted_iota(jnp.int32, (n, w), 1) % n,
                    1.0, 0.0).astype(F32)
    ps = [eye + a for a in mats]
    aks = [_mm_neumann(a, _bd(a)) for a in mats]
    for _ in range(4):
        ss = [_mm_neumann(ak, _bd(jnp.concatenate([ak, p], axis=1))) for p, ak in zip(ps, aks)]
        ps = [p + s[:, w:] for p, s in zip(ps, ss)]
        aks = [s[:, :w] for s in ss]
    return [p + _mm_neumann(ak, _bd(p)) for p, ak in zip(ps, aks)]


def _rwkv_scan_body(*refs, rev, final, n_chunks, group):
    h_ref = refs[-1]

    @pl.when(pl.program_id(1) == 0)
    def _():
        h_ref[...] = jnp.zeros_like(h_ref)

    n_groups = n_chunks // group
    if n_groups == 1:
        _rwkv_group(refs, 0, rev=rev, final=final, group=group)
    else:
        def body(i, carry):
            gi = (n_groups - 1 - i) if rev else i
            _rwkv_group(refs, gi * (group * CHUNK), rev=rev, final=final, group=group)
            return carry

        lax.fori_loop(0, n_groups, body, 0)


def _rwkv_group(refs, base, *, rev, final, group):
    if final:
        (r_ref, k_ref, v_ref, kk_ref, bv_ref, lw_ref, yf_ref, g_ref, bonus_ref, lnw_ref, lnb_ref,
         o_ref, h_ref) = refs
    else:
        r_ref, k_ref, v_ref, kk_ref, bv_ref, lw_ref, o_ref, h_ref = refs

    C = CHUNK
    last = 0 if rev else C - 1
    t_row = lax.broadcasted_iota(jnp.int32, (C, 2 * PAIR), 0)
    t_col = lax.broadcasted_iota(jnp.int32, (C, 2 * PAIR), 1) % C
    strict = (t_row < t_col) if rev else (t_row > t_col)
    incl = (t_row <= t_col) if rev else (t_row >= t_col)
    ch_row = lax.broadcasted_iota(jnp.int32, (PAIR, 2 * PAIR), 0)
    ch_col = lax.broadcasted_iota(jnp.int32, (PAIR, 2 * PAIR), 1) % PAIR
    same_head = (ch_row // RWKV_DIM) == (ch_col // RWKV_DIM)
    eye = (lax.broadcasted_iota(jnp.int32, (PAIR, PAIR), 0)
           == lax.broadcasted_iota(jnp.int32, (PAIR, PAIR), 1))
    zeros = jnp.zeros((C, PAIR), F32)
    pairs = range(N_PAIRS)
    cat = jnp.concatenate

    def rows(c):
        if isinstance(base, int):
            return slice(base + c * C, base + (c + 1) * C)
        return pl.ds(pl.multiple_of(base + c * C, C), C)

    def slabs(x):
        return [x[:, p * PAIR:(p + 1) * PAIR] for p in pairs]

    at, rt, bt, kt, vv, bh, kh, w_tot = [], [], [], [], [], [], [], []
    for c in range(group):
        sl = rows(c)
        lw = lw_ref[sl, :]
        c_inc = _cumsum_rows(lw, rev)
        c_exc = c_inc - lw
        c_tot = c_inc[last:last + 1, :]
        e_ninc = jnp.exp(-c_inc)
        e_hat = jnp.exp(c_tot - c_inc)
        kk = kk_ref[sl, :].astype(F32)
        bv = bv_ref[sl, :].astype(F32)
        k2 = k_ref[sl, :].astype(F32)
        w_tot += slabs(jnp.exp(c_tot))
        at += slabs(-kk * jnp.exp(c_exc))
        rt += slabs(r_ref[sl, :].astype(F32) * jnp.exp(c_inc))
        bt += slabs(bv * e_ninc)
        kt += slabs(k2 * e_ninc)
        vv += slabs(v_ref[sl, :].astype(F32))
        bh += slabs(bv * e_hat)
        kh += slabs(k2 * e_hat)

    items = range(group * N_PAIRS)
    pm = [_mm_nt(cat([at[i], rt[i]], axis=0), cat([_bd(bt[i]), _bd(kt[i])], axis=0)) for i in items]
    a_abk = [jnp.where(strict, pm[i][:C, :], 0.0) for i in items]
    a_rbk = [jnp.where(incl, pm[i][C:, :], 0.0) for i in items]
    tinv = _neumann_inverse([a_abk[i][:, :PAIR] for i in items])
    x1 = [_mm(a_abk[i][:, PAIR:], _bd(vv[i])) for i in items]
    z = [_mm(tinv[i], _bd(cat([at[i], x1[i]], axis=1))) for i in items]
    w2 = [cat([z[i], cat([zeros, vv[i]], axis=1)], axis=0) for i in items]
    mg = [jnp.where(same_head, _mm_tn(cat([bh[i], kh[i]], axis=0), w2[i]), 0.0) for i in items]
    ry = [_mm(a_rbk[i], cat([_bd(z[i]), _bd(w2[i][C:])], axis=0)) for i in items]
    lhs = [cat([rt[i] + ry[i][:, :PAIR], mg[i][:, :PAIR] + jnp.where(eye, w_tot[i], 0.0)], axis=0)
           for i in items]

    hs = [h_ref[p] for p in pairs]
    for c in (range(group - 1, -1, -1) if rev else range(group)):
        sl = rows(c)
        yh = [_mm(lhs[c * N_PAIRS + p], hs[p]) for p in pairs]
        hs = [yh[p][C:] + mg[c * N_PAIRS + p][:, PAIR:] for p in pairs]
        y = cat([yh[p][:C] + ry[c * N_PAIRS + p][:, PAIR:] for p in pairs], axis=1)
        if final:
            y = y + yf_ref[sl, :]
            mean = _head_sum(y) * (1.0 / RWKV_DIM)
            yc = y - mean
            var = _head_sum(yc * yc) * (1.0 / RWKV_DIM)
            yn = yc * lax.rsqrt(var + GN_EPS) * lnw_ref[...] + lnb_ref[...]
            o_ref[sl, :] = ((yn + bonus_ref[sl, :].astype(F32)) * g_ref[sl, :].astype(F32)).astype(o_ref.dtype)
        else:
            o_ref[sl, :] = y
    h_ref[...] = jnp.stack(hs, axis=0)


def _rwkv_dir(r, k2, v, kk, bv, lw, *, rev, tb, fin=None):
    B, T, W = r.shape
    nc = T // tb

    def tmap(n):
        return (nc - 1 - n) if rev else n

    blk = pl.BlockSpec((None, tb, W), lambda b, n: (b, tmap(n), 0))
    vec = pl.BlockSpec((1, W), lambda b, n: (0, 0))
    in_specs = [blk] * 6
    args = [r, k2, v, kk, bv, lw]
    if fin is not None:
        in_specs += [blk, blk, blk, vec, vec]
        args += list(fin)
    return pl.pallas_call(
        functools.partial(_rwkv_scan_body, rev=rev, final=fin is not None, n_chunks=tb // CHUNK,
                          group=min(RWKV_GROUP, tb // CHUNK)),
        out_shape=jax.ShapeDtypeStruct((B, T, W), BF16 if fin is not None else F32),
        grid=(B, nc),
        in_specs=in_specs,
        out_specs=blk,
        scratch_shapes=[pltpu.VMEM((N_PAIRS, PAIR, PAIR), F32)],
        compiler_params=_cparams(("parallel", "arbitrary")),
        name="rwkv_bwd" if rev else "rwkv_fwd",
    )(*args)


def _attn_body(sink_ref, q_ref, kp_ref, kc_ref, kn_ref, vp_ref, vc_ref, vn_ref, o_ref):
    n = pl.program_id(1)
    nb = pl.num_programs(1)
    r = lax.broadcasted_iota(jnp.int32, (BLOCK, 3 * BLOCK), 0)
    c = lax.broadcasted_iota(jnp.int32, (BLOCK, 3 * BLOCK), 1)
    d = c - BLOCK - r
    lo = jnp.where(n > 0, 0, BLOCK)
    hi = jnp.where(n < nb - 1, 3 * BLOCK, 2 * BLOCK)
    valid = (d >= -BLOCK) & (d <= BLOCK) & (c >= lo) & (c < hi)
    scale = ATT_DIM ** -0.5
    for kh in range(ATT_KV_HEADS):
        ks = slice(kh * ATT_DIM, (kh + 1) * ATT_DIM)
        kw = jnp.concatenate([kp_ref[:, ks], kc_ref[:, ks], kn_ref[:, ks]], axis=0)
        vw = jnp.concatenate([vp_ref[:, ks], vc_ref[:, ks], vn_ref[:, ks]], axis=0)
        heads = [kh * ATT_GROUP + g for g in range(ATT_GROUP)]
        qg = jnp.concatenate([q_ref[:, h * ATT_DIM:(h + 1) * ATT_DIM] for h in heads], axis=0)
        s_all = lax.dot_general(qg, kw, (((1,), (1,)), ((), ())), preferred_element_type=F32)
        ps, inv_dens = [], []
        for g, h in enumerate(heads):
            sg = s_all[g * BLOCK:(g + 1) * BLOCK, :]
            s = jnp.concatenate([jnp.where(valid[:, :BLOCK], sg[:, :BLOCK], -jnp.inf), sg[:, BLOCK:2 * BLOCK],
                                 jnp.where(valid[:, 2 * BLOCK:], sg[:, 2 * BLOCK:], -jnp.inf)], axis=1)
            sk = sink_ref[h] * (1.0 / scale)
            m = jnp.maximum(jnp.max(s, axis=-1, keepdims=True), sk)
            e = jnp.exp2((s - m) * (scale * LOG2_E))
            inv_dens.append(1.0 / (jnp.sum(e, axis=-1, keepdims=True) + jnp.exp2((sk - m) * (scale * LOG2_E))))
            ps.append(e.astype(BF16))
        o_all = jnp.dot(jnp.concatenate(ps, axis=0), vw, preferred_element_type=F32)
        for g, h in enumerate(heads):
            o_ref[:, h * ATT_DIM:(h + 1) * ATT_DIM] = (
                o_all[g * BLOCK:(g + 1) * BLOCK, :] * inv_dens[g]).astype(o_ref.dtype)


def _attention(qkv, sink):
    B, T, _ = qkv.shape
    nb = T // BLOCK
    kvw = ATT_KV_HEADS * ATT_DIM
    kc = (ATT_HEADS * ATT_DIM) // kvw
    vc = kc + 1

    def blk(cidx, off):
        def imap(b, n):
            return (b, jnp.clip(n + off, 0, nb - 1), cidx)
        return pl.BlockSpec((None, BLOCK, kvw), imap)

    return pl.pallas_call(
        _attn_body,
        out_shape=jax.ShapeDtypeStruct((B, T, ATT_HEADS * ATT_DIM), BF16),
        grid=(B, nb),
        in_specs=[pl.BlockSpec(memory_space=pltpu.SMEM),
                  pl.BlockSpec((None, BLOCK, ATT_HEADS * ATT_DIM), lambda b, n: (b, n, 0)),
                  blk(kc, -1), blk(kc, 0), blk(kc, 1), blk(vc, -1), blk(vc, 0), blk(vc, 1)],
        out_specs=pl.BlockSpec((None, BLOCK, ATT_HEADS * ATT_DIM), lambda b, n: (b, n, 0)),
        compiler_params=_cparams(("parallel", "parallel")),
        name="attention",
    )(sink, qkv, qkv, qkv, qkv, qkv, qkv, qkv)


def _ffn_body(x_ref, xp_ref, xn_ref, g_ref, wg_ref, wv_ref, cw_ref, cb_ref, wd_ref, fg_ref, o_ref, h_ref,
              *, tm, tiles_per_seq, final_norm):
    i = pl.program_id(0)
    f = pl.program_id(1)
    nf = pl.num_programs(1)

    t = i % tiles_per_seq
    has_prev = jnp.where(t > 0, 1.0, 0.0).astype(F32)
    has_next = jnp.where(t < tiles_per_seq - 1, 1.0, 0.0).astype(F32)
    rows = lax.broadcasted_iota(jnp.int32, (tm, 1), 0)

    def activation(gm, g_halo, val):
        g_prev = jnp.where(rows == 0, g_halo[7:8, :] * has_prev, pltpu.roll(gm, 1, axis=0))
        g_next = jnp.where(rows == tm - 1, g_halo[8:9, :] * has_next, pltpu.roll(gm, tm - 1, axis=0))
        gate = g_prev * cw_ref[0:1, :] + gm * cw_ref[1:2, :] + g_next * cw_ref[2:3, :] + cb_ref[...]
        return (gate * _sigmoid(gate) * val).astype(BF16)

    @pl.when(f == 0)
    def _():
        def norm(x):
            ms = jnp.mean(x * x, axis=-1, keepdims=True)
            return (x * lax.rsqrt(ms + RMS_EPS) * g_ref[...]).astype(BF16)

        n_chunks = 4
        rc = tm // n_chunks
        h_halo = norm(jnp.concatenate([xp_ref[...], xn_ref[...]], axis=0))
        h_ref[tm:tm + 16, :] = h_halo
        gms, vals, g_halo = [], [], None
        for c in range(n_chunks):
            h_c = norm(x_ref[c * rc:(c + 1) * rc, :])
            h_ref[c * rc:(c + 1) * rc, :] = h_c
            if c == n_chunks - 1:
                ge = jnp.dot(jnp.concatenate([h_c, h_halo], axis=0), wg_ref[...], preferred_element_type=F32)
                gms.append(ge[:rc])
                g_halo = ge[rc:]
            else:
                gms.append(jnp.dot(h_c, wg_ref[...], preferred_element_type=F32))
            vals.append(jnp.dot(h_c, wv_ref[...], preferred_element_type=F32))
        act = activation(jnp.concatenate(gms, axis=0), g_halo, jnp.concatenate(vals, axis=0))
        o_ref[...] = x_ref[...] + jnp.dot(act, wd_ref[...], preferred_element_type=F32)

    @pl.when(f > 0)
    def _():
        ge = jnp.dot(h_ref[...], wg_ref[...], preferred_element_type=F32)
        val = jnp.dot(h_ref[0:tm, :], wv_ref[...], preferred_element_type=F32)
        act = activation(ge[0:tm, :], ge[tm:, :], val)
        o_ref[...] += jnp.dot(act, wd_ref[...], preferred_element_type=F32)

    if final_norm:
        @pl.when(f == nf - 1)
        def _():
            y = o_ref[...]
            ms = jnp.mean(y * y, axis=-1, keepdims=True)
            o_ref[...] = y * lax.rsqrt(ms + RMS_EPS) * fg_ref[...]


def _ffn(x, g, w_up, conv_w, conv_b, w_down, final_g, *, layer, seq_len, tm, tf, final_norm):
    M, D = x.shape
    nf = D_FF // tf
    hb = tm // 8
    last8 = M // 8 - 1
    in_specs = [
        pl.BlockSpec((tm, D), lambda i, f: (i, 0)),
        pl.BlockSpec((8, D), lambda i, f: (jnp.maximum(i * hb - 1, 0), 0)),
        pl.BlockSpec((8, D), lambda i, f: (jnp.minimum((i + 1) * hb, last8), 0)),
        pl.BlockSpec((1, D), lambda i, f: (0, 0)),
        pl.BlockSpec((None, D, tf), lambda i, f: (layer, 0, f)),
        pl.BlockSpec((None, D, tf), lambda i, f: (layer, 0, nf + f)),
        pl.BlockSpec((3, tf), lambda i, f: (0, f)),
        pl.BlockSpec((1, tf), lambda i, f: (0, f)),
        pl.BlockSpec((None, tf, D), lambda i, f: (layer, f, 0)),
        pl.BlockSpec((1, D), lambda i, f: (0, 0)),
    ]
    return pl.pallas_call(
        functools.partial(_ffn_body, tm=tm, tiles_per_seq=seq_len // tm, final_norm=final_norm),
        out_shape=jax.ShapeDtypeStruct((M, D), F32),
        grid=(M // tm, nf),
        in_specs=in_specs,
        out_specs=pl.BlockSpec((tm, D), lambda i, f: (i, 0)),
        scratch_shapes=[pltpu.VMEM((tm + 16, D), BF16)],
        compiler_params=_cparams(("parallel", "arbitrary")),
        name="conv_ffn",
    )(x, x, x, g, w_up, w_up, conv_w, conv_b, w_down, final_g)


def _pick(n, prefs):
    for p in prefs:
        if n % p == 0:
            return p
    raise ValueError(f"no tile for {n}")


def _rope_tables(T):
    half = ATT_DIM // 2
    inv = ROPE_THETA ** (-jnp.arange(half, dtype=F32) / half)
    ang = jnp.arange(T, dtype=F32)[:, None] * inv[None, :]
    cos = jnp.cos(ang)
    sin = jnp.sin(ang)
    return jnp.concatenate([cos, cos], axis=1), jnp.concatenate([-sin, sin], axis=1)


def _prepare_params(p):
    q = dict(p)
    w_in = p['ab_w_in'][0]
    q['w_in'] = jnp.pad(w_in, ((0, 0), (0, MIX_COLS_PAD - MIX_COLS))).astype(BF16)
    mu = p['rwkv_mu'][0]
    q['mu'] = jnp.pad(mu, (0, LORA_PAD - LORA_COLS))[None, :]
    q['lb'] = jnp.cumsum(jax.nn.softmax(p['hgrn_lb'].astype(F32), axis=0), axis=0)
    q['w_out'] = p['ab_w_out'][0].astype(BF16)
    q['w_qkv'] = p['att_w_qkv'][0].astype(BF16)
    q['w_o'] = p['att_w_o'][0].astype(BF16)
    q['w_up'] = p['ffn_w_up'].astype(BF16)
    q['w_down'] = p['ffn_w_down'].astype(BF16)
    return q


def _mixer_layer(x2, q, B, T, layer):
    M = B * T
    tm = _pick(M, (512, 256, 128))
    proj = _norm_matmul(x2, q['mix_norm'][layer][None, :], q['w_in'], tm=_pick(M, (1024, 512, 256, 128)),
                        tn=MIX_COLS_PAD // 4, out_dtype=F32)
    proj = proj.reshape(B, T, MIX_COLS_PAD)
    lb = q['lb'][layer][None, :]
    tb = _pick(T, (256, 128, 64))
    tb_scan = _pick(T, (512, 256, 128, 64))
    o_fwd = _hgrn_dir(proj, lb, rev=False, tb=tb_scan)
    ya = _hgrn_dir(proj, lb, rev=True, tb=tb_scan, ofwd=o_fwd, onorm=q['hgrn_onorm'][0][None, :])

    r, k2, v, kk, bv, lwf, lwb, g, bonus = _rwkv_prep(
        proj, q['mu'], q['rwkv_w0'][0], q['rwkv_w2'][0], q['rwkv_a0'][0][None, :], q['rwkv_a2'][0],
        q['rwkv_g2'][0], q['rwkv_kk'][0][None, :], q['rwkv_ka'][0][None, :],
        q['rwkv_rk'][0].reshape(1, RWKV_WIDTH), tb=tb)
    y_fwd = _rwkv_dir(r, k2, v, kk, bv, lwf, rev=False, tb=tb_scan)
    yb = _rwkv_dir(r, k2, v, kk, bv, lwb, rev=True, tb=tb_scan,
                   fin=(y_fwd, g, bonus, q['rwkv_ln_w'][0][None, :], q['rwkv_ln_b'][0][None, :]))
    return _matmul_res([ya.reshape(M, HGRN_WIDTH), yb.reshape(M, RWKV_WIDTH)], q['w_out'], x2,
                       tm=tm, tn=D_MODEL)


def _attention_layer(x2, q, B, T, layer, rope_tabs):
    M = B * T
    tm = _pick(M, (512, 256, 128))
    cos, sin = rope_tabs
    qkv = _norm_matmul(x2, q['mix_norm'][layer][None, :], q['w_qkv'], tm=tm, tn=QKV_COLS, out_dtype=BF16,
                       rope=(cos, sin, ATT_HEADS + ATT_KV_HEADS))
    o = _attention(qkv.reshape(B, T, QKV_COLS), q['att_sink'][0])
    return _matmul_res([o.reshape(M, ATT_HEADS * ATT_DIM)], q['w_o'], x2, tm=tm, tn=D_MODEL)


def _trunk(x, q):
    B, T, D = x.shape
    M = B * T
    x2 = x.reshape(M, D)
    rope_tabs = _rope_tables(T)
    tm = _pick(T, (1024, 512, 256, 128))
    for layer in range(DEPTH):
        if layer % 2 == 0:
            x2 = _mixer_layer(x2, q, B, T, layer)
        else:
            x2 = _attention_layer(x2, q, B, T, layer, rope_tabs)
        x2 = _ffn(x2, q['ffn_norm'][layer][None, :], q['w_up'], q['ffn_conv_w'][layer],
                  q['ffn_conv_b'][layer][None, :], q['w_down'], q['final_norm'][None, :],
                  layer=layer, seq_len=T, tm=tm, tf=512, final_norm=(layer == DEPTH - 1))
    return x2.reshape(B, T, D)


def kernel(x_prompt, x_sample, mix_norm, ab_w_in, hgrn_lb, hgrn_onorm, rwkv_mu, rwkv_w0, rwkv_w2, rwkv_a0, rwkv_a2, rwkv_g2, rwkv_kk, rwkv_ka, rwkv_rk, rwkv_ln_w, rwkv_ln_b, ab_w_out, att_w_qkv, att_sink, att_w_o, ffn_norm, ffn_w_up, ffn_conv_w, ffn_conv_b, ffn_w_down, final_norm):
    p = {
        'mix_norm': mix_norm, 'ab_w_in': ab_w_in, 'hgrn_lb': hgrn_lb, 'hgrn_onorm': hgrn_onorm,
        'rwkv_mu': rwkv_mu, 'rwkv_w0': rwkv_w0, 'rwkv_w2': rwkv_w2, 'rwkv_a0': rwkv_a0, 'rwkv_a2': rwkv_a2,
        'rwkv_g2': rwkv_g2, 'rwkv_kk': rwkv_kk, 'rwkv_ka': rwkv_ka, 'rwkv_rk': rwkv_rk,
        'rwkv_ln_w': rwkv_ln_w, 'rwkv_ln_b': rwkv_ln_b, 'ab_w_out': ab_w_out,
        'att_w_qkv': att_w_qkv, 'att_sink': att_sink, 'att_w_o': att_w_o,
        'ffn_norm': ffn_norm, 'ffn_w_up': ffn_w_up, 'ffn_conv_w': ffn_conv_w, 'ffn_conv_b': ffn_conv_b,
        'ffn_w_down': ffn_w_down, 'final_norm': final_norm,
    }
    q = _prepare_params(p)
    return (_trunk(x_prompt, q), _trunk(x_sample, q))
```

```python
import functools

import jax
import jax.numpy as jnp
from jax import lax
from jax.experimental import pallas as pl
from jax.experimental.pallas import tpu as pltpu

F32 = jnp.float32
BF16 = jnp.bfloat16

D_MODEL = 2048
DEPTH = 2
HGRN_DIM = 128
HGRN_HEADS = 8
HGRN_WIDTH = 1024
RWKV_DIM = 64
RWKV_HEADS = 16
RWKV_WIDTH = 1024
DECAY_LORA = 64
AAA_LORA = 64
GATE_LORA = 160
LORA_COLS = 2 * DECAY_LORA + AAA_LORA + GATE_LORA
LORA_PAD = 512
LORA_SLAB = 128
MIX_A_COLS = 5 * HGRN_WIDTH
MIX_COLS = MIX_A_COLS + 3 * RWKV_WIDTH + LORA_COLS
MIX_COLS_PAD = MIX_A_COLS + 3 * RWKV_WIDTH + LORA_PAD
ATT_DIM = 128
ATT_HEADS = 16
ATT_KV_HEADS = 4
ATT_GROUP = 4
QKV_COLS = (ATT_HEADS + 2 * ATT_KV_HEADS) * ATT_DIM
BLOCK = 128
ROPE_THETA = 10000.0
D_FF = 5632
RMS_EPS = 1e-6
GN_EPS = 64e-5
CHUNK = 64
PAIR = 2 * RWKV_DIM
N_PAIRS = RWKV_WIDTH // PAIR
LOG2_E = 1.4426950408889634
EXP_NEG_HALF = 0.6065306597126334
RWKV_GROUP = 8

V7X_VMEM_BYTES = 64 * 1024 * 1024
VMEM_LIMIT = 56 * 1024 * 1024


def _cparams(sem, vmem_limit=VMEM_LIMIT):
    return pltpu.CompilerParams(dimension_semantics=sem, vmem_limit_bytes=vmem_limit)


def _mm(a, b):
    return jnp.dot(a.astype(BF16), b.astype(BF16), preferred_element_type=F32)


def _mm_nt(a, b):
    return lax.dot_general(a.astype(BF16), b.astype(BF16), (((1,), (1,)), ((), ())),
                           preferred_element_type=F32)


def _mm_tn(a, b):
    return lax.dot_general(a.astype(BF16), b.astype(BF16), (((0,), (0,)), ((), ())),
                           preferred_element_type=F32)


def _cumsum_rows(x, rev):
    n = x.shape[0]
    row = lax.broadcasted_iota(jnp.int32, (n, 1), 0)
    s = 1
    while s < n:
        if rev:
            x = x + jnp.where(row < n - s, pltpu.roll(x, n - s, axis=0), 0.0)
        else:
            x = x + jnp.where(row >= s, pltpu.roll(x, s, axis=0), 0.0)
        s *= 2
    return x


_mm_lora = _mm
_mm_neumann = _mm


def _sigmoid(x):
    return 1.0 / (1.0 + jnp.exp(-x))


def _norm_matmul_body(x_ref, g_ref, w_ref, o_ref, h_ref):
    @pl.when(pl.program_id(1) == 0)
    def _():
        x = x_ref[...]
        ms = jnp.mean(x * x, axis=-1, keepdims=True)
        h_ref[...] = (x * lax.rsqrt(ms + RMS_EPS) * g_ref[...]).astype(BF16)

    o_ref[...] = jnp.dot(h_ref[...], w_ref[...], preferred_element_type=F32).astype(o_ref.dtype)


def _norm_matmul_rope_body(x_ref, g_ref, w_ref, cos_ref, sin_ref, o_ref, h_ref, *, n_rope, tn):
    x = x_ref[...]
    ms = jnp.mean(x * x, axis=-1, keepdims=True)
    h_ref[...] = (x * lax.rsqrt(ms + RMS_EPS) * g_ref[...]).astype(BF16)
    cos = cos_ref[...]
    sin = sin_ref[...]
    group = 4 * ATT_DIM
    for gi in range(tn // group):
        acc = jnp.dot(h_ref[...], w_ref[:, gi * group:(gi + 1) * group], preferred_element_type=F32)
        for hh in range(group // ATT_DIM):
            a = acc[:, hh * ATT_DIM:(hh + 1) * ATT_DIM]
            head = gi * (group // ATT_DIM) + hh
            if head < n_rope:
                a = a * cos + pltpu.roll(a, ATT_DIM // 2, axis=1) * sin
            o_ref[:, head * ATT_DIM:(head + 1) * ATT_DIM] = a.astype(o_ref.dtype)


def _norm_matmul(x, g, w, *, tm, tn, out_dtype, rope=None):
    M, K = x.shape
    N = w.shape[1]
    grid = (M // tm, N // tn)
    in_specs = [pl.BlockSpec((tm, K), lambda i, j: (i, 0)),
                pl.BlockSpec((1, K), lambda i, j: (0, 0)),
                pl.BlockSpec((K, tn), lambda i, j: (0, j))]
    args = [x, g, w]
    blocks = 2 * (tm * K * 4 + K * tn * 2 + tm * tn * jnp.dtype(out_dtype).itemsize) + tm * K * 2
    vmem_limit = max(VMEM_LIMIT, min(int(blocks * 1.08), V7X_VMEM_BYTES - (2 << 20)))
    if rope is None:
        body = _norm_matmul_body
    else:
        cos, sin, n_rope = rope
        tiles_per_seq = cos.shape[0] // tm
        in_specs += [pl.BlockSpec((tm, ATT_DIM), lambda i, j: (i % tiles_per_seq, 0)),
                     pl.BlockSpec((tm, ATT_DIM), lambda i, j: (i % tiles_per_seq, 0))]
        args += [cos, sin]
        body = functools.partial(_norm_matmul_rope_body, n_rope=n_rope, tn=tn)
    return pl.pallas_call(
        body,
        out_shape=jax.ShapeDtypeStruct((M, N), out_dtype),
        grid=grid,
        in_specs=in_specs,
        out_specs=pl.BlockSpec((tm, tn), lambda i, j: (i, j)),
        scratch_shapes=[pltpu.VMEM((tm, K), BF16)],
        compiler_params=_cparams(("parallel", "arbitrary"), vmem_limit),
        name="norm_matmul" if rope is None else "norm_matmul_rope",
    )(*args)


def _matmul_res_body(*refs, n_parts):
    a_refs = refs[:n_parts]
    w_ref, r_ref, o_ref = refs[n_parts:]
    acc = r_ref[...]
    k0 = 0
    for a_ref in a_refs:
        kw = a_ref.shape[1]
        acc = acc + jnp.dot(a_ref[...], w_ref[k0:k0 + kw, :], preferred_element_type=F32)
        k0 += kw
    o_ref[...] = acc


def _matmul_res(a_parts, w, res, *, tm, tn):
    M, N = res.shape
    K = w.shape[0]
    in_specs = [pl.BlockSpec((tm, a.shape[1]), lambda i, j: (i, 0)) for a in a_parts]
    in_specs += [pl.BlockSpec((K, tn), lambda i, j: (0, j)),
                 pl.BlockSpec((tm, tn), lambda i, j: (i, j))]
    return pl.pallas_call(
        functools.partial(_matmul_res_body, n_parts=len(a_parts)),
        out_shape=jax.ShapeDtypeStruct((M, N), F32),
        grid=(M // tm, N // tn),
        in_specs=in_specs,
        out_specs=pl.BlockSpec((tm, tn), lambda i, j: (i, j)),
        compiler_params=_cparams(("parallel", "arbitrary")),
        name="matmul_res",
    )(*a_parts, w, res)


def _hgrn_body(*refs, rev, n_chunks, final):
    if final:
        q_ref, i_ref, z_ref, lb_ref, ofwd_ref, g_ref, onorm_ref, o_ref, st_ref = refs
    else:
        q_ref, i_ref, z_ref, lb_ref, o_ref, st_ref = refs

    @pl.when(pl.program_id(1) == 0)
    def _():
        st_ref[...] = jnp.zeros_like(st_ref)

    lb = lb_ref[...]
    row = lax.broadcasted_iota(jnp.int32, (CHUNK, CHUNK), 0)
    col = lax.broadcasted_iota(jnp.int32, (CHUNK, CHUNK), 1)
    keep = (row <= col) if rev else (row >= col)
    last = 0 if rev else CHUNK - 1

    head_slices = [slice(h * HGRN_DIM, (h + 1) * HGRN_DIM) for h in range(HGRN_HEADS)]
    sts = [st_ref[h] for h in range(HGRN_HEADS)]
    order = range(n_chunks - 1, -1, -1) if rev else range(n_chunks)
    for ci in order:
        sl = slice(ci * CHUNK, (ci + 1) * CHUNK)
        f = lb + (1.0 - lb) * _sigmoid(z_ref[sl, :])
        kk = 1.0 - f
        b = _cumsum_rows(jnp.log(f), rev)
        b_last = b[last:last + 1, :]
        q_d = q_ref[sl, :] * jnp.exp(b)
        k_d = kk * jnp.exp(-b)
        k_u = kk * jnp.exp(b_last - b)
        dec = jnp.exp(b_last)
        v = i_ref[sl, :]
        att = [jnp.where(keep, _mm_nt(q_d[:, hs], k_d[:, hs]), 0.0) for hs in head_slices]
        outs = [_mm(att[h], v[:, hs]) + _mm_nt(q_d[:, hs], sts[h]) for h, hs in enumerate(head_slices)]
        upd = [_mm_tn(v[:, hs], k_u[:, hs]) for hs in head_slices]
        sts = [sts[h] * dec[:, hs] + upd[h] for h, hs in enumerate(head_slices)]
        if final:
            g = g_ref[sl, :]
            silu_g = g * _sigmoid(g)
            normed = []
            for h, hs in enumerate(head_slices):
                oa = outs[h] + ofwd_ref[sl, hs].astype(F32)
                normed.append(oa * lax.rsqrt(jnp.mean(oa * oa, axis=-1, keepdims=True) + RMS_EPS))
            o_ref[sl, :] = (jnp.concatenate(normed, axis=1) * onorm_ref[...] * silu_g).astype(o_ref.dtype)
        else:
            o_ref[sl, :] = jnp.concatenate(outs, axis=1).astype(o_ref.dtype)
    st_ref[...] = jnp.stack(sts, axis=0)


def _hgrn_dir(proj, lb, *, rev, tb, ofwd=None, onorm=None):
    B, T, _ = proj.shape
    nt = T // tb
    final = ofwd is not None

    def tmap(n):
        return (nt - 1 - n) if rev else n

    def col(c):
        return pl.BlockSpec((None, tb, HGRN_WIDTH), lambda b, n: (b, tmap(n), c))

    vec = pl.BlockSpec((1, HGRN_WIDTH), lambda b, n: (0, 0))
    in_specs = [col(0), col(1), col(3 if rev else 2), vec]
    args = [proj, proj, proj, lb]
    if final:
        in_specs += [col(0), col(4), vec]
        args += [ofwd, proj, onorm]
    return pl.pallas_call(
        functools.partial(_hgrn_body, rev=rev, n_chunks=tb // CHUNK, final=final),
        out_shape=jax.ShapeDtypeStruct((B, T, HGRN_WIDTH), BF16),
        grid=(B, nt),
        in_specs=in_specs,
        out_specs=pl.BlockSpec((None, tb, HGRN_WIDTH), lambda b, n: (b, tmap(n), 0)),
        scratch_shapes=[pltpu.VMEM((HGRN_HEADS, HGRN_DIM, HGRN_DIM), F32)],
        compiler_params=_cparams(("parallel", "arbitrary")),
        name="hgrn_bwd" if rev else "hgrn_fwd",
    )(*args)


def _head_sum(x):
    m0 = lax.broadcasted_iota(jnp.int32, (x.shape[0], PAIR), 1) < RWKV_DIM
    outs = []
    for p in range(N_PAIRS):
        xs = x[:, p * PAIR:(p + 1) * PAIR]
        s0 = jnp.sum(jnp.where(m0, xs, 0.0), axis=-1, keepdims=True)
        s1 = jnp.sum(jnp.where(m0, 0.0, xs), axis=-1, keepdims=True)
        outs.append(jnp.where(m0, s0, s1))
    return jnp.concatenate(outs, axis=1)


def _rwkv_prep_body(r_ref, k_ref, v_ref, l_ref,
                    rp_ref, kp_ref, vp_ref, lp_ref, rn_ref, kn_ref, vn_ref, ln_ref,
                    mu_r_ref, mu_k_ref, mu_v_ref, mu_l_ref,
                    w0_ref, w2f_ref, w2b_ref, a0_ref, a2_ref, g2_ref, kkw_ref, kaw_ref, rk_ref,
                    ro_ref, ko_ref, vo_ref, kko_ref, bvo_ref, lwf_ref, lwb_ref, go_ref, bo_ref,
                    *, tb):
    n = pl.program_id(1)
    nt = pl.num_programs(1)
    has_prev = jnp.where(n > 0, 1.0, 0.0).astype(F32)
    has_next = jnp.where(n < nt - 1, 1.0, 0.0).astype(F32)
    rows8 = lax.broadcasted_iota(jnp.int32, (8, 1), 0)
    first8 = rows8 == 0
    last8 = rows8 == 7

    def shift(x_ref, p_ref, n_ref, mu_ref):
        x = x_ref[...]
        down = pltpu.roll(x, 1, axis=0)
        up = pltpu.roll(x, tb - 1, axis=0)
        prev = jnp.concatenate([jnp.where(first8, p_ref[7:8, :] * has_prev, down[0:8]), down[8:]], axis=0)
        nxt = jnp.concatenate([up[:tb - 8], jnp.where(last8, n_ref[0:1, :] * has_next, up[tb - 8:])], axis=0)
        mu = mu_ref[...]
        return x * (1.0 - mu) + (prev + nxt) * (0.5 * mu)

    r = shift(r_ref, rp_ref, rn_ref, mu_r_ref)
    k = shift(k_ref, kp_ref, kn_ref, mu_k_ref)
    v = shift(v_ref, vp_ref, vn_ref, mu_v_ref)
    lo = shift(l_ref, lp_ref, ln_ref, mu_l_ref)
    wd = jnp.tanh(lo[:, 0:LORA_SLAB])
    ag = lo[:, LORA_SLAB:3 * LORA_SLAB]

    def log_decay(w0, w2):
        u = w0 + _mm_lora(wd, w2)
        return -EXP_NEG_HALF * _sigmoid(u)

    lwf_ref[...] = log_decay(w0_ref[0:1, :], w2f_ref[...])
    lwb_ref[...] = log_decay(w0_ref[1:2, :], w2b_ref[...])
    a = _sigmoid(a0_ref[...] + _mm_lora(ag[:, 0:LORA_SLAB], a2_ref[...]))
    go_ref[...] = _mm_lora(_sigmoid(ag), g2_ref[...]).astype(go_ref.dtype)

    kk = k * kkw_ref[...]
    norm = jnp.maximum(jnp.sqrt(_head_sum(kk * kk)), 1e-12)
    kk = kk / norm
    k2 = k * (1.0 + (a - 1.0) * kaw_ref[...])
    ro_ref[...] = r.astype(ro_ref.dtype)
    ko_ref[...] = k2.astype(ko_ref.dtype)
    vo_ref[...] = v.astype(vo_ref.dtype)
    kko_ref[...] = kk.astype(kko_ref.dtype)
    bvo_ref[...] = (kk * a).astype(bvo_ref.dtype)
    bo_ref[...] = (_head_sum(r * k2 * rk_ref[...]) * v).astype(bo_ref.dtype)


def _rwkv_prep(proj, mu, w0, w2, a0, a2, g2, kkw, kaw, rk, *, tb):
    B, T, _ = proj.shape
    nt = T // tb
    hb = tb // 8
    last8 = T // 8 - 1

    def main(width, c):
        return pl.BlockSpec((None, tb, width), lambda b, n: (b, n, c))

    def prev(width, c):
        return pl.BlockSpec((None, 8, width), lambda b, n: (b, jnp.maximum(n * hb - 1, 0), c))

    def nxt(width, c):
        return pl.BlockSpec((None, 8, width), lambda b, n: (b, jnp.minimum((n + 1) * hb, last8), c))

    def full(a):
        return pl.BlockSpec(a.shape, lambda b, n: (0,) * a.ndim)

    W = RWKV_WIDTH
    c_l = (MIX_A_COLS + 3 * W) // LORA_PAD
    zpad = lambda a, before, total: jnp.pad(a, ((before, total - before - a.shape[0]), (0, 0)))
    params = [mu[:, 0:W], mu[:, W:2 * W], mu[:, 2 * W:3 * W], mu[:, 3 * W:],
              w0, zpad(w2[0], 0, LORA_SLAB), zpad(w2[1], DECAY_LORA, LORA_SLAB), a0,
              zpad(a2, 0, LORA_SLAB), zpad(g2, AAA_LORA, 2 * LORA_SLAB), kkw, kaw, rk]
    in_specs = ([main(W, 5), main(W, 6), main(W, 7), main(LORA_PAD, c_l),
                 prev(W, 5), prev(W, 6), prev(W, 7), prev(LORA_PAD, c_l),
                 nxt(W, 5), nxt(W, 6), nxt(W, 7), nxt(LORA_PAD, c_l)]
                + [full(p) for p in params])
    out_dtypes = [BF16] * 5 + [F32, F32, BF16, BF16]
    return pl.pallas_call(
        functools.partial(_rwkv_prep_body, tb=tb),
        out_shape=[jax.ShapeDtypeStruct((B, T, W), dt) for dt in out_dtypes],
        grid=(B, nt),
        in_specs=in_specs,
        out_specs=[pl.BlockSpec((None, tb, W), lambda b, n: (b, n, 0))] * 9,
        compiler_params=_cparams(("parallel", "parallel")),
        name="rwkv_prep",
    )(*([proj] * 12), *params)


def _bd(x):
    lane = lax.broadcasted_iota(jnp.int32, x.shape, 1) % PAIR
    m0 = lane < RWKV_DIM
    return jnp.concatenate([jnp.where(m0, x, 0.0), jnp.where(m0, 0.0, x)], axis=0)

def _neumann_inverse(mats):
    n, w = mats[0].shape
    eye = jnp.where(lax.broadcasted_iota(jnp.int32, (n, w), 0) == lax.broadcasted_iota(jnp.int32, (n, w), 1) % n,
                    1.0, 0.0).astype(F32)
    ps = [eye + a for a in mats]
    aks = [_mm_neumann(a, _bd(a)) for a in mats]
    for _ in range(4):
        ss = [_mm_neumann(ak, _bd(jnp.concatenate([ak, p], axis=1))) for p, ak in zip(ps, aks)]
        ps = [p + s[:, w:] for p, s in zip(ps, ss)]
        aks = [s[:, :w] for s in ss]
    return [p + _mm_neumann(ak, _bd(p)) for p, ak in zip(ps, aks)]


def _rwkv_scan_body(*refs, rev, final, n_chunks, group):
    h_ref = refs[-1]

    @pl.when(pl.program_id(1) == 0)
    def _():
        h_ref[...] = jnp.zeros_like(h_ref)

    n_groups = n_chunks // group
    if n_groups == 1:
        _rwkv_group(refs, 0, rev=rev, final=final, group=group)
    else:
        def body(i, carry):
            gi = (n_groups - 1 - i) if rev else i
            _rwkv_group(refs, gi * (group * CHUNK), rev=rev, final=final, group=group)
            return carry

        lax.fori_loop(0, n_groups, body, 0)


def _rwkv_group(refs, base, *, rev, final, group):
    if final:
        (r_ref, k_ref, v_ref, kk_ref, bv_ref, lw_ref, yf_ref, g_ref, bonus_ref, lnw_ref, lnb_ref,
         o_ref, h_ref) = refs
    else:
        r_ref, k_ref, v_ref, kk_ref, bv_ref, lw_ref, o_ref, h_ref = refs

    C = CHUNK
    last = 0 if rev else C - 1
    t_row = lax.broadcasted_iota(jnp.int32, (C, 2 * PAIR), 0)
    t_col = lax.broadcasted_iota(jnp.int32, (C, 2 * PAIR), 1) % C
    strict = (t_row < t_col) if rev else (t_row > t_col)
    incl = (t_row <= t_col) if rev else (t_row >= t_col)
    ch_row = lax.broadcasted_iota(jnp.int32, (PAIR, 2 * PAIR), 0)
    ch_col = lax.broadcasted_iota(jnp.int32, (PAIR, 2 * PAIR), 1) % PAIR
    same_head = (ch_row // RWKV_DIM) == (ch_col // RWKV_DIM)
    eye = (lax.broadcasted_iota(jnp.int32, (PAIR, PAIR), 0)
           == lax.broadcasted_iota(jnp.int32, (PAIR, PAIR), 1))
    zeros = jnp.zeros((C, PAIR), F32)
    pairs = range(N_PAIRS)
    cat = jnp.concatenate

    def rows(c):
        if isinstance(base, int):
            return slice(base + c * C, base + (c + 1) * C)
        return pl.ds(pl.multiple_of(base + c * C, C), C)

    def slabs(x):
        return [x[:, p * PAIR:(p + 1) * PAIR] for p in pairs]

    at, rt, bt, kt, vv, bh, kh, w_tot = [], [], [], [], [], [], [], []
    for c in range(group):
        sl = rows(c)
        lw = lw_ref[sl, :]
        c_inc = _cumsum_rows(lw, rev)
        c_exc = c_inc - lw
        c_tot = c_inc[last:last + 1, :]
        e_ninc = jnp.exp(-c_inc)
        e_hat = jnp.exp(c_tot - c_inc)
        kk = kk_ref[sl, :].astype(F32)
        bv = bv_ref[sl, :].astype(F32)
        k2 = k_ref[sl, :].astype(F32)
        w_tot += slabs(jnp.exp(c_tot))
        at += slabs(-kk * jnp.exp(c_exc))
        rt += slabs(r_ref[sl, :].astype(F32) * jnp.exp(c_inc))
        bt += slabs(bv * e_ninc)
        kt += slabs(k2 * e_ninc)
        vv += slabs(v_ref[sl, :].astype(F32))
        bh += slabs(bv * e_hat)
        kh += slabs(k2 * e_hat)

    items = range(group * N_PAIRS)
    pm = [_mm_nt(cat([at[i], rt[i]], axis=0), cat([_bd(bt[i]), _bd(kt[i])], axis=0)) for i in items]
    a_abk = [jnp.where(strict, pm[i][:C, :], 0.0) for i in items]
    a_rbk = [jnp.where(incl, pm[i][C:, :], 0.0) for i in items]
    tinv = _neumann_inverse([a_abk[i][:, :PAIR] for i in items])
    x1 = [_mm(a_abk[i][:, PAIR:], _bd(vv[i])) for i in items]
    z = [_mm(tinv[i], _bd(cat([at[i], x1[i]], axis=1))) for i in items]
    w2 = [cat([z[i], cat([zeros, vv[i]], axis=1)], axis=0) for i in items]
    mg = [jnp.where(same_head, _mm_tn(cat([bh[i], kh[i]], axis=0), w2[i]), 0.0) for i in items]
    ry = [_mm(a_rbk[i], cat([_bd(z[i]), _bd(w2[i][C:])], axis=0)) for i in items]
    lhs = [cat([rt[i] + ry[i][:, :PAIR], mg[i][:, :PAIR] + jnp.where(eye, w_tot[i], 0.0)], axis=0)
           for i in items]

    hs = [h_ref[p] for p in pairs]
    for c in (range(group - 1, -1, -1) if rev else range(group)):
        sl = rows(c)
        yh = [_mm(lhs[c * N_PAIRS + p], hs[p]) for p in pairs]
        hs = [yh[p][C:] + mg[c * N_PAIRS + p][:, PAIR:] for p in pairs]
        y = cat([yh[p][:C] + ry[c * N_PAIRS + p][:, PAIR:] for p in pairs], axis=1)
        if final:
            y = y + yf_ref[sl, :]
            mean = _head_sum(y) * (1.0 / RWKV_DIM)
            yc = y - mean
            var = _head_sum(yc * yc) * (1.0 / RWKV_DIM)
            yn = yc * lax.rsqrt(var + GN_EPS) * lnw_ref[...] + lnb_ref[...]
            o_ref[sl, :] = ((yn + bonus_ref[sl, :].astype(F32)) * g_ref[sl, :].astype(F32)).astype(o_ref.dtype)
        else:
            o_ref[sl, :] = y
    h_ref[...] = jnp.stack(hs, axis=0)


def _rwkv_dir(r, k2, v, kk, bv, lw, *, rev, tb, fin=None):
    B, T, W = r.shape
    nc = T // tb

    def tmap(n):
        return (nc - 1 - n) if rev else n

    blk = pl.BlockSpec((None, tb, W), lambda b, n: (b, tmap(n), 0))
    vec = pl.BlockSpec((1, W), lambda b, n: (0, 0))
    in_specs = [blk] * 6
    args = [r, k2, v, kk, bv, lw]
    if fin is not None:
        in_specs += [blk, blk, blk, vec, vec]
        args += list(fin)
    return pl.pallas_call(
        functools.partial(_rwkv_scan_body, rev=rev, final=fin is not None, n_chunks=tb // CHUNK,
                          group=min(RWKV_GROUP, tb // CHUNK)),
        out_shape=jax.ShapeDtypeStruct((B, T, W), BF16 if fin is not None else F32),
        grid=(B, nc),
        in_specs=in_specs,
        out_specs=blk,
        scratch_shapes=[pltpu.VMEM((N_PAIRS, PAIR, PAIR), F32)],
        compiler_params=_cparams(("parallel", "arbitrary")),
        name="rwkv_bwd" if rev else "rwkv_fwd",
    )(*args)


def _attn_body(sink_ref, q_ref, kp_ref, kc_ref, kn_ref, vp_ref, vc_ref, vn_ref, o_ref):
    n = pl.program_id(1)
    nb = pl.num_programs(1)
    r = lax.broadcasted_iota(jnp.int32, (BLOCK, 3 * BLOCK), 0)
    c = lax.broadcasted_iota(jnp.int32, (BLOCK, 3 * BLOCK), 1)
    d = c - BLOCK - r
    lo = jnp.where(n > 0, 0, BLOCK)
    hi = jnp.where(n < nb - 1, 3 * BLOCK, 2 * BLOCK)
    valid = (d >= -BLOCK) & (d <= BLOCK) & (c >= lo) & (c < hi)
    scale = ATT_DIM ** -0.5
    for kh in range(ATT_KV_HEADS):
        ks = slice(kh * ATT_DIM, (kh + 1) * ATT_DIM)
        kw = jnp.concatenate([kp_ref[:, ks], kc_ref[:, ks], kn_ref[:, ks]], axis=0)
        vw = jnp.concatenate([vp_ref[:, ks], vc_ref[:, ks], vn_ref[:, ks]], axis=0)
        heads = [kh * ATT_GROUP + g for g in range(ATT_GROUP)]
        qg = jnp.concatenate([q_ref[:, h * ATT_DIM:(h + 1) * ATT_DIM] for h in heads], axis=0)
        s_all = lax.dot_general(qg, kw, (((1,), (1,)), ((), ())), preferred_element_type=F32)
        ps, inv_dens = [], []
        for g, h in enumerate(heads):
            sg = s_all[g * BLOCK:(g + 1) * BLOCK, :]
            s = jnp.concatenate([jnp.where(valid[:, :BLOCK], sg[:, :BLOCK], -jnp.inf), sg[:, BLOCK:2 * BLOCK],
                                 jnp.where(valid[:, 2 * BLOCK:], sg[:, 2 * BLOCK:], -jnp.inf)], axis=1)
            sk = sink_ref[h] * (1.0 / scale)
            m = jnp.maximum(jnp.max(s, axis=-1, keepdims=True), sk)
            e = jnp.exp2((s - m) * (scale * LOG2_E))
            inv_dens.append(1.0 / (jnp.sum(e, axis=-1, keepdims=True) + jnp.exp2((sk - m) * (scale * LOG2_E))))
            ps.append(e.astype(BF16))
        o_all = jnp.dot(jnp.concatenate(ps, axis=0), vw, preferred_element_type=F32)
        for g, h in enumerate(heads):
            o_ref[:, h * ATT_DIM:(h + 1) * ATT_DIM] = (
                o_all[g * BLOCK:(g + 1) * BLOCK, :] * inv_dens[g]).astype(o_ref.dtype)


def _attention(qkv, sink):
    B, T, _ = qkv.shape
    nb = T // BLOCK
    kvw = ATT_KV_HEADS * ATT_DIM
    kc = (ATT_HEADS * ATT_DIM) // kvw
    vc = kc + 1

    def blk(cidx, off):
        def imap(b, n):
            return (b, jnp.clip(n + off, 0, nb - 1), cidx)
        return pl.BlockSpec((None, BLOCK, kvw), imap)

    return pl.pallas_call(
        _attn_body,
        out_shape=jax.ShapeDtypeStruct((B, T, ATT_HEADS * ATT_DIM), BF16),
        grid=(B, nb),
        in_specs=[pl.BlockSpec(memory_space=pltpu.SMEM),
                  pl.BlockSpec((None, BLOCK, ATT_HEADS * ATT_DIM), lambda b, n: (b, n, 0)),
                  blk(kc, -1), blk(kc, 0), blk(kc, 1), blk(vc, -1), blk(vc, 0), blk(vc, 1)],
        out_specs=pl.BlockSpec((None, BLOCK, ATT_HEADS * ATT_DIM), lambda b, n: (b, n, 0)),
        compiler_params=_cparams(("parallel", "parallel")),
        name="attention",
    )(sink, qkv, qkv, qkv, qkv, qkv, qkv, qkv)


def _ffn_body(x_ref, xp_ref, xn_ref, g_ref, wg_ref, wv_ref, cw_ref, cb_ref, wd_ref, fg_ref, o_ref, h_ref,
              *, tm, tiles_per_seq, final_norm):
    i = pl.program_id(0)
    f = pl.program_id(1)
    nf = pl.num_programs(1)

    t = i % tiles_per_seq
    has_prev = jnp.where(t > 0, 1.0, 0.0).astype(F32)
    has_next = jnp.where(t < tiles_per_seq - 1, 1.0, 0.0).astype(F32)
    rows = lax.broadcasted_iota(jnp.int32, (tm, 1), 0)

    def activation(gm, g_halo, val):
        g_prev = jnp.where(rows == 0, g_halo[7:8, :] * has_prev, pltpu.roll(gm, 1, axis=0))
        g_next = jnp.where(rows == tm - 1, g_halo[8:9, :] * has_next, pltpu.roll(gm, tm - 1, axis=0))
        gate = g_prev * cw_ref[0:1, :] + gm * cw_ref[1:2, :] + g_next * cw_ref[2:3, :] + cb_ref[...]
        return (gate * _sigmoid(gate) * val).astype(BF16)

    @pl.when(f == 0)
    def _():
        def norm(x):
            ms = jnp.mean(x * x, axis=-1, keepdims=True)
            return (x * lax.rsqrt(ms + RMS_EPS) * g_ref[...]).astype(BF16)

        n_chunks = 4
        rc = tm // n_chunks
        h_halo = norm(jnp.concatenate([xp_ref[...], xn_ref[...]], axis=0))
        h_ref[tm:tm + 16, :] = h_halo
        gms, vals, g_halo = [], [], None
        for c in range(n_chunks):
            h_c = norm(x_ref[c * rc:(c + 1) * rc, :])
            h_ref[c * rc:(c + 1) * rc, :] = h_c
            if c == n_chunks - 1:
                ge = jnp.dot(jnp.concatenate([h_c, h_halo], axis=0), wg_ref[...], preferred_element_type=F32)
                gms.append(ge[:rc])
                g_halo = ge[rc:]
            else:
                gms.append(jnp.dot(h_c, wg_ref[...], preferred_element_type=F32))
            vals.append(jnp.dot(h_c, wv_ref[...], preferred_element_type=F32))
        act = activation(jnp.concatenate(gms, axis=0), g_halo, jnp.concatenate(vals, axis=0))
        o_ref[...] = x_ref[...] + jnp.dot(act, wd_ref[...], preferred_element_type=F32)

    @pl.when(f > 0)
    def _():
        ge = jnp.dot(h_ref[...], wg_ref[...], preferred_element_type=F32)
        val = jnp.dot(h_ref[0:tm, :], wv_ref[...], preferred_element_type=F32)
        act = activation(ge[0:tm, :], ge[tm:, :], val)
        o_ref[...] += jnp.dot(act, wd_ref[...], preferred_element_type=F32)

    if final_norm:
        @pl.when(f == nf - 1)
        def _():
            y = o_ref[...]
            ms = jnp.mean(y * y, axis=-1, keepdims=True)
            o_ref[...] = y * lax.rsqrt(ms + RMS_EPS) * fg_ref[...]


def _ffn(x, g, w_up, conv_w, conv_b, w_down, final_g, *, layer, seq_len, tm, tf, final_norm):
    M, D = x.shape
    nf = D_FF // tf
    hb = tm // 8
    last8 = M // 8 - 1
    in_specs = [
        pl.BlockSpec((tm, D), lambda i, f: (i, 0)),
        pl.BlockSpec((8, D), lambda i, f: (jnp.maximum(i * hb - 1, 0), 0)),
        pl.BlockSpec((8, D), lambda i, f: (jnp.minimum((i + 1) * hb, last8), 0)),
        pl.BlockSpec((1, D), lambda i, f: (0, 0)),
        pl.BlockSpec((None, D, tf), lambda i, f: (layer, 0, f)),
        pl.BlockSpec((None, D, tf), lambda i, f: (layer, 0, nf + f)),
        pl.BlockSpec((3, tf), lambda i, f: (0, f)),
        pl.BlockSpec((1, tf), lambda i, f: (0, f)),
        pl.BlockSpec((None, tf, D), lambda i, f: (layer, f, 0)),
        pl.BlockSpec((1, D), lambda i, f: (0, 0)),
    ]
    return pl.pallas_call(
        functools.partial(_ffn_body, tm=tm, tiles_per_seq=seq_len // tm, final_norm=final_norm),
        out_shape=jax.ShapeDtypeStruct((M, D), F32),
        grid=(M // tm, nf),
        in_specs=in_specs,
        out_specs=pl.BlockSpec((tm, D), lambda i, f: (i, 0)),
        scratch_shapes=[pltpu.VMEM((tm + 16, D), BF16)],
        compiler_params=_cparams(("parallel", "arbitrary")),
        name="conv_ffn",
    )(x, x, x, g, w_up, w_up, conv_w, conv_b, w_down, final_g)


def _pick(n, prefs):
    for p in prefs:
        if n % p == 0:
            return p
    raise ValueError(f"no tile for {n}")


def _rope_tables(T):
    half = ATT_DIM // 2
    inv = ROPE_THETA ** (-jnp.arange(half, dtype=F32) / half)
    ang = jnp.arange(T, dtype=F32)[:, None] * inv[None, :]
    cos = jnp.cos(ang)
    sin = jnp.sin(ang)
    return jnp.concatenate([cos, cos], axis=1), jnp.concatenate([-sin, sin], axis=1)


def _prepare_params(p):
    q = dict(p)
    w_in = p['ab_w_in'][0]
    q['w_in'] = jnp.pad(w_in, ((0, 0), (0, MIX_COLS_PAD - MIX_COLS))).astype(BF16)
    mu = p['rwkv_mu'][0]
    q['mu'] = jnp.pad(mu, (0, LORA_PAD - LORA_COLS))[None, :]
    q['lb'] = jnp.cumsum(jax.nn.softmax(p['hgrn_lb'].astype(F32), axis=0), axis=0)
    q['w_out'] = p['ab_w_out'][0].astype(BF16)
    q['w_qkv'] = p['att_w_qkv'][0].astype(BF16)
    q['w_o'] = p['att_w_o'][0].astype(BF16)
    q['w_up'] = p['ffn_w_up'].astype(BF16)
    q['w_down'] = p['ffn_w_down'].astype(BF16)
    return q


def _mixer_layer(x2, q, B, T, layer):
    M = B * T
    tm = _pick(M, (512, 256, 128))
    proj = _norm_matmul(x2, q['mix_norm'][layer][None, :], q['w_in'], tm=_pick(M, (1024, 512, 256, 128)),
                        tn=MIX_COLS_PAD // 4, out_dtype=F32)
    proj = proj.reshape(B, T, MIX_COLS_PAD)
    lb = q['lb'][layer][None, :]
    tb = _pick(T, (256, 128, 64))
    tb_scan = _pick(T, (512, 256, 128, 64))
    o_fwd = _hgrn_dir(proj, lb, rev=False, tb=tb_scan)
    ya = _hgrn_dir(proj, lb, rev=True, tb=tb_scan, ofwd=o_fwd, onorm=q['hgrn_onorm'][0][None, :])

    r, k2, v, kk, bv, lwf, lwb, g, bonus = _rwkv_prep(
        proj, q['mu'], q['rwkv_w0'][0], q['rwkv_w2'][0], q['rwkv_a0'][0][None, :], q['rwkv_a2'][0],
        q['rwkv_g2'][0], q['rwkv_kk'][0][None, :], q['rwkv_ka'][0][None, :],
        q['rwkv_rk'][0].reshape(1, RWKV_WIDTH), tb=tb)
    y_fwd = _rwkv_dir(r, k2, v, kk, bv, lwf, rev=False, tb=tb_scan)
    yb = _rwkv_dir(r, k2, v, kk, bv, lwb, rev=True, tb=tb_scan,
                   fin=(y_fwd, g, bonus, q['rwkv_ln_w'][0][None, :], q['rwkv_ln_b'][0][None, :]))
    return _matmul_res([ya.reshape(M, HGRN_WIDTH), yb.reshape(M, RWKV_WIDTH)], q['w_out'], x2,
                       tm=tm, tn=D_MODEL)


def _attention_layer(x2, q, B, T, layer, rope_tabs):
    M = B * T
    tm = _pick(M, (512, 256, 128))
    cos, sin = rope_tabs
    qkv = _norm_matmul(x2, q['mix_norm'][layer][None, :], q['w_qkv'], tm=tm, tn=QKV_COLS, out_dtype=BF16,
                       rope=(cos, sin, ATT_HEADS + ATT_KV_HEADS))
    o = _attention(qkv.reshape(B, T, QKV_COLS), q['att_sink'][0])
    return _matmul_res([o.reshape(M, ATT_HEADS * ATT_DIM)], q['w_o'], x2, tm=tm, tn=D_MODEL)


def _trunk(x, q):
    B, T, D = x.shape
    M = B * T
    x2 = x.reshape(M, D)
    rope_tabs = _rope_tables(T)
    tm = _pick(T, (1024, 512, 256, 128))
    for layer in range(DEPTH):
        if layer % 2 == 0:
            x2 = _mixer_layer(x2, q, B, T, layer)
        else:
            x2 = _attention_layer(x2, q, B, T, layer, rope_tabs)
        x2 = _ffn(x2, q['ffn_norm'][layer][None, :], q['w_up'], q['ffn_conv_w'][layer],
                  q['ffn_conv_b'][layer][None, :], q['w_down'], q['final_norm'][None, :],
                  layer=layer, seq_len=T, tm=tm, tf=512, final_norm=(layer == DEPTH - 1))
    return x2.reshape(B, T, D)


def kernel(x_prompt, x_sample, mix_norm, ab_w_in, hgrn_lb, hgrn_onorm, rwkv_mu, rwkv_w0, rwkv_w2, rwkv_a0, rwkv_a2, rwkv_g2, rwkv_kk, rwkv_ka, rwkv_rk, rwkv_ln_w, rwkv_ln_b, ab_w_out, att_w_qkv, att_sink, att_w_o, ffn_norm, ffn_w_up, ffn_conv_w, ffn_conv_b, ffn_w_down, final_norm):
    p = {
        'mix_norm': mix_norm, 'ab_w_in': ab_w_in, 'hgrn_lb': hgrn_lb, 'hgrn_onorm': hgrn_onorm,
        'rwkv_mu': rwkv_mu, 'rwkv_w0': rwkv_w0, 'rwkv_w2': rwkv_w2, 'rwkv_a0': rwkv_a0, 'rwkv_a2': rwkv_a2,
        'rwkv_g2': rwkv_g2, 'rwkv_kk': rwkv_kk, 'rwkv_ka': rwkv_ka, 'rwkv_rk': rwkv_rk,
        'rwkv_ln_w': rwkv_ln_w, 'rwkv_ln_b': rwkv_ln_b, 'ab_w_out': ab_w_out,
        'att_w_qkv': att_w_qkv, 'att_sink': att_sink, 'att_w_o': att_w_o,
        'ffn_norm': ffn_norm, 'ffn_w_up': ffn_w_up, 'ffn_conv_w': ffn_conv_w, 'ffn_conv_b': ffn_conv_b,
        'ffn_w_down': ffn_w_down, 'final_norm': final_norm,
    }
    q = _prepare_params(p)
    return (_trunk(x_prompt, q), _trunk(x_sample, q))
```

```python
import functools

import jax
import jax.numpy as jnp
from jax import lax
from jax.experimental import pallas as pl
from jax.experimental.pallas import tpu as pltpu

F32 = jnp.float32
BF16 = jnp.bfloat16

D_MODEL = 2048
DEPTH = 2
HGRN_DIM = 128
HGRN_HEADS = 8
HGRN_WIDTH = 1024
RWKV_DIM = 64
RWKV_HEADS = 16
RWKV_WIDTH = 1024
DECAY_LORA = 64
AAA_LORA = 64
GATE_LORA = 160
LORA_COLS = 2 * DECAY_LORA + AAA_LORA + GATE_LORA
LORA_PAD = 512
LORA_SLAB = 128
MIX_A_COLS = 5 * HGRN_WIDTH
MIX_COLS = MIX_A_COLS + 3 * RWKV_WIDTH + LORA_COLS
MIX_COLS_PAD = MIX_A_COLS + 3 * RWKV_WIDTH + LORA_PAD
ATT_DIM = 128
ATT_HEADS = 16
ATT_KV_HEADS = 4
ATT_GROUP = 4
QKV_COLS = (ATT_HEADS + 2 * ATT_KV_HEADS) * ATT_DIM
BLOCK = 128
ROPE_THETA = 10000.0
D_FF = 5632
RMS_EPS = 1e-6
GN_EPS = 64e-5
CHUNK = 64
PAIR = 2 * RWKV_DIM
N_PAIRS = RWKV_WIDTH // PAIR
LOG2_E = 1.4426950408889634
EXP_NEG_HALF = 0.6065306597126334
RWKV_GROUP = 8

V7X_VMEM_BYTES = 64 * 1024 * 1024
VMEM_LIMIT = 56 * 1024 * 1024


def _cparams(sem, vmem_limit=VMEM_LIMIT):
    return pltpu.CompilerParams(dimension_semantics=sem, vmem_limit_bytes=vmem_limit)


def _mm(a, b):
    return jnp.dot(a.astype(BF16), b.astype(BF16), preferred_element_type=F32)


def _mm_nt(a, b):
    return lax.dot_general(a.astype(BF16), b.astype(BF16), (((1,), (1,)), ((), ())),
                           preferred_element_type=F32)


def _mm_tn(a, b):
    return lax.dot_general(a.astype(BF16), b.astype(BF16), (((0,), (0,)), ((), ())),
                           preferred_element_type=F32)


def _cumsum_rows(x, rev):
    n = x.shape[0]
    row = lax.broadcasted_iota(jnp.int32, (n, 1), 0)
    s = 1
    while s < n:
        if rev:
            x = x + jnp.where(row < n - s, pltpu.roll(x, n - s, axis=0), 0.0)
        else:
            x = x + jnp.where(row >= s, pltpu.roll(x, s, axis=0), 0.0)
        s *= 2
    return x


_mm_lora = _mm
_mm_neumann = _mm


def _sigmoid(x):
    return 1.0 / (1.0 + jnp.exp(-x))


def _norm_matmul_body(x_ref, g_ref, w_ref, o_ref, h_ref):
    @pl.when(pl.program_id(1) == 0)
    def _():
        x = x_ref[...]
        ms = jnp.mean(x * x, axis=-1, keepdims=True)
        h_ref[...] = (x * lax.rsqrt(ms + RMS_EPS) * g_ref[...]).astype(BF16)

    o_ref[...] = jnp.dot(h_ref[...], w_ref[...], preferred_element_type=F32).astype(o_ref.dtype)


def _norm_matmul_rope_body(x_ref, g_ref, w_ref, cos_ref, sin_ref, o_ref, h_ref, *, n_rope, tn):
    x = x_ref[...]
    ms = jnp.mean(x * x, axis=-1, keepdims=True)
    h_ref[...] = (x * lax.rsqrt(ms + RMS_EPS) * g_ref[...]).astype(BF16)
    cos = cos_ref[...]
    sin = sin_ref[...]
    group = 4 * ATT_DIM
    for gi in range(tn // group):
        acc = jnp.dot(h_ref[...], w_ref[:, gi * group:(gi + 1) * group], preferred_element_type=F32)
        for hh in range(group // ATT_DIM):
            a = acc[:, hh * ATT_DIM:(hh + 1) * ATT_DIM]
            head = gi * (group // ATT_DIM) + hh
            if head < n_rope:
                a = a * cos + pltpu.roll(a, ATT_DIM // 2, axis=1) * sin
            o_ref[:, head * ATT_DIM:(head + 1) * ATT_DIM] = a.astype(o_ref.dtype)


def _norm_matmul(x, g, w, *, tm, tn, out_dtype, rope=None):
    M, K = x.shape
    N = w.shape[1]
    grid = (M // tm, N // tn)
    in_specs = [pl.BlockSpec((tm, K), lambda i, j: (i, 0)),
                pl.BlockSpec((1, K), lambda i, j: (0, 0)),
                pl.BlockSpec((K, tn), lambda i, j: (0, j))]
    args = [x, g, w]
    blocks = 2 * (tm * K * 4 + K * tn * 2 + tm * tn * jnp.dtype(out_dtype).itemsize) + tm * K * 2
    vmem_limit = max(VMEM_LIMIT, min(int(blocks * 1.08), V7X_VMEM_BYTES - (2 << 20)))
    if rope is None:
        body = _norm_matmul_body
    else:
        cos, sin, n_rope = rope
        tiles_per_seq = cos.shape[0] // tm
        in_specs += [pl.BlockSpec((tm, ATT_DIM), lambda i, j: (i % tiles_per_seq, 0)),
                     pl.BlockSpec((tm, ATT_DIM), lambda i, j: (i % tiles_per_seq, 0))]
        args += [cos, sin]
        body = functools.partial(_norm_matmul_rope_body, n_rope=n_rope, tn=tn)
    return pl.pallas_call(
        body,
        out_shape=jax.ShapeDtypeStruct((M, N), out_dtype),
        grid=grid,
        in_specs=in_specs,
        out_specs=pl.BlockSpec((tm, tn), lambda i, j: (i, j)),
        scratch_shapes=[pltpu.VMEM((tm, K), BF16)],
        compiler_params=_cparams(("parallel", "arbitrary"), vmem_limit),
        name="norm_matmul" if rope is None else "norm_matmul_rope",
    )(*args)


def _matmul_res_body(*refs, n_parts):
    a_refs = refs[:n_parts]
    w_ref, r_ref, o_ref = refs[n_parts:]
    acc = r_ref[...]
    k0 = 0
    for a_ref in a_refs:
        kw = a_ref.shape[1]
        acc = acc + jnp.dot(a_ref[...], w_ref[k0:k0 + kw, :], preferred_element_type=F32)
        k0 += kw
    o_ref[...] = acc


def _matmul_res(a_parts, w, res, *, tm, tn):
    M, N = res.shape
    K = w.shape[0]
    in_specs = [pl.BlockSpec((tm, a.shape[1]), lambda i, j: (i, 0)) for a in a_parts]
    in_specs += [pl.BlockSpec((K, tn), lambda i, j: (0, j)),
                 pl.BlockSpec((tm, tn), lambda i, j: (i, j))]
    return pl.pallas_call(
        functools.partial(_matmul_res_body, n_parts=len(a_parts)),
        out_shape=jax.ShapeDtypeStruct((M, N), F32),
        grid=(M // tm, N // tn),
        in_specs=in_specs,
        out_specs=pl.BlockSpec((tm, tn), lambda i, j: (i, j)),
        compiler_params=_cparams(("parallel", "arbitrary")),
        name="matmul_res",
    )(*a_parts, w, res)


def _hgrn_body(*refs, rev, n_chunks, final):
    if final:
        q_ref, i_ref, z_ref, lb_ref, ofwd_ref, g_ref, onorm_ref, o_ref, st_ref = refs
    else:
        q_ref, i_ref, z_ref, lb_ref, o_ref, st_ref = refs

    @pl.when(pl.program_id(1) == 0)
    def _():
        st_ref[...] = jnp.zeros_like(st_ref)

    lb = lb_ref[...]
    row = lax.broadcasted_iota(jnp.int32, (CHUNK, CHUNK), 0)
    col = lax.broadcasted_iota(jnp.int32, (CHUNK, CHUNK), 1)
    keep = (row <= col) if rev else (row >= col)
    last = 0 if rev else CHUNK - 1

    head_slices = [slice(h * HGRN_DIM, (h + 1) * HGRN_DIM) for h in range(HGRN_HEADS)]
    sts = [st_ref[h] for h in range(HGRN_HEADS)]
    order = range(n_chunks - 1, -1, -1) if rev else range(n_chunks)
    for ci in order:
        sl = slice(ci * CHUNK, (ci + 1) * CHUNK)
        f = lb + (1.0 - lb) * _sigmoid(z_ref[sl, :])
        kk = 1.0 - f
        b = _cumsum_rows(jnp.log(f), rev)
        b_last = b[last:last + 1, :]
        q_d = q_ref[sl, :] * jnp.exp(b)
        k_d = kk * jnp.exp(-b)
        k_u = kk * jnp.exp(b_last - b)
        dec = jnp.exp(b_last)
        v = i_ref[sl, :]
        att = [jnp.where(keep, _mm_nt(q_d[:, hs], k_d[:, hs]), 0.0) for hs in head_slices]
        outs = [_mm(att[h], v[:, hs]) + _mm_nt(q_d[:, hs], sts[h]) for h, hs in enumerate(head_slices)]
        upd = [_mm_tn(v[:, hs], k_u[:, hs]) for hs in head_slices]
        sts = [sts[h] * dec[:, hs] + upd[h] for h, hs in enumerate(head_slices)]
        if final:
            g = g_ref[sl, :]
            silu_g = g * _sigmoid(g)
            normed = []
            for h, hs in enumerate(head_slices):
                oa = outs[h] + ofwd_ref[sl, hs].astype(F32)
                normed.append(oa * lax.rsqrt(jnp.mean(oa * oa, axis=-1, keepdims=True) + RMS_EPS))
            o_ref[sl, :] = (jnp.concatenate(normed, axis=1) * onorm_ref[...] * silu_g).astype(o_ref.dtype)
        else:
            o_ref[sl, :] = jnp.concatenate(outs, axis=1).astype(o_ref.dtype)
    st_ref[...] = jnp.stack(sts, axis=0)


def _hgrn_dir(proj, lb, *, rev, tb, ofwd=None, onorm=None):
    B, T, _ = proj.shape
    nt = T // tb
    final = ofwd is not None

    def tmap(n):
        return (nt - 1 - n) if rev else n

    def col(c):
        return pl.BlockSpec((None, tb, HGRN_WIDTH), lambda b, n: (b, tmap(n), c))

    vec = pl.BlockSpec((1, HGRN_WIDTH), lambda b, n: (0, 0))
    in_specs = [col(0), col(1), col(3 if rev else 2), vec]
    args = [proj, proj, proj, lb]
    if final:
        in_specs += [col(0), col(4), vec]
        args += [ofwd, proj, onorm]
    return pl.pallas_call(
        functools.partial(_hgrn_body, rev=rev, n_chunks=tb // CHUNK, final=final),
        out_shape=jax.ShapeDtypeStruct((B, T, HGRN_WIDTH), BF16),
        grid=(B, nt),
        in_specs=in_specs,
        out_specs=pl.BlockSpec((None, tb, HGRN_WIDTH), lambda b, n: (b, tmap(n), 0)),
        scratch_shapes=[pltpu.VMEM((HGRN_HEADS, HGRN_DIM, HGRN_DIM), F32)],
        compiler_params=_cparams(("parallel", "arbitrary")),
        name="hgrn_bwd" if rev else "hgrn_fwd",
    )(*args)


def _head_sum(x):
    m0 = lax.broadcasted_iota(jnp.int32, (x.shape[0], PAIR), 1) < RWKV_DIM
    outs = []
    for p in range(N_PAIRS):
        xs = x[:, p * PAIR:(p + 1) * PAIR]
        s0 = jnp.sum(jnp.where(m0, xs, 0.0), axis=-1, keepdims=True)
        s1 = jnp.sum(jnp.where(m0, 0.0, xs), axis=-1, keepdims=True)
        outs.append(jnp.where(m0, s0, s1))
    return jnp.concatenate(outs, axis=1)


def _rwkv_prep_body(r_ref, k_ref, v_ref, l_ref,
                    rp_ref, kp_ref, vp_ref, lp_ref, rn_ref, kn_ref, vn_ref, ln_ref,
                    mu_r_ref, mu_k_ref, mu_v_ref, mu_l_ref,
                    w0_ref, w2f_ref, w2b_ref, a0_ref, a2_ref, g2_ref, kkw_ref, kaw_ref, rk_ref,
                    ro_ref, ko_ref, vo_ref, kko_ref, bvo_ref, lwf_ref, lwb_ref, go_ref, bo_ref,
                    *, tb):
    n = pl.program_id(1)
    nt = pl.num_programs(1)
    has_prev = jnp.where(n > 0, 1.0, 0.0).astype(F32)
    has_next = jnp.where(n < nt - 1, 1.0, 0.0).astype(F32)
    rows8 = lax.broadcasted_iota(jnp.int32, (8, 1), 0)
    first8 = rows8 == 0
    last8 = rows8 == 7

    def shift(x_ref, p_ref, n_ref, mu_ref):
        x = x_ref[...]
        down = pltpu.roll(x, 1, axis=0)
        up = pltpu.roll(x, tb - 1, axis=0)
        prev = jnp.concatenate([jnp.where(first8, p_ref[7:8, :] * has_prev, down[0:8]), down[8:]], axis=0)
        nxt = jnp.concatenate([up[:tb - 8], jnp.where(last8, n_ref[0:1, :] * has_next, up[tb - 8:])], axis=0)
        mu = mu_ref[...]
        return x * (1.0 - mu) + (prev + nxt) * (0.5 * mu)

    r = shift(r_ref, rp_ref, rn_ref, mu_r_ref)
    k = shift(k_ref, kp_ref, kn_ref, mu_k_ref)
    v = shift(v_ref, vp_ref, vn_ref, mu_v_ref)
    lo = shift(l_ref, lp_ref, ln_ref, mu_l_ref)
    wd = jnp.tanh(lo[:, 0:LORA_SLAB])
    ag = lo[:, LORA_SLAB:3 * LORA_SLAB]

    def log_decay(w0, w2):
        u = w0 + _mm_lora(wd, w2)
        return -EXP_NEG_HALF * _sigmoid(u)

    lwf_ref[...] = log_decay(w0_ref[0:1, :], w2f_ref[...])
    lwb_ref[...] = log_decay(w0_ref[1:2, :], w2b_ref[...])
    a = _sigmoid(a0_ref[...] + _mm_lora(ag[:, 0:LORA_SLAB], a2_ref[...]))
    go_ref[...] = _mm_lora(_sigmoid(ag), g2_ref[...]).astype(go_ref.dtype)

    kk = k * kkw_ref[...]
    norm = jnp.maximum(jnp.sqrt(_head_sum(kk * kk)), 1e-12)
    kk = kk / norm
    k2 = k * (1.0 + (a - 1.0) * kaw_ref[...])
    ro_ref[...] = r.astype(ro_ref.dtype)
    ko_ref[...] = k2.astype(ko_ref.dtype)
    vo_ref[...] = v.astype(vo_ref.dtype)
    kko_ref[...] = kk.astype(kko_ref.dtype)
    bvo_ref[...] = (kk * a).astype(bvo_ref.dtype)
    bo_ref[...] = (_head_sum(r * k2 * rk_ref[...]) * v).astype(bo_ref.dtype)


def _rwkv_prep(proj, mu, w0, w2, a0, a2, g2, kkw, kaw, rk, *, tb):
    B, T, _ = proj.shape
    nt = T // tb
    hb = tb // 8
    last8 = T // 8 - 1

    def main(width, c):
        return pl.BlockSpec((None, tb, width), lambda b, n: (b, n, c))

    def prev(width, c):
        return pl.BlockSpec((None, 8, width), lambda b, n: (b, jnp.maximum(n * hb - 1, 0), c))

    def nxt(width, c):
        return pl.BlockSpec((None, 8, width), lambda b, n: (b, jnp.minimum((n + 1) * hb, last8), c))

    def full(a):
        return pl.BlockSpec(a.shape, lambda b, n: (0,) * a.ndim)

    W = RWKV_WIDTH
    c_l = (MIX_A_COLS + 3 * W) // LORA_PAD
    zpad = lambda a, before, total: jnp.pad(a, ((before, total - before - a.shape[0]), (0, 0)))
    params = [mu[:, 0:W], mu[:, W:2 * W], mu[:, 2 * W:3 * W], mu[:, 3 * W:],
              w0, zpad(w2[0], 0, LORA_SLAB), zpad(w2[1], DECAY_LORA, LORA_SLAB), a0,
              zpad(a2, 0, LORA_SLAB), zpad(g2, AAA_LORA, 2 * LORA_SLAB), kkw, kaw, rk]
    in_specs = ([main(W, 5), main(W, 6), main(W, 7), main(LORA_PAD, c_l),
                 prev(W, 5), prev(W, 6), prev(W, 7), prev(LORA_PAD, c_l),
                 nxt(W, 5), nxt(W, 6), nxt(W, 7), nxt(LORA_PAD, c_l)]
                + [full(p) for p in params])
    out_dtypes = [BF16] * 5 + [F32, F32, BF16, BF16]
    return pl.pallas_call(
        functools.partial(_rwkv_prep_body, tb=tb),
        out_shape=[jax.ShapeDtypeStruct((B, T, W), dt) for dt in out_dtypes],
        grid=(B, nt),
        in_specs=in_specs,
        out_specs=[pl.BlockSpec((None, tb, W), lambda b, n: (b, n, 0))] * 9,
        compiler_params=_cparams(("parallel", "parallel")),
        name="rwkv_prep",
    )(*([proj] * 12), *params)


def _bd(x):
    lane = lax.broadcasted_iota(jnp.int32, x.shape, 1) % PAIR
    m0 = lane < RWKV_DIM
    return jnp.concatenate([jnp.where(m0, x, 0.0), jnp.where(m0, 0.0, x)], axis=0)

def _neumann_inverse(mats):
    n, w = mats[0].shape
    eye = jnp.where(lax.broadcasted_iota(jnp.int32, (n, w), 0) == lax.broadcasted_iota(jnp.int32, (n, w), 1) % n,
                    1.0, 0.0).astype(F32)
    ps = [eye + a for a in mats]
    aks = [_mm_neumann(a, _bd(a)) for a in mats]
    for _ in range(4):
        ss = [_mm_neumann(ak, _bd(jnp.concatenate([ak, p], axis=1))) for p, ak in zip(ps, aks)]
        ps = [p + s[:, w:] for p, s in zip(ps, ss)]
        aks = [s[:, :w] for s in ss]
    return [p + _mm_neumann(ak, _bd(p)) for p, ak in zip(ps, aks)]


def _rwkv_scan_body(*refs, rev, final, n_chunks, group):
    h_ref = refs[-1]

    @pl.when(pl.program_id(1) == 0)
    def _():
        h_ref[...] = jnp.zeros_like(h_ref)

    n_groups = n_chunks // group
    if n_groups == 1:
        _rwkv_group(refs, 0, rev=rev, final=final, group=group)
    else:
        def body(i, carry):
            gi = (n_groups - 1 - i) if rev else i
            _rwkv_group(refs, gi * (group * CHUNK), rev=rev, final=final, group=group)
            return carry

        lax.fori_loop(0, n_groups, body, 0)


def _rwkv_group(refs, base, *, rev, final, group):
    if final:
        (r_ref, k_ref, v_ref, kk_ref, bv_ref, lw_ref, yf_ref, g_ref, bonus_ref, lnw_ref, lnb_ref,
         o_ref, h_ref) = refs
    else:
        r_ref, k_ref, v_ref, kk_ref, bv_ref, lw_ref, o_ref, h_ref = refs

    C = CHUNK
    last = 0 if rev else C - 1
    t_row = lax.broadcasted_iota(jnp.int32, (C, 2 * PAIR), 0)
    t_col = lax.broadcasted_iota(jnp.int32, (C, 2 * PAIR), 1) % C
    strict = (t_row < t_col) if rev else (t_row > t_col)
    incl = (t_row <= t_col) if rev else (t_row >= t_col)
    ch_row = lax.broadcasted_iota(jnp.int32, (PAIR, 2 * PAIR), 0)
    ch_col = lax.broadcasted_iota(jnp.int32, (PAIR, 2 * PAIR), 1) % PAIR
    same_head = (ch_row // RWKV_DIM) == (ch_col // RWKV_DIM)
    eye = (lax.broadcasted_iota(jnp.int32, (PAIR, PAIR), 0)
           == lax.broadcasted_iota(jnp.int32, (PAIR, PAIR), 1))
    zeros = jnp.zeros((C, PAIR), F32)
    pairs = range(N_PAIRS)
    cat = jnp.concatenate

    def rows(c):
        if isinstance(base, int):
            return slice(base + c * C, base + (c + 1) * C)
        return pl.ds(pl.multiple_of(base + c * C, C), C)

    def slabs(x):
        return [x[:, p * PAIR:(p + 1) * PAIR] for p in pairs]

    at, rt, bt, kt, vv, bh, kh, w_tot = [], [], [], [], [], [], [], []
    for c in range(group):
        sl = rows(c)
        lw = lw_ref[sl, :]
        c_inc = _cumsum_rows(lw, rev)
        c_exc = c_inc - lw
        c_tot = c_inc[last:last + 1, :]
        e_ninc = jnp.exp(-c_inc)
        e_hat = jnp.exp(c_tot - c_inc)
        kk = kk_ref[sl, :].astype(F32)
        bv = bv_ref[sl, :].astype(F32)
        k2 = k_ref[sl, :].astype(F32)
        w_tot += slabs(jnp.exp(c_tot))
        at += slabs(-kk * jnp.exp(c_exc))
        rt += slabs(r_ref[sl, :].astype(F32) * jnp.exp(c_inc))
        bt += slabs(bv * e_ninc)
        kt += slabs(k2 * e_ninc)
        vv += slabs(v_ref[sl, :].astype(F32))
        bh += slabs(bv * e_hat)
        kh += slabs(k2 * e_hat)

    items = range(group * N_PAIRS)
    pm = [_mm_nt(cat([at[i], rt[i]], axis=0), cat([_bd(bt[i]), _bd(kt[i])], axis=0)) for i in items]
    a_abk = [jnp.where(strict, pm[i][:C, :], 0.0) for i in items]
    a_rbk = [jnp.where(incl, pm[i][C:, :], 0.0) for i in items]
    tinv = _neumann_inverse([a_abk[i][:, :PAIR] for i in items])
    x1 = [_mm(a_abk[i][:, PAIR:], _bd(vv[i])) for i in items]
    z = [_mm(tinv[i], _bd(cat([at[i], x1[i]], axis=1))) for i in items]
    w2 = [cat([z[i], cat([zeros, vv[i]], axis=1)], axis=0) for i in items]
    mg = [jnp.where(same_head, _mm_tn(cat([bh[i], kh[i]], axis=0), w2[i]), 0.0) for i in items]
    ry = [_mm(a_rbk[i], cat([_bd(z[i]), _bd(w2[i][C:])], axis=0)) for i in items]
    lhs = [cat([rt[i] + ry[i][:, :PAIR], mg[i][:, :PAIR] + jnp.where(eye, w_tot[i], 0.0)], axis=0)
           for i in items]

    hs = [h_ref[p] for p in pairs]
    for c in (range(group - 1, -1, -1) if rev else range(group)):
        sl = rows(c)
        yh = [_mm(lhs[c * N_PAIRS + p], hs[p]) for p in pairs]
        hs = [yh[p][C:] + mg[c * N_PAIRS + p][:, PAIR:] for p in pairs]
        y = cat([yh[p][:C] + ry[c * N_PAIRS + p][:, PAIR:] for p in pairs], axis=1)
        if final:
            y = y + yf_ref[sl, :]
            mean = _head_sum(y) * (1.0 / RWKV_DIM)
            yc = y - mean
            var = _head_sum(yc * yc) * (1.0 / RWKV_DIM)
            yn = yc * lax.rsqrt(var + GN_EPS) * lnw_ref[...] + lnb_ref[...]
            o_ref[sl, :] = ((yn + bonus_ref[sl, :].astype(F32)) * g_ref[sl, :].astype(F32)).astype(o_ref.dtype)
        else:
            o_ref[sl, :] = y
    h_ref[...] = jnp.stack(hs, axis=0)


def _rwkv_dir(r, k2, v, kk, bv, lw, *, rev, tb, fin=None):
    B, T, W = r.shape
    nc = T // tb

    def tmap(n):
        return (nc - 1 - n) if rev else n

    blk = pl.BlockSpec((None, tb, W), lambda b, n: (b, tmap(n), 0))
    vec = pl.BlockSpec((1, W), lambda b, n: (0, 0))
    in_specs = [blk] * 6
    args = [r, k2, v, kk, bv, lw]
    if fin is not None:
        in_specs += [blk, blk, blk, vec, vec]
        args += list(fin)
    return pl.pallas_call(
        functools.partial(_rwkv_scan_body, rev=rev, final=fin is not None, n_chunks=tb // CHUNK,
                          group=min(RWKV_GROUP, tb // CHUNK)),
        out_shape=jax.ShapeDtypeStruct((B, T, W), BF16 if fin is not None else F32),
        grid=(B, nc),
        in_specs=in_specs,
        out_specs=blk,
        scratch_shapes=[pltpu.VMEM((N_PAIRS, PAIR, PAIR), F32)],
        compiler_params=_cparams(("parallel", "arbitrary")),
        name="rwkv_bwd" if rev else "rwkv_fwd",
    )(*args)


def _attn_body(sink_ref, q_ref, kp_ref, kc_ref, kn_ref, vp_ref, vc_ref, vn_ref, o_ref):
    n = pl.program_id(1)
    nb = pl.num_programs(1)
    r = lax.broadcasted_iota(jnp.int32, (BLOCK, 3 * BLOCK), 0)
    c = lax.broadcasted_iota(jnp.int32, (BLOCK, 3 * BLOCK), 1)
    d = c - BLOCK - r
    lo = jnp.where(n > 0, 0, BLOCK)
    hi = jnp.where(n < nb - 1, 3 * BLOCK, 2 * BLOCK)
    valid = (d >= -BLOCK) & (d <= BLOCK) & (c >= lo) & (c < hi)
    scale = ATT_DIM ** -0.5
    for kh in range(ATT_KV_HEADS):
        ks = slice(kh * ATT_DIM, (kh + 1) * ATT_DIM)
        kw = jnp.concatenate([kp_ref[:, ks], kc_ref[:, ks], kn_ref[:, ks]], axis=0)
        vw = jnp.concatenate([vp_ref[:, ks], vc_ref[:, ks], vn_ref[:, ks]], axis=0)
        heads = [kh * ATT_GROUP + g for g in range(ATT_GROUP)]
        qg = jnp.concatenate([q_ref[:, h * ATT_DIM:(h + 1) * ATT_DIM] for h in heads], axis=0)
        s_all = lax.dot_general(qg, kw, (((1,), (1,)), ((), ())), preferred_element_type=F32)
        ps, inv_dens = [], []
        for g, h in enumerate(heads):
            sg = s_all[g * BLOCK:(g + 1) * BLOCK, :]
            s = jnp.concatenate([jnp.where(valid[:, :BLOCK], sg[:, :BLOCK], -jnp.inf), sg[:, BLOCK:2 * BLOCK],
                                 jnp.where(valid[:, 2 * BLOCK:], sg[:, 2 * BLOCK:], -jnp.inf)], axis=1)
            sk = sink_ref[h] * (1.0 / scale)
            m = jnp.maximum(jnp.max(s, axis=-1, keepdims=True), sk)
            e = jnp.exp2((s - m) * (scale * LOG2_E))
            inv_dens.append(1.0 / (jnp.sum(e, axis=-1, keepdims=True) + jnp.exp2((sk - m) * (scale * LOG2_E))))
            ps.append(e.astype(BF16))
        o_all = jnp.dot(jnp.concatenate(ps, axis=0), vw, preferred_element_type=F32)
        for g, h in enumerate(heads):
            o_ref[:, h * ATT_DIM:(h + 1) * ATT_DIM] = (
                o_all[g * BLOCK:(g + 1) * BLOCK, :] * inv_dens[g]).astype(o_ref.dtype)


def _attention(qkv, sink):
    B, T, _ = qkv.shape
    nb = T // BLOCK
    kvw = ATT_KV_HEADS * ATT_DIM
    kc = (ATT_HEADS * ATT_DIM) // kvw
    vc = kc + 1

    def blk(cidx, off):
        def imap(b, n):
            return (b, jnp.clip(n + off, 0, nb - 1), cidx)
        return pl.BlockSpec((None, BLOCK, kvw), imap)

    return pl.pallas_call(
        _attn_body,
        out_shape=jax.ShapeDtypeStruct((B, T, ATT_HEADS * ATT_DIM), BF16),
        grid=(B, nb),
        in_specs=[pl.BlockSpec(memory_space=pltpu.SMEM),
                  pl.BlockSpec((None, BLOCK, ATT_HEADS * ATT_DIM), lambda b, n: (b, n, 0)),
                  blk(kc, -1), blk(kc, 0), blk(kc, 1), blk(vc, -1), blk(vc, 0), blk(vc, 1)],
        out_specs=pl.BlockSpec((None, BLOCK, ATT_HEADS * ATT_DIM), lambda b, n: (b, n, 0)),
        compiler_params=_cparams(("parallel", "parallel")),
        name="attention",
    )(sink, qkv, qkv, qkv, qkv, qkv, qkv, qkv)


def _ffn_body(x_ref, xp_ref, xn_ref, g_ref, wg_ref, wv_ref, cw_ref, cb_ref, wd_ref, fg_ref, o_ref, h_ref,
              *, tm, tiles_per_seq, final_norm):
    i = pl.program_id(0)
    f = pl.program_id(1)
    nf = pl.num_programs(1)

    t = i % tiles_per_seq
    has_prev = jnp.where(t > 0, 1.0, 0.0).astype(F32)
    has_next = jnp.where(t < tiles_per_seq - 1, 1.0, 0.0).astype(F32)
    rows = lax.broadcasted_iota(jnp.int32, (tm, 1), 0)

    def activation(gm, g_halo, val):
        g_prev = jnp.where(rows == 0, g_halo[7:8, :] * has_prev, pltpu.roll(gm, 1, axis=0))
        g_next = jnp.where(rows == tm - 1, g_halo[8:9, :] * has_next, pltpu.roll(gm, tm - 1, axis=0))
        gate = g_prev * cw_ref[0:1, :] + gm * cw_ref[1:2, :] + g_next * cw_ref[2:3, :] + cb_ref[...]
        return (gate * _sigmoid(gate) * val).astype(BF16)

    @pl.when(f == 0)
    def _():
        def norm(x):
            ms = jnp.mean(x * x, axis=-1, keepdims=True)
            return (x * lax.rsqrt(ms + RMS_EPS) * g_ref[...]).astype(BF16)

        n_chunks = 4
        rc = tm // n_chunks
        h_halo = norm(jnp.concatenate([xp_ref[...], xn_ref[...]], axis=0))
        h_ref[tm:tm + 16, :] = h_halo
        gms, vals, g_halo = [], [], None
        for c in range(n_chunks):
            h_c = norm(x_ref[c * rc:(c + 1) * rc, :])
            h_ref[c * rc:(c + 1) * rc, :] = h_c
            if c == n_chunks - 1:
                ge = jnp.dot(jnp.concatenate([h_c, h_halo], axis=0), wg_ref[...], preferred_element_type=F32)
                gms.append(ge[:rc])
                g_halo = ge[rc:]
            else:
                gms.append(jnp.dot(h_c, wg_ref[...], preferred_element_type=F32))
            vals.append(jnp.dot(h_c, wv_ref[...], preferred_element_type=F32))
        act = activation(jnp.concatenate(gms, axis=0), g_halo, jnp.concatenate(vals, axis=0))
        o_ref[...] = x_ref[...] + jnp.dot(act, wd_ref[...], preferred_element_type=F32)

    @pl.when(f > 0)
    def _():
        ge = jnp.dot(h_ref[...], wg_ref[...], preferred_element_type=F32)
        val = jnp.dot(h_ref[0:tm, :], wv_ref[...], preferred_element_type=F32)
        act = activation(ge[0:tm, :], ge[tm:, :], val)
        o_ref[...] += jnp.dot(act, wd_ref[...], preferred_element_type=F32)

    if final_norm:
        @pl.when(f == nf - 1)
        def _():
            y = o_ref[...]
            ms = jnp.mean(y * y, axis=-1, keepdims=True)
            o_ref[...] = y * lax.rsqrt(ms + RMS_EPS) * fg_ref[...]


def _ffn(x, g, w_up, conv_w, conv_b, w_down, final_g, *, layer, seq_len, tm, tf, final_norm):
    M, D = x.shape
    nf = D_FF // tf
    hb = tm // 8
    last8 = M // 8 - 1
    in_specs = [
        pl.BlockSpec((tm, D), lambda i, f: (i, 0)),
        pl.BlockSpec((8, D), lambda i, f: (jnp.maximum(i * hb - 1, 0), 0)),
        pl.BlockSpec((8, D), lambda i, f: (jnp.minimum((i + 1) * hb, last8), 0)),
        pl.BlockSpec((1, D), lambda i, f: (0, 0)),
        pl.BlockSpec((None, D, tf), lambda i, f: (layer, 0, f)),
        pl.BlockSpec((None, D, tf), lambda i, f: (layer, 0, nf + f)),
        pl.BlockSpec((3, tf), lambda i, f: (0, f)),
        pl.BlockSpec((1, tf), lambda i, f: (0, f)),
        pl.BlockSpec((None, tf, D), lambda i, f: (layer, f, 0)),
        pl.BlockSpec((1, D), lambda i, f: (0, 0)),
    ]
    return pl.pallas_call(
        functools.partial(_ffn_body, tm=tm, tiles_per_seq=seq_len // tm, final_norm=final_norm),
        out_shape=jax.ShapeDtypeStruct((M, D), F32),
        grid=(M // tm, nf),
        in_specs=in_specs,
        out_specs=pl.BlockSpec((tm, D), lambda i, f: (i, 0)),
        scratch_shapes=[pltpu.VMEM((tm + 16, D), BF16)],
        compiler_params=_cparams(("parallel", "arbitrary")),
        name="conv_ffn",
    )(x, x, x, g, w_up, w_up, conv_w, conv_b, w_down, final_g)


def _pick(n, prefs):
    for p in prefs:
        if n % p == 0:
            return p
    raise ValueError(f"no tile for {n}")


def _rope_tables(T):
    half = ATT_DIM // 2
    inv = ROPE_THETA ** (-jnp.arange(half, dtype=F32) / half)
    ang = jnp.arange(T, dtype=F32)[:, None] * inv[None, :]
    cos = jnp.cos(ang)
    sin = jnp.sin(ang)
    return jnp.concatenate([cos, cos], axis=1), jnp.concatenate([-sin, sin], axis=1)


def _prepare_params(p):
    q = dict(p)
    w_in = p['ab_w_in'][0]
    q['w_in'] = jnp.pad(w_in, ((0, 0), (0, MIX_COLS_PAD - MIX_COLS))).astype(BF16)
    mu = p['rwkv_mu'][0]
    q['mu'] = jnp.pad(mu, (0, LORA_PAD - LORA_COLS))[None, :]
    q['lb'] = jnp.cumsum(jax.nn.softmax(p['hgrn_lb'].astype(F32), axis=0), axis=0)
    q['w_out'] = p['ab_w_out'][0].astype(BF16)
    q['w_qkv'] = p['att_w_qkv'][0].astype(BF16)
    q['w_o'] = p['att_w_o'][0].astype(BF16)
    q['w_up'] = p['ffn_w_up'].astype(BF16)
    q['w_down'] = p['ffn_w_down'].astype(BF16)
    return q


def _mixer_layer(x2, q, B, T, layer):
    M = B * T
    tm = _pick(M, (512, 256, 128))
    proj = _norm_matmul(x2, q['mix_norm'][layer][None, :], q['w_in'], tm=_pick(M, (1024, 512, 256, 128)),
                        tn=MIX_COLS_PAD // 4, out_dtype=F32)
    proj = proj.reshape(B, T, MIX_COLS_PAD)
    lb = q['lb'][layer][None, :]
    tb = _pick(T, (256, 128, 64))
    tb_scan = _pick(T, (512, 256, 128, 64))
    o_fwd = _hgrn_dir(proj, lb, rev=False, tb=tb_scan)
    ya = _hgrn_dir(proj, lb, rev=True, tb=tb_scan, ofwd=o_fwd, onorm=q['hgrn_onorm'][0][None, :])

    r, k2, v, kk, bv, lwf, lwb, g, bonus = _rwkv_prep(
        proj, q['mu'], q['rwkv_w0'][0], q['rwkv_w2'][0], q['rwkv_a0'][0][None, :], q['rwkv_a2'][0],
        q['rwkv_g2'][0], q['rwkv_kk'][0][None, :], q['rwkv_ka'][0][None, :],
        q['rwkv_rk'][0].reshape(1, RWKV_WIDTH), tb=tb_scan)
    y_fwd = _rwkv_dir(r, k2, v, kk, bv, lwf, rev=False, tb=tb_scan)
    yb = _rwkv_dir(r, k2, v, kk, bv, lwb, rev=True, tb=tb_scan,
                   fin=(y_fwd, g, bonus, q['rwkv_ln_w'][0][None, :], q['rwkv_ln_b'][0][None, :]))
    return _matmul_res([ya.reshape(M, HGRN_WIDTH), yb.reshape(M, RWKV_WIDTH)], q['w_out'], x2,
                       tm=tm, tn=D_MODEL)


def _attention_layer(x2, q, B, T, layer, rope_tabs):
    M = B * T
    tm = _pick(M, (512, 256, 128))
    cos, sin = rope_tabs
    qkv = _norm_matmul(x2, q['mix_norm'][layer][None, :], q['w_qkv'], tm=tm, tn=QKV_COLS, out_dtype=BF16,
                       rope=(cos, sin, ATT_HEADS + ATT_KV_HEADS))
    o = _attention(qkv.reshape(B, T, QKV_COLS), q['att_sink'][0])
    return _matmul_res([o.reshape(M, ATT_HEADS * ATT_DIM)], q['w_o'], x2, tm=tm, tn=D_MODEL)


def _trunk(x, q):
    B, T, D = x.shape
    M = B * T
    x2 = x.reshape(M, D)
    rope_tabs = _rope_tables(T)
    tm = _pick(T, (1024, 512, 256, 128))
    for layer in range(DEPTH):
        if layer % 2 == 0:
            x2 = _mixer_layer(x2, q, B, T, layer)
        else:
            x2 = _attention_layer(x2, q, B, T, layer, rope_tabs)
        x2 = _ffn(x2, q['ffn_norm'][layer][None, :], q['w_up'], q['ffn_conv_w'][layer],
                  q['ffn_conv_b'][layer][None, :], q['w_down'], q['final_norm'][None, :],
                  layer=layer, seq_len=T, tm=tm, tf=512, final_norm=(layer == DEPTH - 1))
    return x2.reshape(B, T, D)


def kernel(x_prompt, x_sample, mix_norm, ab_w_in, hgrn_lb, hgrn_onorm, rwkv_mu, rwkv_w0, rwkv_w2, rwkv_a0, rwkv_a2, rwkv_g2, rwkv_kk, rwkv_ka, rwkv_rk, rwkv_ln_w, rwkv_ln_b, ab_w_out, att_w_qkv, att_sink, att_w_o, ffn_norm, ffn_w_up, ffn_conv_w, ffn_conv_b, ffn_w_down, final_norm):
    p = {
        'mix_norm': mix_norm, 'ab_w_in': ab_w_in, 'hgrn_lb': hgrn_lb, 'hgrn_onorm': hgrn_onorm,
        'rwkv_mu': rwkv_mu, 'rwkv_w0': rwkv_w0, 'rwkv_w2': rwkv_w2, 'rwkv_a0': rwkv_a0, 'rwkv_a2': rwkv_a2,
        'rwkv_g2': rwkv_g2, 'rwkv_kk': rwkv_kk, 'rwkv_ka': rwkv_ka, 'rwkv_rk': rwkv_rk,
        'rwkv_ln_w': rwkv_ln_w, 'rwkv_ln_b': rwkv_ln_b, 'ab_w_out': ab_w_out,
        'att_w_qkv': att_w_qkv, 'att_sink': att_sink, 'att_w_o': att_w_o,
        'ffn_norm': ffn_norm, 'ffn_w_up': ffn_w_up, 'ffn_conv_w': ffn_conv_w, 'ffn_conv_b': ffn_conv_b,
        'ffn_w_down': ffn_w_down, 'final_norm': final_norm,
    }
    q = _prepare_params(p)
    return (_trunk(x_prompt, q), _trunk(x_sample, q))
```
